```python
import jax, jax.numpy as jnp
from jax import lax
import numpy as np

D_MODEL = 1024
BATCH = 4
SEQ = 8192
DEPTH = 2

N_A_LAYERS = DEPTH // 2
N_B_LAYERS = DEPTH - N_A_LAYERS
NORM_EPS = 1e-6
RWKV_HEAD_DIM = 64
RWKV_HEADS = D_MODEL // RWKV_HEAD_DIM
DECAY_LORA = 64
AAA_LORA = 64
GATE_LORA = 128
GN_EPS = 64e-5
MLA_HEADS = 16
Q_LORA = 384
KV_LORA = 256
QK_NOPE = 64
QK_ROPE = 32
V_HEAD = 64
ROPE_THETA = 10000.0
ATTN_BLOCK = 128
N_EXPERTS = 16
N_GROUPS = 4
EXPERTS_PER_GROUP = N_EXPERTS // N_GROUPS
TOP_K = 2
D_EXPERT = 256
POS_OFFSET_MAX = 4096

kernel_name = 'hybrid_rwkv7_mla_yoco_grouped_moe'


def rmsnorm(x, g):
    xf = x.astype(jnp.float32)
    y = xf * lax.rsqrt(jnp.mean(xf * xf, axis=-1, keepdims=True) + NORM_EPS)
    return (y * g.astype(jnp.float32)).astype(x.dtype)


def ada_params(c, w, b, n):
    m = (jax.nn.silu(c) @ w + b)[:, None, :]
    return jnp.split(m, n, axis=-1)


def modulate(x, g, shift, scale):
    return rmsnorm(x, g) * (1.0 + scale) + shift


def rope_tables(positions):
    inv_freq = ROPE_THETA ** (-jnp.arange(0, QK_ROPE, 2, dtype=jnp.float32) / QK_ROPE)
    ang = positions.astype(jnp.float32)[..., None] * inv_freq
    return jnp.cos(ang), jnp.sin(ang)


def apply_rope(x, cos, sin):
    xf = x.astype(jnp.float32)
    x1, x2 = jnp.split(xf, 2, axis=-1)
    return jnp.concatenate([x1 * cos - x2 * sin, x2 * cos + x1 * sin], axis=-1).astype(x.dtype)


def wkv7_scan(r, w, k, v, a, b):
    bsz, _, nh, n = r.shape
    to_tm = lambda t: jnp.moveaxis(t.astype(jnp.float32), 1, 0)

    def step(state, inp):
        r_t, w_t, k_t, v_t, a_t, b_t = inp
        sa = jnp.einsum('bhij,bhj->bhi', state, a_t)
        state = (state * w_t[:, :, None, :] + sa[..., None] * b_t[:, :, None, :]
                 + v_t[..., None] * k_t[:, :, None, :])
        return state, jnp.einsum('bhij,bhj->bhi', state, r_t)

    s0 = jnp.zeros((bsz, nh, n, n), jnp.float32)
    _, ys = lax.scan(step, s0, (to_tm(r), to_tm(w), to_tm(k), to_tm(v), to_tm(a), to_tm(b)))
    return jnp.moveaxis(ys, 0, 1)


def rwkv7_time_mix(h, mu, w_rkv, w0, w1, w2, a0, a1, a2, g1, g2, k_k, k_a, r_k, lnx_w, lnx_b, w_o):
    bsz, s, d = h.shape
    f32 = jnp.float32
    xx = jnp.pad(h, ((0, 0), (1, 0), (0, 0)))[:, :-1] - h
    xr = h + xx * mu[0]
    xw = h + xx * mu[1]
    xk = h + xx * mu[2]
    xv = h + xx * mu[3]
    xa = h + xx * mu[4]
    xg = h + xx * mu[5]
    r = xr @ w_rkv[0]
    k = xk @ w_rkv[1]
    v = xv @ w_rkv[2]
    w_log = -jax.nn.softplus(-(w0 + jnp.tanh(xw @ w1) @ w2).astype(f32)) - 0.5
    decay = jnp.exp(-jnp.exp(w_log))
    a = jax.nn.sigmoid((a0 + (xa @ a1) @ a2).astype(f32))
    g = jax.nn.sigmoid(xg @ g1) @ g2
    hs = lambda t: t.reshape(bsz, s, RWKV_HEADS, RWKV_HEAD_DIM)
    kk = hs((k * k_k).astype(f32))
    kk = kk / jnp.maximum(jnp.sqrt(jnp.sum(kk * kk, axis=-1, keepdims=True)), 1e-12)
    k = k.astype(f32) * (1.0 + (a - 1.0) * k_a.astype(f32))
    r_h, k_h, v_h, a_h = hs(r.astype(f32)), hs(k), hs(v.astype(f32)), hs(a)
    y = wkv7_scan(r_h, hs(decay), k_h, v_h, -kk, kk * a_h)
    mean = jnp.mean(y, axis=-1, keepdims=True)
    var = jnp.mean(jnp.square(y - mean), axis=-1, keepdims=True)
    y = ((y - mean) * lax.rsqrt(var + GN_EPS)).reshape(bsz, s, d) * lnx_w + lnx_b
    bonus = jnp.sum(r_h * k_h * r_k.astype(f32), axis=-1, keepdims=True) * v_h
    y = y + bonus.reshape(bsz, s, d)
    return (y * g).astype(h.dtype) @ w_o


def mla_shared_kv(h, cos, sin, w_dkv, g_kv, w_uk, w_uv, w_kr):
    c_kv = rmsnorm(h @ w_dkv, g_kv)
    k_nope = jnp.einsum('bsc,chd->bhsd', c_kv, w_uk)
    v = jnp.einsum('bsc,chd->bhsd', c_kv, w_uv)
    k_rope = apply_rope(h @ w_kr, cos, sin)
    return k_nope, k_rope, v


def mla_causal_attention(q_nope, q_rope, k_nope, k_rope, v):
    bsz, nh, s, _ = q_nope.shape
    n_blk = s // ATTN_BLOCK
    scale = (QK_NOPE + QK_ROPE) ** -0.5
    offs = jnp.arange(ATTN_BLOCK)
    f32 = jnp.float32

    def query_block(i):
        q0 = i * ATTN_BLOCK
        qn = lax.dynamic_slice_in_dim(q_nope, q0, ATTN_BLOCK, axis=2)
        qr = lax.dynamic_slice_in_dim(q_rope, q0, ATTN_BLOCK, axis=2)
        q_idx = q0 + offs

        def key_block(j, carry):
            m, l, acc = carry
            k0 = j * ATTN_BLOCK
            kn = lax.dynamic_slice_in_dim(k_nope, k0, ATTN_BLOCK, axis=2)
            kr = lax.dynamic_slice_in_dim(k_rope, k0, ATTN_BLOCK, axis=1)
            vb = lax.dynamic_slice_in_dim(v, k0, ATTN_BLOCK, axis=2)
            sc = (jnp.einsum('bhqd,bhkd->bhqk', qn, kn).astype(f32)
                  + jnp.einsum('bhqd,bkd->bhqk', qr, kr).astype(f32)) * scale
            sc = jnp.where((k0 + offs)[None, :] <= q_idx[:, None], sc, -jnp.inf)
            m_new = jnp.maximum(m, jnp.max(sc, axis=-1))
            p = jnp.exp(sc - m_new[..., None])
            corr = jnp.exp(m - m_new)
            l = l * corr + jnp.sum(p, axis=-1)
            acc = acc * corr[..., None] + jnp.einsum('bhqk,bhkd->bhqd', p, vb.astype(f32))
            return m_new, l, acc

        init = (jnp.full((bsz, nh, ATTN_BLOCK), -jnp.inf, f32),
                jnp.zeros((bsz, nh, ATTN_BLOCK), f32),
                jnp.zeros((bsz, nh, ATTN_BLOCK, V_HEAD), f32))
        _, l, acc = lax.fori_loop(0, i + 1, key_block, init)
        return (acc / l[..., None]).astype(v.dtype)

    out = lax.map(query_block, jnp.arange(n_blk))
    return jnp.transpose(out, (1, 0, 3, 2, 4)).reshape(bsz, s, nh * V_HEAD)


def mla_layer(h, cos, sin, w_dq, g_q, w_uq, w_qr, w_o, k_nope, k_rope, v):
    c_q = rmsnorm(h @ w_dq, g_q)
    q_nope = jnp.einsum('bsc,chd->bhsd', c_q, w_uq)
    q_rope = apply_rope(jnp.einsum('bsc,chd->bshd', c_q, w_qr), cos[:, :, None, :], sin[:, :, None, :])
    q_rope = jnp.transpose(q_rope, (0, 2, 1, 3))
    o = mla_causal_attention(q_nope, q_rope, k_nope, k_rope, v)
    return o @ w_o


def grouped_moe(h, router_w, router_b, w_gu, w_down):
    bsz, s, _ = h.shape
    f32 = jnp.float32
    aff = jax.nn.sigmoid((h @ router_w).astype(f32))
    sel = aff + router_b
    grp = sel.reshape(bsz, s, N_GROUPS, EXPERTS_PER_GROUP)
    group_score = jnp.sum(lax.top_k(grp, TOP_K)[0], axis=-1)
    best_group = jnp.argmax(group_score, axis=-1)
    in_group = (jnp.arange(N_EXPERTS) // EXPERTS_PER_GROUP)[None, None, :] == best_group[..., None]
    _, idx = lax.top_k(jnp.where(in_group, sel, -jnp.inf), TOP_K)
    top_aff = jnp.take_along_axis(aff, idx, axis=-1)
    gates = top_aff / jnp.sum(top_aff, axis=-1, keepdims=True)
    combine = jnp.sum(jax.nn.one_hot(idx, N_EXPERTS, dtype=f32) * gates[..., None], axis=-2)

    def per_sequence(args):
        hs, cs = args
        gu = jnp.einsum('sd,edf->sef', hs, w_gu)
        gt, up = jnp.split(gu, 2, axis=-1)
        act = jax.nn.silu(gt) * up * cs[..., None].astype(hs.dtype)
        return jnp.einsum('sef,efd->sd', act, w_down)

    return lax.map(per_sequence, (h, combine))


def setup_inputs(seed: int = 0) -> dict:
    key = jax.random.key(seed)
    ks = iter(jax.random.split(key, 64))
    f32 = jnp.float32
    D = D_MODEL
    NA, NB = N_A_LAYERS, N_B_LAYERS

    def nrm(shape, fan_in, scale=1.0):
        return jax.random.normal(next(ks), shape, f32) * (scale * fan_in ** -0.5)

    def gain(shape):
        return 1.0 + 0.02 * jax.random.normal(next(ks), shape, f32)

    def small(shape, s=0.02):
        return s * jax.random.normal(next(ks), shape, f32)

    x = jax.random.normal(next(ks), (BATCH, SEQ, D), f32)
    c = jax.random.normal(next(ks), (BATCH, D), f32)
    positions = (jax.random.randint(next(ks), (BATCH, 1), 0, POS_OFFSET_MAX)
                 + jnp.arange(SEQ)[None, :]).astype(jnp.int32)
    return {
        'x': x, 'c': c, 'positions': positions,
        'ada_w': nrm((DEPTH, 2, D, 3 * D), D, 0.5),
        'ada_b': small((DEPTH, 2, 3 * D)),
        'norm_g': gain((DEPTH, 2, D)),
        'rwkv_mu': jax.random.uniform(next(ks), (NA, 6, D), f32),
        'rwkv_w_rkv': nrm((NA, 3, D, D), D),
        'rwkv_w0': jax.random.uniform(next(ks), (NA, D), f32, -6.5, -1.5),
        'rwkv_w1': nrm((NA, D, DECAY_LORA), D),
        'rwkv_w2': nrm((NA, DECAY_LORA, D), DECAY_LORA, 0.1),
        'rwkv_a0': small((NA, D), 0.1),
        'rwkv_a1': nrm((NA, D, AAA_LORA), D),
        'rwkv_a2': nrm((NA, AAA_LORA, D), AAA_LORA, 0.1),
        'rwkv_g1': nrm((NA, D, GATE_LORA), D),
        'rwkv_g2': nrm((NA, GATE_LORA, D), GATE_LORA),
        'rwkv_k_k': 0.85 + small((NA, D)),
        'rwkv_k_a': gain((NA, D)),
        'rwkv_r_k': small((NA, RWKV_HEADS, RWKV_HEAD_DIM), 0.1),
        'rwkv_lnx_w': gain((NA, D)),
        'rwkv_lnx_b': small((NA, D)),
        'rwkv_w_o': nrm((NA, D, D), D),
        'kv_ada_w': nrm((D, 2 * D), D, 0.5),
        'kv_ada_b': small((2 * D,)),
        'kv_norm_g': gain((D,)),
        'mla_w_dkv': nrm((D, KV_LORA), D),
        'mla_g_kv': gain((KV_LORA,)),
        'mla_w_uk': nrm((KV_LORA, MLA_HEADS, QK_NOPE), KV_LORA),
        'mla_w_uv': nrm((KV_LORA, MLA_HEADS, V_HEAD), KV_LORA),
        'mla_w_kr': nrm((D, QK_ROPE), D),
        'mla_w_dq': nrm((NB, D, Q_LORA), D),
        'mla_g_q': gain((NB, Q_LORA)),
        'mla_w_uq': nrm((NB, Q_LORA, MLA_HEADS, QK_NOPE), Q_LORA),
        'mla_w_qr': nrm((NB, Q_LORA, MLA_HEADS, QK_ROPE), Q_LORA),
        'mla_w_o': nrm((NB, MLA_HEADS * V_HEAD, D), MLA_HEADS * V_HEAD),
        'router_w': nrm((D, N_EXPERTS), D),
        'router_b': small((N_EXPERTS,), 0.01),
        'moe_w_gu': nrm((DEPTH, N_EXPERTS, D, 2 * D_EXPERT), D),
        'moe_w_down': nrm((DEPTH, N_EXPERTS, D_EXPERT, D), D_EXPERT),
        'final_g': gain((D,)),
    }


def reference(x, c, positions, ada_w, ada_b, norm_g,
              rwkv_mu, rwkv_w_rkv, rwkv_w0, rwkv_w1, rwkv_w2, rwkv_a0, rwkv_a1, rwkv_a2,
              rwkv_g1, rwkv_g2, rwkv_k_k, rwkv_k_a, rwkv_r_k, rwkv_lnx_w, rwkv_lnx_b, rwkv_w_o,
              kv_ada_w, kv_ada_b, kv_norm_g, mla_w_dkv, mla_g_kv, mla_w_uk, mla_w_uv, mla_w_kr,
              mla_w_dq, mla_g_q, mla_w_uq, mla_w_qr, mla_w_o,
              router_w, router_b, moe_w_gu, moe_w_down, final_g):
    cos, sin = rope_tables(positions)
    k_nope = k_rope = v_shared = None
    for layer in range(DEPTH):
        if layer == N_A_LAYERS:
            kv_shift, kv_scale = ada_params(c, kv_ada_w, kv_ada_b, 2)
            h_kv = modulate(x, kv_norm_g, kv_shift, kv_scale)
            k_nope, k_rope, v_shared = mla_shared_kv(h_kv, cos, sin, mla_w_dkv, mla_g_kv,
                                                     mla_w_uk, mla_w_uv, mla_w_kr)
        shift, scale, gate = ada_params(c, ada_w[layer, 0], ada_b[layer, 0], 3)
        h = modulate(x, norm_g[layer, 0], shift, scale)
        if layer < N_A_LAYERS:
            a = layer
            mix = rwkv7_time_mix(h, rwkv_mu[a], rwkv_w_rkv[a], rwkv_w0[a], rwkv_w1[a], rwkv_w2[a],
                                 rwkv_a0[a], rwkv_a1[a], rwkv_a2[a], rwkv_g1[a], rwkv_g2[a],
                                 rwkv_k_k[a], rwkv_k_a[a], rwkv_r_k[a], rwkv_lnx_w[a],
                                 rwkv_lnx_b[a], rwkv_w_o[a])
        else:
            bl = layer - N_A_LAYERS
            mix = mla_layer(h, cos, sin, mla_w_dq[bl], mla_g_q[bl], mla_w_uq[bl], mla_w_qr[bl],
                            mla_w_o[bl], k_nope, k_rope, v_shared)
        x = x + gate * mix
        shift, scale, gate = ada_params(c, ada_w[layer, 1], ada_b[layer, 1], 3)
        h = modulate(x, norm_g[layer, 1], shift, scale)
        x = x + gate * grouped_moe(h, router_w, router_b, moe_w_gu[layer], moe_w_down[layer])
    return rmsnorm(x, final_g)
```

```python
import functools

import jax
import jax.numpy as jnp
from jax import lax
from jax.experimental import pallas as pl
from jax.experimental.pallas import tpu as pltpu

F32 = jnp.float32
BF16 = jnp.bfloat16

NORM_EPS = 1e-6
GN_EPS = 64e-5
ROPE_THETA = 10000.0
RWKV_HEAD_DIM = 64
MLA_HEADS = 16
QK_NOPE = 64
QK_ROPE = 32
V_HEAD = 64
N_EXPERTS = 16
N_GROUPS = 4
EXPERTS_PER_GROUP = N_EXPERTS // N_GROUPS
TOP_K = 2

LANES = 128
MXU_DIM = 256
WKV_CHUNK = 64
VMEM_LIMIT = 56 * 1024 * 1024


def _dot(a, b):
    return jnp.dot(a, b, preferred_element_type=F32)


def _dot_nt(a, b):
    return lax.dot_general(a, b, (((1,), (1,)), ((), ())), preferred_element_type=F32)


def _split2(x):
    hi = x.astype(BF16)
    lo = (x - hi.astype(F32)).astype(BF16)
    return hi, lo


def _split3(x):
    hi = x.astype(BF16)
    r1 = x - hi.astype(F32)
    mid = r1.astype(BF16)
    lo = (r1 - mid.astype(F32)).astype(BF16)
    return hi, mid, lo


def _dot_x3(a, b):
    ah, al = _split2(a)
    bh, bl = _split2(b)
    return _dot(ah, bh) + (_dot(ah, bl) + _dot(al, bh))


def _dot_exact_rhs(a, b_bf16):
    h, m, l = _split3(a)
    return _dot(h, b_bf16) + (_dot(m, b_bf16) + _dot(l, b_bf16))


def _rms(x):
    return x * lax.rsqrt(jnp.mean(x * x, axis=-1, keepdims=True) + NORM_EPS)


def _seg_bcast_sum(x, bd):
    d = x.shape[-1]
    outs = []
    for j in range(d // MXU_DIM):
        outs.append(_dot_exact_rhs(x[:, j * MXU_DIM:(j + 1) * MXU_DIM], bd))
    return jnp.concatenate(outs, axis=-1)


def _params(sem):
    return pltpu.CompilerParams(dimension_semantics=sem, vmem_limit_bytes=VMEM_LIMIT)


def _ada_kernel(c_ref, w_ref, b_ref, o_ref):
    c = c_ref[...]
    s = c * jax.nn.sigmoid(c)
    o_ref[0] = _dot_x3(s, w_ref[0]) + b_ref[0]


def _ada(c_pad, w, b, tn=512):
    n, d, nn = w.shape
    return pl.pallas_call(
        _ada_kernel,
        out_shape=jax.ShapeDtypeStruct((n, 8, nn), F32),
        grid=(n, nn // tn),
        in_specs=[pl.BlockSpec((8, d), lambda i, j: (0, 0)),
                  pl.BlockSpec((1, d, tn), lambda i, j: (i, 0, j)),
                  pl.BlockSpec((1, 1, tn), lambda i, j: (i, 0, j))],
        out_specs=pl.BlockSpec((1, 8, tn), lambda i, j: (i, 0, j)),
        compiler_params=_params(("parallel", "parallel")),
        name="ada",
    )(c_pad, w, b)


def _rwkv_pre_kernel(x_ref, xp_ref, mod_ref, g_ref, mu_ref, wrkv_ref, w1_ref, w2_ref, a1_ref, a2_ref,
                     g1_ref, g2_ref, vec_ref, bd_ref,
                     r_ref, k_ref, v_ref, an_ref, b_ref, lw_ref, gate_ref, bonus_ref, *, tiles_per_seq):
    d = x_ref.shape[-1]
    tm = x_ref.shape[0]
    i = pl.program_id(0)
    shift = mod_ref[:, 0:d]
    scale = mod_ref[:, d:2 * d]
    gn = g_ref[...]

    def modulate(xv):
        return _rms(xv) * gn * (1.0 + scale) + shift

    h = modulate(x_ref[...])
    hp_last = modulate(xp_ref[...])[7:8, :]
    hp_last = jnp.where(i % tiles_per_seq == 0, 0.0, hp_last)
    row = lax.broadcasted_iota(jnp.int32, (tm, d), 0)
    h_prev = jnp.where(row == 0, hp_last, pltpu.roll(h, 1, axis=0))
    xx = h_prev - h

    def mix(j):
        return (h + xx * mu_ref[j:j + 1, :]).astype(BF16)

    w0 = vec_ref[0:1, :]
    a0 = vec_ref[1:2, :]
    k_k = vec_ref[2:3, :]
    k_a = vec_ref[3:4, :]
    r_k = vec_ref[4:5, :]
    bd = bd_ref[...]

    r = _dot(mix(0), wrkv_ref[0])
    k = _dot(mix(2), wrkv_ref[1])
    v = _dot(mix(3), wrkv_ref[2])
    z = w0 + _dot(jnp.tanh(_dot(mix(1), w1_ref[...])).astype(BF16), w2_ref[...])
    w_log = -(jnp.maximum(-z, 0.0) + jnp.log(1.0 + jnp.exp(-jnp.abs(z)))) - 0.5
    lw_ref[...] = -jnp.exp(w_log)
    a = jax.nn.sigmoid(a0 + _dot(_dot(mix(4), a1_ref[...]).astype(BF16), a2_ref[...]))
    gate_ref[...] = _dot(jax.nn.sigmoid(_dot(mix(5), g1_ref[...])).astype(BF16), g2_ref[...]).astype(BF16)

    kk = k * k_k
    kk = kk / jnp.maximum(jnp.sqrt(_seg_bcast_sum(kk * kk, bd)), 1e-12)
    km = k * (1.0 + (a - 1.0) * k_a)
    bonus_ref[...] = (_seg_bcast_sum(r * km * r_k, bd) * v).astype(BF16)
    r_ref[...] = r.astype(BF16)
    k_ref[...] = km.astype(BF16)
    v_ref[...] = v.astype(BF16)
    an_ref[...] = (-kk).astype(BF16)
    b_ref[...] = (kk * a).astype(BF16)


def _rwkv_pre(x2d, mod, norm_g, mu8, wrkv, w1, w2, a1, a2, g1, g2, vecs, bd, seq, tm):
    t, d = x2d.shape
    tps = seq // tm
    full = lambda a: pl.BlockSpec(a.shape, lambda i: (0,) * a.ndim)
    tok = pl.BlockSpec((tm, d), lambda i: (i, 0))
    out_bf = jax.ShapeDtypeStruct((t, d), BF16)
    return pl.pallas_call(
        functools.partial(_rwkv_pre_kernel, tiles_per_seq=tps),
        out_shape=(out_bf, out_bf, out_bf, out_bf, out_bf, jax.ShapeDtypeStruct((t, d), F32), out_bf, out_bf),
        grid=(t // tm,),
        in_specs=[tok,
                  pl.BlockSpec((8, d), lambda i: (jnp.maximum(i * (tm // 8) - 1, 0), 0)),
                  pl.BlockSpec((None, 1, mod.shape[-1]), lambda i: (i // tps, 0, 0)),
                  full(norm_g), full(mu8), full(wrkv), full(w1), full(w2), full(a1), full(a2),
                  full(g1), full(g2), full(vecs), full(bd)],
        out_specs=(tok,) * 8,
        compiler_params=_params(("parallel",)),
        name="rwkv_pre",
    )(x2d, x2d, mod, norm_g, mu8, wrkv, w1, w2, a1, a2, g1, g2, vecs, bd)


def _wkv_a_kernel(r_ref, k_ref, v_ref, an_ref, b_ref, lw_ref, q_ref, y0_ref, g_ref, c_ref, *, n_chunks):
    L = WKV_CHUNK
    L2 = 2 * L
    lane = lax.broadcasted_iota(jnp.int32, (L, LANES), 1)
    head0 = lane < RWKV_HEAD_DIM
    ri = lax.broadcasted_iota(jnp.int32, (L2, L2), 0)
    ci = lax.broadcasted_iota(jnp.int32, (L2, L2), 1)
    strict = ci < ri
    incl = ci <= ri
    eye = ci == ri
    tri = (lax.broadcasted_iota(jnp.int32, (L, L), 1) <= lax.broadcasted_iota(jnp.int32, (L, L), 0)).astype(BF16)

    def stack(xv):
        return jnp.concatenate([jnp.where(head0, xv, 0.0), jnp.where(head0, 0.0, xv)], axis=0)

    for c in range(n_chunks):
        sl = pl.ds(c * L, L)
        lw = lw_ref[sl, :]
        cum = _dot_exact_rhs_lhs(tri, lw)
        cl = cum[L - 1:L, :]
        g_in = jnp.exp(cum)
        g_ex = jnp.exp(cum - lw)
        g_inv = jnp.exp(-cum)
        g_end = jnp.exp(cl - cum)
        kf = k_ref[sl, :].astype(F32)
        bf = b_ref[sl, :].astype(F32)
        a2 = stack(an_ref[sl, :].astype(F32) * g_ex).astype(BF16)
        r2 = stack(r_ref[sl, :].astype(F32) * g_in)
        b2 = stack(bf * g_inv).astype(BF16)
        k2 = stack(kf * g_inv).astype(BF16)
        v2 = stack(v_ref[sl, :].astype(F32)).astype(BF16)
        bg2t = stack(bf * g_end).T.astype(BF16)
        kg2t = stack(kf * g_end).T.astype(BF16)

        gram = _dot_nt(jnp.concatenate([a2, r2.astype(BF16)], axis=0), jnp.concatenate([b2, k2], axis=0))
        m_ab = jnp.where(strict, gram[:L2, :L2], 0.0)
        m_ak = jnp.where(strict, gram[:L2, L2:], 0.0)
        m_rb = jnp.where(incl, gram[L2:, :L2], 0.0)
        m_rk = jnp.where(incl, gram[L2:, L2:], 0.0)

        nk = m_ab
        tinv = jnp.where(eye, 1.0, m_ab)
        for _ in range(5):
            nk = _dot_x3(nk, nk)
            tinv = tinv + _dot_x3(tinv, nk)

        mv = _dot(jnp.concatenate([m_ak, m_rk], axis=0).astype(BF16), v2)
        wu = _dot_x3(tinv, jnp.concatenate([a2.astype(F32), mv[:L2]], axis=1))
        wu_b = wu.astype(BF16)
        x = _dot(jnp.concatenate([m_rb.astype(BF16), bg2t], axis=0), wu_b)
        kgv = _dot(kg2t, v2)

        q_ref[c, 0] = (r2 + x[:L2, :L2]).astype(BF16)
        y02 = x[:L2, L2:] + mv[L2:]
        y0_ref[sl, :] = y02[:L] + y02[L:]
        g_end_row = jnp.exp(cl)
        g_ref[c, 0] = (jnp.where(eye, jnp.broadcast_to(g_end_row, (L2, L2)), 0.0) + x[L2:, :L2]).astype(BF16)
        c_ref[c, 0] = (x[L2:, L2:] + kgv).astype(BF16)


def _dot_exact_rhs_lhs(tri_bf16, x):
    h, m, l = _split3(x)
    return _dot(tri_bf16, h) + (_dot(tri_bf16, m) + _dot(tri_bf16, l))


def _wkv_a(r, k, v, an, b, lw, n_chunks_step):
    bsz, s, d = r.shape
    L = WKV_CHUNK
    L2 = 2 * L
    pairs = d // LANES
    nck = s // L
    rows = n_chunks_step * L
    tok = pl.BlockSpec((None, rows, LANES), lambda bi, ci, pi: (bi, ci, pi))
    mat = pl.BlockSpec((n_chunks_step, 1, L2, L2), lambda bi, ci, pi: (ci, bi * pairs + pi, 0, 0))
    mshape = jax.ShapeDtypeStruct((nck, bsz * pairs, L2, L2), BF16)
    return pl.pallas_call(
        functools.partial(_wkv_a_kernel, n_chunks=n_chunks_step),
        out_shape=(mshape, jax.ShapeDtypeStruct((bsz, s, d), F32), mshape, mshape),
        grid=(bsz, nck // n_chunks_step, pairs),
        in_specs=[tok] * 6,
        out_specs=(mat, tok, mat, mat),
        compiler_params=_params(("parallel", "parallel", "parallel")),
        name="wkv_a",
    )(r, k, v, an, b, lw)


def _wkv_b_kernel(q_ref, y0_ref, g_ref, c_ref, y_ref, s_ref, *, n_chunks, bsz, pairs):
    L = WKV_CHUNK

    @pl.when(pl.program_id(0) == 0)
    def _():
        s_ref[...] = jnp.zeros_like(s_ref)

    for c in range(n_chunks):
        for bi in range(bsz):
            for pi in range(pairs):
                n = bi * pairs + pi
                sb = s_ref[n].astype(BF16)
                y2 = _dot(q_ref[c, n], sb)
                y_ref[bi, pl.ds(c * L, L), pl.ds(pi * LANES, LANES)] = (
                    y2[:L] + y2[L:] + y0_ref[bi, pl.ds(c * L, L), pl.ds(pi * LANES, LANES)])
                s_ref[n] = _dot(g_ref[c, n], sb) + c_ref[c, n].astype(F32)


def _wkv_b(q, y0, g, cmat, n_chunks_step):
    nck, bp, L2, _ = q.shape
    bsz, s, d = y0.shape
    pairs = d // LANES
    L = WKV_CHUNK
    rows = n_chunks_step * L
    mat = pl.BlockSpec((n_chunks_step, bp, L2, L2), lambda ci: (ci, 0, 0, 0))
    tok = pl.BlockSpec((bsz, rows, d), lambda ci: (0, ci, 0))
    return pl.pallas_call(
        functools.partial(_wkv_b_kernel, n_chunks=n_chunks_step, bsz=bsz, pairs=pairs),
        out_shape=jax.ShapeDtypeStruct((bsz, s, d), F32),
        grid=(nck // n_chunks_step,),
        in_specs=[mat, tok, mat, mat],
        out_specs=tok,
        scratch_shapes=[pltpu.VMEM((bp, L2, L2), F32)],
        compiler_params=_params(("arbitrary",)),
        name="wkv_b",
    )(q, y0, g, cmat)


def _router_combine(h, rw_ref, rb_ref):
    rw = rw_ref[...]
    hh, hl = _split2(h)
    wh, wl = _split2(rw)
    logits = _dot_nt(wh, hh) + (_dot_nt(wh, hl) + _dot_nt(wl, hh))
    aff = jax.nn.sigmoid(logits)
    sel = aff + rb_ref[...]
    affr = [aff[e:e + 1, :] for e in range(N_EXPERTS)]
    selr = [sel[e:e + 1, :] for e in range(N_EXPERTS)]
    top = []
    score = []
    for g in range(N_GROUPS):
        es = range(g * EXPERTS_PER_GROUP, (g + 1) * EXPERTS_PER_GROUP)
        sc = None
        for e in es:
            rank = None
            for o in es:
                if o == e:
                    continue
                beats = ((selr[o] >= selr[e]) if o < e else (selr[o] > selr[e])).astype(F32)
                rank = beats if rank is None else rank + beats
            t = rank < float(TOP_K)
            top.append(t)
            contrib = jnp.where(t, selr[e], 0.0)
            sc = contrib if sc is None else sc + contrib
        score.append(sc)
    best = score[0]
    for g in range(1, N_GROUPS):
        best = jnp.maximum(best, score[g])
    taken = None
    rows = []
    for g in range(N_GROUPS):
        is_best = score[g] == best
        if taken is not None:
            is_best = is_best & jnp.logical_not(taken)
        taken = is_best if taken is None else (taken | is_best)
        for e in range(g * EXPERTS_PER_GROUP, (g + 1) * EXPERTS_PER_GROUP):
            rows.append(jnp.where(is_best & top[e], affr[e], 0.0))
    comb = jnp.concatenate(rows, axis=0)
    comb = comb / jnp.sum(comb, axis=0, keepdims=True)
    return comb.T


def _moe_prologue(x_new, mod2_ref, g2_ref, rw_ref, rb_ref, h_ref, comb_ref):
    d = x_new.shape[-1]
    h = _rms(x_new) * g2_ref[...] * (1.0 + mod2_ref[:, d:2 * d]) + mod2_ref[:, 0:d]
    h_ref[...] = h.astype(BF16)
    comb_ref[...] = _router_combine(h, rw_ref, rb_ref)


def _rwkv_post_kernel(x_ref, y_ref, bonus_ref, gate_ref, mod_ref, mod2_ref, lnw_ref, lnb_ref, wo_ref, bd_ref,
                      g2_ref, rw_ref, rb_ref, xo_ref, h_ref, comb_ref):
    d = x_ref.shape[-1]
    bd = bd_ref[...]
    y = y_ref[...]
    inv_n = 1.0 / RWKV_HEAD_DIM
    yc = y - _seg_bcast_sum(y, bd) * inv_n
    var = _seg_bcast_sum(yc * yc, bd) * inv_n
    yn = yc * lax.rsqrt(var + GN_EPS) * lnw_ref[...] + lnb_ref[...]
    o = (yn + bonus_ref[...].astype(F32)) * gate_ref[...].astype(F32)
    mixed = _dot(o.astype(BF16), wo_ref[...])
    x_new = x_ref[...] + mod_ref[:, 2 * d:3 * d] * mixed
    xo_ref[...] = x_new
    _moe_prologue(x_new, mod2_ref, g2_ref, rw_ref, rb_ref, h_ref, comb_ref)


def _mla_post_kernel(x_ref, o_ref, mod_ref, mod2_ref, wo_ref, g2_ref, rw_ref, rb_ref, xo_ref, h_ref, comb_ref):
    d = x_ref.shape[-1]
    mixed = _dot(o_ref[...], wo_ref[...])
    x_new = x_ref[...] + mod_ref[:, 2 * d:3 * d] * mixed
    xo_ref[...] = x_new
    _moe_prologue(x_new, mod2_ref, g2_ref, rw_ref, rb_ref, h_ref, comb_ref)


def _post_call(kern, name, tok_inputs, mods, consts, seq, tm):
    t, d = tok_inputs[0].shape
    tps = seq // tm
    full = lambda a: pl.BlockSpec(a.shape, lambda i: (0,) * a.ndim)
    tok = pl.BlockSpec((tm, d), lambda i: (i, 0))
    modspec = lambda m: pl.BlockSpec((None, 1, m.shape[-1]), lambda i: (i // tps, 0, 0))
    return pl.pallas_call(
        kern,
        out_shape=(jax.ShapeDtypeStruct((t, d), F32), jax.ShapeDtypeStruct((t, d), BF16),
                   jax.ShapeDtypeStruct((t, N_EXPERTS), F32)),
        grid=(t // tm,),
        in_specs=[tok] * len(tok_inputs) + [modspec(m) for m in mods] + [full(a) for a in consts],
        out_specs=(tok, tok, pl.BlockSpec((tm, N_EXPERTS), lambda i: (i, 0))),
        compiler_params=_params(("parallel",)),
        name=name,
    )(*tok_inputs, *mods, *consts)


def _moe_kernel(x_ref, h_ref, comb_ref, mod_ref, wg_ref, wu_ref, wd_ref, ex_ref, o_ref, acc_ref):
    d = x_ref.shape[-1]
    j = pl.program_id(1)

    @pl.when(j == 0)
    def _():
        acc_ref[...] = jnp.zeros_like(acc_ref)

    h = h_ref[...]
    gt = _dot(h, wg_ref[...])
    up = _dot(h, wu_ref[...])
    cs = _dot_exact_rhs(comb_ref[...], ex_ref[0])
    act = (gt * jax.nn.sigmoid(gt)) * up * cs
    acc_ref[...] += _dot(act.astype(BF16), wd_ref[...])

    @pl.when(j == pl.num_programs(1) - 1)
    def _():
        o_ref[...] = x_ref[...] + mod_ref[:, 2 * d:3 * d] * acc_ref[...]


def _moe(x2d, h, comb, mod, wg, wu, wd, expand, seq, tm, fc):
    t, d = x2d.shape
    f = wg.shape[1]
    tps = seq // tm
    tok = lambda w: pl.BlockSpec((tm, w), lambda i, j: (i, 0))
    return pl.pallas_call(
        _moe_kernel,
        out_shape=jax.ShapeDtypeStruct((t, d), F32),
        grid=(t // tm, f // fc),
        in_specs=[tok(d), tok(d), tok(N_EXPERTS),
                  pl.BlockSpec((None, 1, mod.shape[-1]), lambda i, j: (i // tps, 0, 0)),
                  pl.BlockSpec((d, fc), lambda i, j: (0, j)),
                  pl.BlockSpec((d, fc), lambda i, j: (0, j)),
                  pl.BlockSpec((fc, d), lambda i, j: (j, 0)),
                  pl.BlockSpec((1, N_EXPERTS, fc), lambda i, j: (j, 0, 0))],
        out_specs=tok(d),
        scratch_shapes=[pltpu.VMEM((tm, d), F32)],
        compiler_params=_params(("parallel", "arbitrary")),
        name="moe",
    )(x2d, h, comb, mod, wg, wu, wd, expand)


def _mla_pre_kernel(x_ref, pos_ref, modkv_ref, modq_ref, gkv_ref, gq_ref, wdkv_ref, gckv_ref, wuk_ref, wuv_ref,
                    wkr_ref, wkrr_ref, wdq_ref, gcq_ref, wq_ref, wqr_ref, freq_ref,
                    q_ref, k_ref, v_ref):
    d = x_ref.shape[-1]
    xn = _rms(x_ref[...])
    ang = pos_ref[...].astype(F32) * freq_ref[...]
    cos = jnp.cos(ang)
    sin = jnp.sin(ang)

    hkv = (xn * gkv_ref[...] * (1.0 + modkv_ref[:, d:2 * d]) + modkv_ref[:, 0:d]).astype(BF16)
    ckv = (_rms(_dot(hkv, wdkv_ref[...])) * gckv_ref[...]).astype(BF16)
    v_ref[...] = _dot(ckv, wuv_ref[...]).astype(BF16)
    kr = _dot(hkv, wkr_ref[...]) * cos + _dot(hkv, wkrr_ref[...]) * sin
    kn = _dot(ckv, wuk_ref[...])

    hq = (xn * gq_ref[...] * (1.0 + modq_ref[:, d:2 * d]) + modq_ref[:, 0:d]).astype(BF16)
    cq = (_rms(_dot(hq, wdq_ref[...])) * gcq_ref[...]).astype(BF16)
    qa = _dot(cq, wq_ref[...])
    qb = _dot(cq, wqr_ref[...])
    for hh in range(MLA_HEADS):
        sl = slice(hh * LANES, (hh + 1) * LANES)
        k_ref[:, sl] = (kn[:, sl] + kr).astype(BF16)
        q_ref[:, sl] = (qa[:, sl] * cos + qb[:, sl] * sin).astype(BF16)


def _mla_pre(x2d, pos2d, modkv, modq, consts, seq, tm):
    t, d = x2d.shape
    tps = seq // tm
    full = lambda a: pl.BlockSpec(a.shape, lambda i: (0,) * a.ndim)
    modspec = lambda m: pl.BlockSpec((None, 1, m.shape[-1]), lambda i: (i // tps, 0, 0))
    hq = MLA_HEADS * LANES
    return pl.pallas_call(
        _mla_pre_kernel,
        out_shape=(jax.ShapeDtypeStruct((t, hq), BF16), jax.ShapeDtypeStruct((t, hq), BF16),
                   jax.ShapeDtypeStruct((t, MLA_HEADS * V_HEAD), BF16)),
        grid=(t // tm,),
        in_specs=[pl.BlockSpec((tm, d), lambda i: (i, 0)), pl.BlockSpec((tm, 1), lambda i: (i, 0)),
                  modspec(modkv), modspec(modq)] + [full(a) for a in consts],
        out_specs=(pl.BlockSpec((tm, hq), lambda i: (i, 0)), pl.BlockSpec((tm, hq), lambda i: (i, 0)),
                   pl.BlockSpec((tm, MLA_HEADS * V_HEAD), lambda i: (i, 0))),
        compiler_params=_params(("parallel",)),
        name="mla_pre",
    )(x2d, pos2d, modkv, modq, *consts)


def _attn_kernel(q_ref, k_ref, v_ref, o_ref, m_ref, l_ref, acc_ref, *, tq, tk):
    qi = pl.program_id(2)
    ki = pl.program_id(3)

    @pl.when(ki == 0)
    def _():
        m_ref[...] = jnp.full_like(m_ref, -jnp.inf)
        l_ref[...] = jnp.zeros_like(l_ref)
        acc_ref[...] = jnp.zeros_like(acc_ref)

    def step(masked):
        v = v_ref[...]
        for hh in range(2):
            q = q_ref[:, hh * LANES:(hh + 1) * LANES]
            k = k_ref[:, hh * LANES:(hh + 1) * LANES]
            s = _dot_nt(q, k)
            if masked:
                rq = qi * tq + lax.broadcasted_iota(jnp.int32, (tq, tk), 0)
                ck = ki * tk + lax.broadcasted_iota(jnp.int32, (tq, tk), 1)
                s = jnp.where(ck <= rq, s, -jnp.inf)
            m_old = m_ref[hh]
            m_new = jnp.maximum(m_old, jnp.max(s, axis=-1, keepdims=True))
            p = jnp.exp(s - m_new)
            corr = jnp.exp(m_old - m_new)
            l_ref[hh] = l_ref[hh] * corr + jnp.sum(p, axis=-1, keepdims=True)
            acc_ref[hh] = acc_ref[hh] * corr + _dot(p.astype(BF16), v)
            m_ref[hh] = m_new

    last = (qi * tq + tq - 1) // tk

    @pl.when(ki < last)
    def _():
        step(False)

    @pl.when(ki == last)
    def _():
        step(True)
        lane = lax.broadcasted_iota(jnp.int32, (tq, LANES), 1)
        o0 = acc_ref[0] / l_ref[0]
        o1 = acc_ref[1] / l_ref[1]
        o_ref[...] = jnp.where(lane < V_HEAD, o0, o1).astype(BF16)


def _attn(q, k, v, tq, tk):
    bsz, s, _ = q.shape
    hp = MLA_HEADS // 2

    def kv_map(b, h, qi, ki):
        return (b, jnp.minimum(ki, (qi * tq + tq - 1) // tk), h)

    return pl.pallas_call(
        functools.partial(_attn_kernel, tq=tq, tk=tk),
        out_shape=jax.ShapeDtypeStruct((bsz, s, MLA_HEADS * V_HEAD), BF16),
        grid=(bsz, hp, s // tq, s // tk),
        in_specs=[pl.BlockSpec((None, tq, 2 * LANES), lambda b, h, qi, ki: (b, qi, h)),
                  pl.BlockSpec((None, tk, 2 * LANES), kv_map),
                  pl.BlockSpec((None, tk, LANES), kv_map)],
        out_specs=pl.BlockSpec((None, tq, LANES), lambda b, h, qi, ki: (b, qi, h)),
        scratch_shapes=[pltpu.VMEM((2, tq, 1), F32), pltpu.VMEM((2, tq, 1), F32),
                        pltpu.VMEM((2, tq, LANES), F32)],
        compiler_params=_params(("parallel", "parallel", "parallel", "arbitrary")),
        name="attn",
    )(q, k, v)


def _final_kernel(x_ref, g_ref, o_ref):
    o_ref[...] = _rms(x_ref[...]) * g_ref[...]


def _final(x2d, g, tm):
    t, d = x2d.shape
    return pl.pallas_call(
        _final_kernel,
        out_shape=jax.ShapeDtypeStruct((t, d), F32),
        grid=(t // tm,),
        in_specs=[pl.BlockSpec((tm, d), lambda i: (i, 0)), pl.BlockSpec((1, d), lambda i: (0, 0))],
        out_specs=pl.BlockSpec((tm, d), lambda i: (i, 0)),
        compiler_params=_params(("parallel",)),
        name="final_norm",
    )(x2d, g)


def _rot_half(w):
    w1, w2 = jnp.split(w, 2, axis=-1)
    return jnp.concatenate([-w2, w1], axis=-1)


def _tile_sizes(seq):
    tm = min(512, seq)
    return tm


def kernel(x, c, positions, ada_w, ada_b, norm_g, rwkv_mu, rwkv_w_rkv, rwkv_w0, rwkv_w1, rwkv_w2, rwkv_a0, rwkv_a1, rwkv_a2, rwkv_g1, rwkv_g2, rwkv_k_k, rwkv_k_a, rwkv_r_k, rwkv_lnx_w, rwkv_lnx_b, rwkv_w_o, kv_ada_w, kv_ada_b, kv_norm_g, mla_w_dkv, mla_g_kv, mla_w_uk, mla_w_uv, mla_w_kr, mla_w_dq, mla_g_q, mla_w_uq, mla_w_qr, mla_w_o, router_w, router_b, moe_w_gu, moe_w_down, final_g):
    bsz, seq, d = x.shape
    depth = ada_w.shape[0]
    n_a = rwkv_mu.shape[0]
    t = bsz * seq
    tm = _tile_sizes(seq)
    row = lambda a: a.reshape(1, -1).astype(F32)

    c_pad = jnp.pad(c, ((0, 8 - bsz), (0, 0)))
    mods = _ada(c_pad, ada_w.reshape(depth * 2, d, 3 * d), ada_b.reshape(depth * 2, 1, 3 * d))
    mods = mods[:, :bsz].reshape(depth, 2, bsz, 1, 3 * d)
    mod_kv = _ada(c_pad, kv_ada_w[None], kv_ada_b.reshape(1, 1, 2 * d))[0, :bsz].reshape(bsz, 1, 2 * d)

    idx = jnp.arange(MXU_DIM) // RWKV_HEAD_DIM
    bd = (idx[:, None] == idx[None, :]).astype(BF16)
    rw_t = router_w.T.astype(F32)
    rb_col = router_b.reshape(N_EXPERTS, 1).astype(F32)
    d_exp = moe_w_down.shape[2]
    f_all = N_EXPERTS * d_exp
    fc = 2 * d_exp
    exp_id = jnp.arange(f_all) // d_exp
    expand = (jnp.arange(N_EXPERTS)[:, None] == exp_id[None, :]).astype(BF16)
    expand = expand.reshape(N_EXPERTS, f_all // fc, fc).transpose(1, 0, 2)

    def moe_weights(layer):
        wgu = moe_w_gu[layer]
        wg = wgu[:, :, :d_exp].transpose(1, 0, 2).reshape(d, f_all).astype(BF16)
        wu = wgu[:, :, d_exp:].transpose(1, 0, 2).reshape(d, f_all).astype(BF16)
        wd = moe_w_down[layer].reshape(f_all, d).astype(BF16)
        return wg, wu, wd

    inv_freq = ROPE_THETA ** (-jnp.arange(0, QK_ROPE, 2, dtype=F32) / QK_ROPE)
    freq = jnp.zeros((1, LANES), F32).at[0, QK_NOPE:QK_NOPE + QK_ROPE].set(jnp.concatenate([inv_freq, inv_freq]))
    pos2d = positions.reshape(t, 1)

    x2d = x.reshape(t, d)
    assert depth - n_a == 1
    for layer in range(depth):
        mod_mix = mods[layer, 0]
        mod_ffn = mods[layer, 1]
        if layer < n_a:
            a = layer
            mu8 = jnp.pad(rwkv_mu[a], ((0, 2), (0, 0)))
            vecs = jnp.stack([rwkv_w0[a], rwkv_a0[a], rwkv_k_k[a], rwkv_k_a[a], rwkv_r_k[a].reshape(d),
                              jnp.zeros((d,), F32), jnp.zeros((d,), F32), jnp.zeros((d,), F32)])
            r, k, v, an, b, lw, gate, bonus = _rwkv_pre(
                x2d, mod_mix, row(norm_g[layer, 0]), mu8, rwkv_w_rkv[a].astype(BF16),
                rwkv_w1[a].astype(BF16), rwkv_w2[a].astype(BF16), rwkv_a1[a].astype(BF16), rwkv_a2[a].astype(BF16),
                rwkv_g1[a].astype(BF16), rwkv_g2[a].astype(BF16), vecs, bd, seq, tm)
            sh = lambda z: z.reshape(bsz, seq, d)
            ncs = min(4, seq // WKV_CHUNK)
            qm, y0, gm, cm = _wkv_a(sh(r), sh(k), sh(v), sh(an), sh(b), sh(lw), ncs)
            y = _wkv_b(qm, y0, gm, cm, min(2, seq // WKV_CHUNK)).reshape(t, d)
            x2d, h, comb = _post_call(
                _rwkv_post_kernel, "rwkv_post", [x2d, y, bonus, gate], [mod_mix, mod_ffn],
                [row(rwkv_lnx_w[a]), row(rwkv_lnx_b[a]), rwkv_w_o[a].astype(BF16), bd,
                 row(norm_g[layer, 1]), rw_t, rb_col], seq, tm)
        else:
            bl = layer - n_a
            scale = (QK_NOPE + QK_ROPE) ** -0.5
            hd = MLA_HEADS
            zpad = lambda w, lo, hi: jnp.pad(w, ((0, 0), (0, 0), (lo, hi)))
            kvl = mla_w_uk.shape[0]
            wuk = zpad(mla_w_uk, 0, LANES - QK_NOPE).reshape(kvl, hd * LANES).astype(BF16)
            wuv = mla_w_uv.reshape(kvl, hd * V_HEAD).astype(BF16)
            rope_pad = lambda w: jnp.pad(w, ((0, 0), (QK_NOPE, LANES - QK_NOPE - QK_ROPE))).astype(BF16)
            wkr = rope_pad(mla_w_kr)
            wkrr = rope_pad(_rot_half(mla_w_kr))
            ql = mla_w_uq.shape[1]
            wq = jnp.concatenate([mla_w_uq[bl], mla_w_qr[bl]], axis=-1) * scale
            wq = zpad(wq, 0, LANES - QK_NOPE - QK_ROPE).reshape(ql, hd * LANES).astype(BF16)
            wqr = zpad(_rot_half(mla_w_qr[bl]) * scale, QK_NOPE, LANES - QK_NOPE - QK_ROPE)
            wqr = wqr.reshape(ql, hd * LANES).astype(BF16)
            consts = [row(kv_norm_g), row(norm_g[layer, 0]), mla_w_dkv.astype(BF16), row(mla_g_kv), wuk, wuv,
                      wkr, wkrr, mla_w_dq[bl].astype(BF16), row(mla_g_q[bl]), wq, wqr, freq]
            q, kf, vv = _mla_pre(x2d, pos2d, mod_kv, mod_mix, consts, seq, tm)
            ta = min(512, seq)
            o = _attn(q.reshape(bsz, seq, -1), kf.reshape(bsz, seq, -1), vv.reshape(bsz, seq, -1), ta, ta)
            x2d, h, comb = _post_call(
                _mla_post_kernel, "mla_post", [x2d, o.reshape(t, d)], [mod_mix, mod_ffn],
                [mla_w_o[bl].astype(BF16), row(norm_g[layer, 1]), rw_t, rb_col], seq, tm)
        wg, wu, wd = moe_weights(layer)
        x2d = _moe(x2d, h, comb, mod_ffn, wg, wu, wd, expand, seq, min(1024, seq), fc)
    return _final(x2d, row(final_g), tm).reshape(bsz, seq, d)
```

```python
import functools

import jax
import jax.numpy as jnp
from jax import lax
from jax.experimental import pallas as pl
from jax.experimental.pallas import tpu as pltpu

F32 = jnp.float32
BF16 = jnp.bfloat16

NORM_EPS = 1e-6
GN_EPS = 64e-5
ROPE_THETA = 10000.0
LOG2_E = 1.4426950408889634
RWKV_HEAD_DIM = 64
MLA_HEADS = 16
QK_NOPE = 64
QK_ROPE = 32
V_HEAD = 64
N_EXPERTS = 16
N_GROUPS = 4
EXPERTS_PER_GROUP = N_EXPERTS // N_GROUPS
TOP_K = 2

LANES = 128
MXU_DIM = 256
WKV_CHUNK = 64
VMEM_LIMIT = 56 * 1024 * 1024


def _dot(a, b):
    return jnp.dot(a, b, preferred_element_type=F32)


def _dot_nt(a, b):
    return lax.dot_general(a, b, (((1,), (1,)), ((), ())), preferred_element_type=F32)


def _split2(x):
    hi = x.astype(BF16)
    lo = (x - hi.astype(F32)).astype(BF16)
    return hi, lo


def _split3(x):
    hi = x.astype(BF16)
    r1 = x - hi.astype(F32)
    mid = r1.astype(BF16)
    lo = (r1 - mid.astype(F32)).astype(BF16)
    return hi, mid, lo


def _dot_x3(a, b):
    ah, al = _split2(a)
    bh, bl = _split2(b)
    return _dot(ah, bh) + (_dot(ah, bl) + _dot(al, bh))


def _dot_exact_rhs(a, b_bf16):
    h, m, l = _split3(a)
    return _dot(h, b_bf16) + (_dot(m, b_bf16) + _dot(l, b_bf16))


def _rms(x):
    return x * lax.rsqrt(jnp.mean(x * x, axis=-1, keepdims=True) + NORM_EPS)


def _seg_bcast_sum(x, bd):
    d = x.shape[-1]
    outs = []
    for j in range(d // MXU_DIM):
        outs.append(_dot_exact_rhs(x[:, j * MXU_DIM:(j + 1) * MXU_DIM], bd))
    return jnp.concatenate(outs, axis=-1)


def _params(sem):
    return pltpu.CompilerParams(dimension_semantics=sem, vmem_limit_bytes=VMEM_LIMIT)


def _ada_kernel(c_ref, w_ref, b_ref, o_ref):
    c = c_ref[...]
    s = c * jax.nn.sigmoid(c)
    o_ref[0] = _dot_x3(s, w_ref[0]) + b_ref[0]


def _ada(c_pad, w, b, tn=512):
    n, d, nn = w.shape
    return pl.pallas_call(
        _ada_kernel,
        out_shape=jax.ShapeDtypeStruct((n, 8, nn), F32),
        grid=(n, nn // tn),
        in_specs=[pl.BlockSpec((8, d), lambda i, j: (0, 0)),
                  pl.BlockSpec((1, d, tn), lambda i, j: (i, 0, j)),
                  pl.BlockSpec((1, 1, tn), lambda i, j: (i, 0, j))],
        out_specs=pl.BlockSpec((1, 8, tn), lambda i, j: (i, 0, j)),
        compiler_params=_params(("parallel", "parallel")),
        name="ada",
    )(c_pad, w, b)


def _rwkv_pre_kernel(x_ref, xp_ref, mod_ref, g_ref, mu_ref, wrkv_ref, w1_ref, w2_ref, a1_ref, a2_ref,
                     g1_ref, g2_ref, vec_ref, bd_ref,
                     r_ref, k_ref, v_ref, an_ref, b_ref, lw_ref, gate_ref, bonus_ref, *, tiles_per_seq):
    d = x_ref.shape[-1]
    tm = x_ref.shape[0]
    i = pl.program_id(0)
    shift = mod_ref[:, 0:d]
    scale = mod_ref[:, d:2 * d]
    gn = g_ref[...]

    def modulate(xv):
        return _rms(xv) * gn * (1.0 + scale) + shift

    h = modulate(x_ref[...])
    hp_last = modulate(xp_ref[...])[7:8, :]
    hp_last = jnp.where(i % tiles_per_seq == 0, 0.0, hp_last)
    row = lax.broadcasted_iota(jnp.int32, (tm, d), 0)
    h_prev = jnp.where(row == 0, hp_last, pltpu.roll(h, 1, axis=0))
    xx = h_prev - h

    def mix(j):
        return (h + xx * mu_ref[j:j + 1, :]).astype(BF16)

    w0 = vec_ref[0:1, :]
    a0 = vec_ref[1:2, :]
    k_k = vec_ref[2:3, :]
    k_a = vec_ref[3:4, :]
    r_k = vec_ref[4:5, :]
    bd = bd_ref[...]

    r = _dot(mix(0), wrkv_ref[0])
    k = _dot(mix(2), wrkv_ref[1])
    v = _dot(mix(3), wrkv_ref[2])
    z = w0 + _dot(jnp.tanh(_dot(mix(1), w1_ref[...])).astype(BF16), w2_ref[...])
    w_log = -(jnp.maximum(-z, 0.0) + jnp.log(1.0 + jnp.exp(-jnp.abs(z)))) - 0.5
    lw_ref[...] = -jnp.exp(w_log)
    a = jax.nn.sigmoid(a0 + _dot(_dot(mix(4), a1_ref[...]).astype(BF16), a2_ref[...]))
    gate_ref[...] = _dot(jax.nn.sigmoid(_dot(mix(5), g1_ref[...])).astype(BF16), g2_ref[...]).astype(BF16)

    kk = k * k_k
    kk = kk / jnp.maximum(jnp.sqrt(_seg_bcast_sum(kk * kk, bd)), 1e-12)
    km = k * (1.0 + (a - 1.0) * k_a)
    bonus_ref[...] = (_seg_bcast_sum(r * km * r_k, bd) * v).astype(BF16)
    r_ref[...] = r.astype(BF16)
    k_ref[...] = km.astype(BF16)
    v_ref[...] = v.astype(BF16)
    an_ref[...] = (-kk).astype(BF16)
    b_ref[...] = (kk * a).astype(BF16)


def _rwkv_pre(x2d, mod, norm_g, mu8, wrkv, w1, w2, a1, a2, g1, g2, vecs, bd, seq, tm):
    t, d = x2d.shape
    tps = seq // tm
    full = lambda a: pl.BlockSpec(a.shape, lambda i: (0,) * a.ndim)
    tok = pl.BlockSpec((tm, d), lambda i: (i, 0))
    out_bf = jax.ShapeDtypeStruct((t, d), BF16)
    return pl.pallas_call(
        functools.partial(_rwkv_pre_kernel, tiles_per_seq=tps),
        out_shape=(out_bf, out_bf, out_bf, out_bf, out_bf, jax.ShapeDtypeStruct((t, d), F32), out_bf, out_bf),
        grid=(t // tm,),
        in_specs=[tok,
                  pl.BlockSpec((8, d), lambda i: (jnp.maximum(i * (tm // 8) - 1, 0), 0)),
                  pl.BlockSpec((None, 1, mod.shape[-1]), lambda i: (i // tps, 0, 0)),
                  full(norm_g), full(mu8), full(wrkv), full(w1), full(w2), full(a1), full(a2),
                  full(g1), full(g2), full(vecs), full(bd)],
        out_specs=(tok,) * 8,
        compiler_params=_params(("parallel",)),
        name="rwkv_pre",
    )(x2d, x2d, mod, norm_g, mu8, wrkv, w1, w2, a1, a2, g1, g2, vecs, bd)


def _wkv_a_kernel(r_ref, k_ref, v_ref, an_ref, b_ref, lw_ref, q_ref, y0_ref, g_ref, c_ref, *, n_chunks):
    L = WKV_CHUNK
    L2 = 2 * L
    lane = lax.broadcasted_iota(jnp.int32, (L, LANES), 1)
    head0 = lane < RWKV_HEAD_DIM
    ri = lax.broadcasted_iota(jnp.int32, (L2, L2), 0)
    ci = lax.broadcasted_iota(jnp.int32, (L2, L2), 1)
    strict = ci < ri
    incl = ci <= ri
    eye = ci == ri
    tri = (lax.broadcasted_iota(jnp.int32, (L, L), 1) <= lax.broadcasted_iota(jnp.int32, (L, L), 0)).astype(BF16)

    def stack(xv):
        return jnp.concatenate([jnp.where(head0, xv, 0.0), jnp.where(head0, 0.0, xv)], axis=0)

    chunks = range(n_chunks)
    pre = []
    for c in chunks:
        sl = pl.ds(c * L, L)
        lw = lw_ref[sl, :]
        cum = _dot_exact_rhs_lhs(tri, lw)
        cl = cum[L - 1:L, :]
        g_in = jnp.exp(cum)
        g_ex = jnp.exp(cum - lw)
        g_inv = jnp.exp(-cum)
        g_end = jnp.exp(cl - cum)
        kf = k_ref[sl, :].astype(F32)
        bf = b_ref[sl, :].astype(F32)
        pre.append(dict(
            a2=stack(an_ref[sl, :].astype(F32) * g_ex).astype(BF16),
            r2=stack(r_ref[sl, :].astype(F32) * g_in),
            b2=stack(bf * g_inv).astype(BF16),
            k2=stack(kf * g_inv).astype(BF16),
            v2=stack(v_ref[sl, :].astype(F32)).astype(BF16),
            bg2t=stack(bf * g_end).T.astype(BF16),
            kg2t=stack(kf * g_end).T.astype(BF16),
            g_last=jnp.exp(cl)))

    gram = [_dot_nt(jnp.concatenate([p["a2"], p["r2"].astype(BF16)], axis=0),
                    jnp.concatenate([p["b2"], p["k2"]], axis=0)) for p in pre]
    m_ab = [jnp.where(strict, g[:L2, :L2], 0.0) for g in gram]
    m_ak = [jnp.where(strict, g[:L2, L2:], 0.0) for g in gram]
    m_rb = [jnp.where(incl, g[L2:, :L2], 0.0) for g in gram]
    m_rk = [jnp.where(incl, g[L2:, L2:], 0.0) for g in gram]
    mv = [_dot(jnp.concatenate([m_ak[c], m_rk[c]], axis=0).astype(BF16), pre[c]["v2"]) for c in chunks]
    kgv = [_dot(p["kg2t"], p["v2"]) for p in pre]

    nk = m_ab
    tinv = [jnp.where(eye, 1.0, n) for n in m_ab]
    for _ in range(5):
        nk = [_dot_x3(n, n) for n in nk]
        tinv = [tinv[c] + _dot_x3(tinv[c], nk[c]) for c in chunks]

    wu = [_dot_x3(tinv[c], jnp.concatenate([pre[c]["a2"].astype(F32), mv[c][:L2]], axis=1)) for c in chunks]
    x = [_dot(jnp.concatenate([m_rb[c].astype(BF16), pre[c]["bg2t"]], axis=0), wu[c].astype(BF16)) for c in chunks]

    for c in chunks:
        q_ref[c, 0] = (pre[c]["r2"] + x[c][:L2, :L2]).astype(BF16)
        y02 = x[c][:L2, L2:] + mv[c][L2:]
        y0_ref[pl.ds(c * L, L), :] = y02[:L] + y02[L:]
        g_ref[c, 0] = (jnp.where(eye, jnp.broadcast_to(pre[c]["g_last"], (L2, L2)), 0.0) + x[c][L2:, :L2]).astype(BF16)
        c_ref[c, 0] = (x[c][L2:, L2:] + kgv[c]).astype(BF16)


def _dot_exact_rhs_lhs(tri_bf16, x):
    h, m, l = _split3(x)
    return _dot(tri_bf16, h) + (_dot(tri_bf16, m) + _dot(tri_bf16, l))


def _wkv_a(r, k, v, an, b, lw, n_chunks_step):
    bsz, s, d = r.shape
    L = WKV_CHUNK
    L2 = 2 * L
    pairs = d // LANES
    nck = s // L
    rows = n_chunks_step * L
    tok = pl.BlockSpec((None, rows, LANES), lambda bi, ci, pi: (bi, ci, pi))
    mat = pl.BlockSpec((n_chunks_step, 1, L2, L2), lambda bi, ci, pi: (ci, bi * pairs + pi, 0, 0))
    mshape = jax.ShapeDtypeStruct((nck, bsz * pairs, L2, L2), BF16)
    return pl.pallas_call(
        functools.partial(_wkv_a_kernel, n_chunks=n_chunks_step),
        out_shape=(mshape, jax.ShapeDtypeStruct((bsz, s, d), F32), mshape, mshape),
        grid=(bsz, nck // n_chunks_step, pairs),
        in_specs=[tok] * 6,
        out_specs=(mat, tok, mat, mat),
        compiler_params=_params(("parallel", "parallel", "parallel")),
        name="wkv_a",
    )(r, k, v, an, b, lw)


def _wkv_b_kernel(q_ref, y0_ref, g_ref, c_ref, y_ref, s_ref, *, n_chunks, bsz, pairs):
    L = WKV_CHUNK

    @pl.when(pl.program_id(0) == 0)
    def _():
        s_ref[...] = jnp.zeros_like(s_ref)

    for c in range(n_chunks):
        for bi in range(bsz):
            for pi in range(pairs):
                n = bi * pairs + pi
                sb = s_ref[n].astype(BF16)
                y2 = _dot(q_ref[c, n], sb)
                y_ref[bi, pl.ds(c * L, L), pl.ds(pi * LANES, LANES)] = (
                    y2[:L] + y2[L:] + y0_ref[bi, pl.ds(c * L, L), pl.ds(pi * LANES, LANES)])
                s_ref[n] = _dot(g_ref[c, n], sb) + c_ref[c, n].astype(F32)


def _wkv_b(q, y0, g, cmat, n_chunks_step):
    nck, bp, L2, _ = q.shape
    bsz, s, d = y0.shape
    pairs = d // LANES
    L = WKV_CHUNK
    rows = n_chunks_step * L
    mat = pl.BlockSpec((n_chunks_step, bp, L2, L2), lambda ci: (ci, 0, 0, 0))
    tok = pl.BlockSpec((bsz, rows, d), lambda ci: (0, ci, 0))
    return pl.pallas_call(
        functools.partial(_wkv_b_kernel, n_chunks=n_chunks_step, bsz=bsz, pairs=pairs),
        out_shape=jax.ShapeDtypeStruct((bsz, s, d), F32),
        grid=(nck // n_chunks_step,),
        in_specs=[mat, tok, mat, mat],
        out_specs=tok,
        scratch_shapes=[pltpu.VMEM((bp, L2, L2), F32)],
        compiler_params=_params(("arbitrary",)),
        name="wkv_b",
    )(q, y0, g, cmat)


def _router_combine(h, rw_ref, rb_ref):
    rw = rw_ref[...]
    hh, hl = _split2(h)
    wh, wl = _split2(rw)
    logits = _dot_nt(wh, hh) + (_dot_nt(wh, hl) + _dot_nt(wl, hh))
    aff = jax.nn.sigmoid(logits)
    sel = aff + rb_ref[...]
    affr = [aff[e:e + 1, :] for e in range(N_EXPERTS)]
    selr = [sel[e:e + 1, :] for e in range(N_EXPERTS)]
    top = []
    score = []
    for g in range(N_GROUPS):
        es = range(g * EXPERTS_PER_GROUP, (g + 1) * EXPERTS_PER_GROUP)
        sc = None
        for e in es:
            rank = None
            for o in es:
                if o == e:
                    continue
                beats = ((selr[o] >= selr[e]) if o < e else (selr[o] > selr[e])).astype(F32)
                rank = beats if rank is None else rank + beats
            t = rank < float(TOP_K)
            top.append(t)
            contrib = jnp.where(t, selr[e], 0.0)
            sc = contrib if sc is None else sc + contrib
        score.append(sc)
    best = score[0]
    for g in range(1, N_GROUPS):
        best = jnp.maximum(best, score[g])
    taken = None
    rows = []
    for g in range(N_GROUPS):
        is_best = score[g] == best
        if taken is not None:
            is_best = is_best & jnp.logical_not(taken)
        taken = is_best if taken is None else (taken | is_best)
        for e in range(g * EXPERTS_PER_GROUP, (g + 1) * EXPERTS_PER_GROUP):
            rows.append(jnp.where(is_best & top[e], affr[e], 0.0))
    comb = jnp.concatenate(rows, axis=0)
    comb = comb / jnp.sum(comb, axis=0, keepdims=True)
    return comb.T


def _moe_prologue(x_new, mod2_ref, g2_ref, rw_ref, rb_ref, h_ref, comb_ref):
    d = x_new.shape[-1]
    h = _rms(x_new) * g2_ref[...] * (1.0 + mod2_ref[:, d:2 * d]) + mod2_ref[:, 0:d]
    h_ref[...] = h.astype(BF16)
    comb_ref[...] = _router_combine(h, rw_ref, rb_ref)


def _rwkv_post_kernel(x_ref, y_ref, bonus_ref, gate_ref, mod_ref, mod2_ref, lnw_ref, lnb_ref, wo_ref, bd_ref,
                      g2_ref, rw_ref, rb_ref, xo_ref, h_ref, comb_ref):
    d = x_ref.shape[-1]
    bd = bd_ref[...]
    y = y_ref[...]
    inv_n = 1.0 / RWKV_HEAD_DIM
    yc = y - _seg_bcast_sum(y, bd) * inv_n
    var = _seg_bcast_sum(yc * yc, bd) * inv_n
    yn = yc * lax.rsqrt(var + GN_EPS) * lnw_ref[...] + lnb_ref[...]
    o = (yn + bonus_ref[...].astype(F32)) * gate_ref[...].astype(F32)
    mixed = _dot(o.astype(BF16), wo_ref[...])
    x_new = x_ref[...] + mod_ref[:, 2 * d:3 * d] * mixed
    xo_ref[...] = x_new
    _moe_prologue(x_new, mod2_ref, g2_ref, rw_ref, rb_ref, h_ref, comb_ref)


def _mla_post_kernel(x_ref, o_ref, mod_ref, mod2_ref, wo_ref, g2_ref, rw_ref, rb_ref, xo_ref, h_ref, comb_ref):
    d = x_ref.shape[-1]
    mixed = _dot(o_ref[...], wo_ref[...])
    x_new = x_ref[...] + mod_ref[:, 2 * d:3 * d] * mixed
    xo_ref[...] = x_new
    _moe_prologue(x_new, mod2_ref, g2_ref, rw_ref, rb_ref, h_ref, comb_ref)


def _post_call(kern, name, tok_inputs, mods, consts, seq, tm):
    t, d = tok_inputs[0].shape
    tps = seq // tm
    full = lambda a: pl.BlockSpec(a.shape, lambda i: (0,) * a.ndim)
    tok = pl.BlockSpec((tm, d), lambda i: (i, 0))
    modspec = lambda m: pl.BlockSpec((None, 1, m.shape[-1]), lambda i: (i // tps, 0, 0))
    return pl.pallas_call(
        kern,
        out_shape=(jax.ShapeDtypeStruct((t, d), F32), jax.ShapeDtypeStruct((t, d), BF16),
                   jax.ShapeDtypeStruct((t, N_EXPERTS), F32)),
        grid=(t // tm,),
        in_specs=[tok] * len(tok_inputs) + [modspec(m) for m in mods] + [full(a) for a in consts],
        out_specs=(tok, tok, pl.BlockSpec((tm, N_EXPERTS), lambda i: (i, 0))),
        compiler_params=_params(("parallel",)),
        name=name,
    )(*tok_inputs, *mods, *consts)


def _moe_kernel(x_ref, h_ref, comb_ref, mod_ref, wg_ref, wu_ref, wd_ref, ex_ref, o_ref, acc_ref):
    d = x_ref.shape[-1]
    j = pl.program_id(1)

    @pl.when(j == 0)
    def _():
        acc_ref[...] = jnp.zeros_like(acc_ref)

    h = h_ref[...]
    gt = _dot(h, wg_ref[...])
    up = _dot(h, wu_ref[...])
    cs = _dot_exact_rhs(comb_ref[...], ex_ref[0])
    act = (gt * jax.nn.sigmoid(gt)) * up * cs
    acc_ref[...] += _dot(act.astype(BF16), wd_ref[...])

    @pl.when(j == pl.num_programs(1) - 1)
    def _():
        o_ref[...] = x_ref[...] + mod_ref[:, 2 * d:3 * d] * acc_ref[...]


def _moe(x2d, h, comb, mod, wg, wu, wd, expand, seq, tm, fc):
    t, d = x2d.shape
    f = wg.shape[1]
    tps = seq // tm
    tok = lambda w: pl.BlockSpec((tm, w), lambda i, j: (i, 0))
    return pl.pallas_call(
        _moe_kernel,
        out_shape=jax.ShapeDtypeStruct((t, d), F32),
        grid=(t // tm, f // fc),
        in_specs=[tok(d), tok(d), tok(N_EXPERTS),
                  pl.BlockSpec((None, 1, mod.shape[-1]), lambda i, j: (i // tps, 0, 0)),
                  pl.BlockSpec((d, fc), lambda i, j: (0, j)),
                  pl.BlockSpec((d, fc), lambda i, j: (0, j)),
                  pl.BlockSpec((fc, d), lambda i, j: (j, 0)),
                  pl.BlockSpec((1, N_EXPERTS, fc), lambda i, j: (j, 0, 0))],
        out_specs=tok(d),
        scratch_shapes=[pltpu.VMEM((tm, d), F32)],
        compiler_params=_params(("parallel", "arbitrary")),
        name="moe",
    )(x2d, h, comb, mod, wg, wu, wd, expand)


def _mla_pre_kernel(x_ref, pos_ref, modkv_ref, modq_ref, gkv_ref, gq_ref, wdkv_ref, gckv_ref, wuk_ref, wuv_ref,
                    wkr_ref, wkrr_ref, wdq_ref, gcq_ref, wq_ref, wqr_ref, freq_ref, vone_ref,
                    q_ref, k_ref, v_ref):
    d = x_ref.shape[-1]
    xn = _rms(x_ref[...])
    ang = pos_ref[...].astype(F32) * freq_ref[...]
    cos = jnp.cos(ang)
    sin = jnp.sin(ang)

    hkv = (xn * gkv_ref[...] * (1.0 + modkv_ref[:, d:2 * d]) + modkv_ref[:, 0:d]).astype(BF16)
    ckv = (_rms(_dot(hkv, wdkv_ref[...])) * gckv_ref[...]).astype(BF16)
    v_ref[...] = (_dot(ckv, wuv_ref[...]) + vone_ref[...]).astype(BF16)
    kr = _dot(hkv, wkr_ref[...]) * cos + _dot(hkv, wkrr_ref[...]) * sin
    kn = _dot(ckv, wuk_ref[...])

    hq = (xn * gq_ref[...] * (1.0 + modq_ref[:, d:2 * d]) + modq_ref[:, 0:d]).astype(BF16)
    cq = (_rms(_dot(hq, wdq_ref[...])) * gcq_ref[...]).astype(BF16)
    qa = _dot(cq, wq_ref[...])
    qb = _dot(cq, wqr_ref[...])
    for hh in range(MLA_HEADS):
        sl = slice(hh * LANES, (hh + 1) * LANES)
        k_ref[:, sl] = (kn[:, sl] + kr).astype(BF16)
        q_ref[:, sl] = (qa[:, sl] * cos + qb[:, sl] * sin).astype(BF16)


def _mla_pre(x2d, pos2d, modkv, modq, consts, seq, tm):
    t, d = x2d.shape
    tps = seq // tm
    full = lambda a: pl.BlockSpec(a.shape, lambda i: (0,) * a.ndim)
    modspec = lambda m: pl.BlockSpec((None, 1, m.shape[-1]), lambda i: (i // tps, 0, 0))
    hq = MLA_HEADS * LANES
    return pl.pallas_call(
        _mla_pre_kernel,
        out_shape=(jax.ShapeDtypeStruct((t, hq), BF16), jax.ShapeDtypeStruct((t, hq), BF16),
                   jax.ShapeDtypeStruct((t, hq), BF16)),
        grid=(t // tm,),
        in_specs=[pl.BlockSpec((tm, d), lambda i: (i, 0)), pl.BlockSpec((tm, 1), lambda i: (i, 0)),
                  modspec(modkv), modspec(modq)] + [full(a) for a in consts],
        out_specs=(pl.BlockSpec((tm, hq), lambda i: (i, 0)), pl.BlockSpec((tm, hq), lambda i: (i, 0)),
                   pl.BlockSpec((tm, hq), lambda i: (i, 0))),
        compiler_params=_params(("parallel",)),
        name="mla_pre",
    )(x2d, pos2d, modkv, modq, *consts)


def _attn_kernel(q_ref, k_ref, v_ref, o_ref, m_ref, acc_ref, *, tq, tk):
    qi = pl.program_id(2)
    m_ref[...] = jnp.full_like(m_ref, -jnp.inf)
    acc_ref[...] = jnp.zeros_like(acc_ref)

    def block(j, masked):
        k0 = pl.multiple_of(j * tk, tk)
        for hh in range(2):
            sl = slice(hh * LANES, (hh + 1) * LANES)
            s = _dot_nt(q_ref[:, sl], k_ref[pl.ds(k0, tk), sl])
            if masked:
                rq = qi * tq + lax.broadcasted_iota(jnp.int32, (tq, tk), 0)
                ck = k0 + lax.broadcasted_iota(jnp.int32, (tq, tk), 1)
                s = jnp.where(ck <= rq, s, -jnp.inf)
            m_old = m_ref[hh]
            m_new = jnp.maximum(m_old, jnp.max(s, axis=-1, keepdims=True))
            p = jnp.exp2(s - pltpu.repeat(m_new, tk // LANES, axis=1))
            acc_ref[hh] = acc_ref[hh] * jnp.exp2(m_old - m_new) + _dot(p.astype(BF16), v_ref[pl.ds(k0, tk), sl])
            m_ref[hh] = m_new

    n_full = (qi * tq) // tk

    def body(j, carry):
        block(j, False)
        return carry

    lax.fori_loop(0, n_full, body, 0)
    for jj in range(tq // tk):
        block(n_full + jj, True)

    outs = []
    for hh in range(2):
        acc = acc_ref[hh]
        outs.append((acc / acc[:, V_HEAD:V_HEAD + 1])[:, :V_HEAD])
    o_ref[...] = jnp.concatenate(outs, axis=1).astype(BF16)


def _attn(q, k, v, tq, tk):
    bsz, s, _ = q.shape
    hp = MLA_HEADS // 2
    kv = pl.BlockSpec((None, s, 2 * LANES), lambda b, h, qi: (b, 0, h))
    return pl.pallas_call(
        functools.partial(_attn_kernel, tq=tq, tk=tk),
        out_shape=jax.ShapeDtypeStruct((bsz, s, MLA_HEADS * V_HEAD), BF16),
        grid=(bsz, hp, s // tq),
        in_specs=[pl.BlockSpec((None, tq, 2 * LANES), lambda b, h, qi: (b, qi, h)), kv, kv],
        out_specs=pl.BlockSpec((None, tq, LANES), lambda b, h, qi: (b, qi, h)),
        scratch_shapes=[pltpu.VMEM((2, tq, LANES), F32), pltpu.VMEM((2, tq, LANES), F32)],
        compiler_params=_params(("parallel", "parallel", "arbitrary")),
        name="attn",
    )(q, k, v)


def _final_kernel(x_ref, g_ref, o_ref):
    o_ref[...] = _rms(x_ref[...]) * g_ref[...]


def _final(x2d, g, tm):
    t, d = x2d.shape
    return pl.pallas_call(
        _final_kernel,
        out_shape=jax.ShapeDtypeStruct((t, d), F32),
        grid=(t // tm,),
        in_specs=[pl.BlockSpec((tm, d), lambda i: (i, 0)), pl.BlockSpec((1, d), lambda i: (0, 0))],
        out_specs=pl.BlockSpec((tm, d), lambda i: (i, 0)),
        compiler_params=_params(("parallel",)),
        name="final_norm",
    )(x2d, g)


def _rot_half(w):
    w1, w2 = jnp.split(w, 2, axis=-1)
    return jnp.concatenate([-w2, w1], axis=-1)


def _tile_sizes(seq):
    tm = min(512, seq)
    return tm


def kernel(x, c, positions, ada_w, ada_b, norm_g, rwkv_mu, rwkv_w_rkv, rwkv_w0, rwkv_w1, rwkv_w2, rwkv_a0, rwkv_a1, rwkv_a2, rwkv_g1, rwkv_g2, rwkv_k_k, rwkv_k_a, rwkv_r_k, rwkv_lnx_w, rwkv_lnx_b, rwkv_w_o, kv_ada_w, kv_ada_b, kv_norm_g, mla_w_dkv, mla_g_kv, mla_w_uk, mla_w_uv, mla_w_kr, mla_w_dq, mla_g_q, mla_w_uq, mla_w_qr, mla_w_o, router_w, router_b, moe_w_gu, moe_w_down, final_g):
    bsz, seq, d = x.shape
    depth = ada_w.shape[0]
    n_a = rwkv_mu.shape[0]
    t = bsz * seq
    tm = _tile_sizes(seq)
    row = lambda a: a.reshape(1, -1).astype(F32)

    c_pad = jnp.pad(c, ((0, 8 - bsz), (0, 0)))
    mods = _ada(c_pad, ada_w.reshape(depth * 2, d, 3 * d), ada_b.reshape(depth * 2, 1, 3 * d))
    mods = mods[:, :bsz].reshape(depth, 2, bsz, 1, 3 * d)
    mod_kv = _ada(c_pad, kv_ada_w[None], kv_ada_b.reshape(1, 1, 2 * d))[0, :bsz].reshape(bsz, 1, 2 * d)

    idx = jnp.arange(MXU_DIM) // RWKV_HEAD_DIM
    bd = (idx[:, None] == idx[None, :]).astype(BF16)
    rw_t = router_w.T.astype(F32)
    rb_col = router_b.reshape(N_EXPERTS, 1).astype(F32)
    d_exp = moe_w_down.shape[2]
    f_all = N_EXPERTS * d_exp
    fc = 2 * d_exp
    exp_id = jnp.arange(f_all) // d_exp
    expand = (jnp.arange(N_EXPERTS)[:, None] == exp_id[None, :]).astype(BF16)
    expand = expand.reshape(N_EXPERTS, f_all // fc, fc).transpose(1, 0, 2)

    def moe_weights(layer):
        wgu = moe_w_gu[layer]
        wg = wgu[:, :, :d_exp].transpose(1, 0, 2).reshape(d, f_all).astype(BF16)
        wu = wgu[:, :, d_exp:].transpose(1, 0, 2).reshape(d, f_all).astype(BF16)
        wd = moe_w_down[layer].reshape(f_all, d).astype(BF16)
        return wg, wu, wd

    inv_freq = ROPE_THETA ** (-jnp.arange(0, QK_ROPE, 2, dtype=F32) / QK_ROPE)
    freq = jnp.zeros((1, LANES), F32).at[0, QK_NOPE:QK_NOPE + QK_ROPE].set(jnp.concatenate([inv_freq, inv_freq]))
    pos2d = positions.reshape(t, 1)

    x2d = x.reshape(t, d)
    assert depth - n_a == 1
    for layer in range(depth):
        mod_mix = mods[layer, 0]
        mod_ffn = mods[layer, 1]
        if layer < n_a:
            a = layer
            mu8 = jnp.pad(rwkv_mu[a], ((0, 2), (0, 0)))
            vecs = jnp.stack([rwkv_w0[a], rwkv_a0[a], rwkv_k_k[a], rwkv_k_a[a], rwkv_r_k[a].reshape(d),
                              jnp.zeros((d,), F32), jnp.zeros((d,), F32), jnp.zeros((d,), F32)])
            r, k, v, an, b, lw, gate, bonus = _rwkv_pre(
                x2d, mod_mix, row(norm_g[layer, 0]), mu8, rwkv_w_rkv[a].astype(BF16),
                rwkv_w1[a].astype(BF16), rwkv_w2[a].astype(BF16), rwkv_a1[a].astype(BF16), rwkv_a2[a].astype(BF16),
                rwkv_g1[a].astype(BF16), rwkv_g2[a].astype(BF16), vecs, bd, seq, tm)
            sh = lambda z: z.reshape(bsz, seq, d)
            ncs = min(4, seq // WKV_CHUNK)
            qm, y0, gm, cm = _wkv_a(sh(r), sh(k), sh(v), sh(an), sh(b), sh(lw), ncs)
            y = _wkv_b(qm, y0, gm, cm, min(2, seq // WKV_CHUNK)).reshape(t, d)
            x2d, h, comb = _post_call(
                _rwkv_post_kernel, "rwkv_post", [x2d, y, bonus, gate], [mod_mix, mod_ffn],
                [row(rwkv_lnx_w[a]), row(rwkv_lnx_b[a]), rwkv_w_o[a].astype(BF16), bd,
                 row(norm_g[layer, 1]), rw_t, rb_col], seq, tm)
        else:
            bl = layer - n_a
            scale = (QK_NOPE + QK_ROPE) ** -0.5 * LOG2_E
            hd = MLA_HEADS
            zpad = lambda w, lo, hi: jnp.pad(w, ((0, 0), (0, 0), (lo, hi)))
            kvl = mla_w_uk.shape[0]
            wuk = zpad(mla_w_uk, 0, LANES - QK_NOPE).reshape(kvl, hd * LANES).astype(BF16)
            wuv = zpad(mla_w_uv, 0, LANES - V_HEAD).reshape(kvl, hd * LANES).astype(BF16)
            vone = jnp.tile(jnp.zeros((LANES,), F32).at[V_HEAD].set(1.0), hd).reshape(1, hd * LANES)
            rope_pad = lambda w: jnp.pad(w, ((0, 0), (QK_NOPE, LANES - QK_NOPE - QK_ROPE))).astype(BF16)
            wkr = rope_pad(mla_w_kr)
            wkrr = rope_pad(_rot_half(mla_w_kr))
            ql = mla_w_uq.shape[1]
            wq = jnp.concatenate([mla_w_uq[bl], mla_w_qr[bl]], axis=-1) * scale
            wq = zpad(wq, 0, LANES - QK_NOPE - QK_ROPE).reshape(ql, hd * LANES).astype(BF16)
            wqr = zpad(_rot_half(mla_w_qr[bl]) * scale, QK_NOPE, LANES - QK_NOPE - QK_ROPE)
            wqr = wqr.reshape(ql, hd * LANES).astype(BF16)
            consts = [row(kv_norm_g), row(norm_g[layer, 0]), mla_w_dkv.astype(BF16), row(mla_g_kv), wuk, wuv,
                      wkr, wkrr, mla_w_dq[bl].astype(BF16), row(mla_g_q[bl]), wq, wqr, freq, vone]
            q, kf, vv = _mla_pre(x2d, pos2d, mod_kv, mod_mix, consts, seq, tm)
            ta = min(512, seq)
            o = _attn(q.reshape(bsz, seq, -1), kf.reshape(bsz, seq, -1), vv.reshape(bsz, seq, -1), ta, ta)
            x2d, h, comb = _post_call(
                _mla_post_kernel, "mla_post", [x2d, o.reshape(t, d)], [mod_mix, mod_ffn],
                [mla_w_o[bl].astype(BF16), row(norm_g[layer, 1]), rw_t, rb_col], seq, tm)
        wg, wu, wd = moe_weights(layer)
        x2d = _moe(x2d, h, comb, mod_ffn, wg, wu, wd, expand, seq, min(1024, seq), fc)
    return _final(x2d, row(final_g), tm).reshape(bsz, seq, d)
```

```python
import functools

import jax
import jax.numpy as jnp
from jax import lax
from jax.experimental import pallas as pl
from jax.experimental.pallas import tpu as pltpu

F32 = jnp.float32
BF16 = jnp.bfloat16

NORM_EPS = 1e-6
GN_EPS = 64e-5
ROPE_THETA = 10000.0
LOG2_E = 1.4426950408889634
RWKV_HEAD_DIM = 64
MLA_HEADS = 16
QK_NOPE = 64
QK_ROPE = 32
V_HEAD = 64
N_EXPERTS = 16
N_GROUPS = 4
EXPERTS_PER_GROUP = N_EXPERTS // N_GROUPS
TOP_K = 2

LANES = 128
MXU_DIM = 256
WKV_CHUNK = 64
VMEM_LIMIT = 56 * 1024 * 1024


def _dot(a, b):
    return jnp.dot(a, b, preferred_element_type=F32)


def _dot_nt(a, b):
    return lax.dot_general(a, b, (((1,), (1,)), ((), ())), preferred_element_type=F32)


def _split2(x):
    hi = x.astype(BF16)
    lo = (x - hi.astype(F32)).astype(BF16)
    return hi, lo


def _split3(x):
    hi = x.astype(BF16)
    r1 = x - hi.astype(F32)
    mid = r1.astype(BF16)
    lo = (r1 - mid.astype(F32)).astype(BF16)
    return hi, mid, lo


def _dot_x3(a, b):
    ah, al = _split2(a)
    bh, bl = _split2(b)
    return _dot(ah, bh) + (_dot(ah, bl) + _dot(al, bh))


def _dot_exact_rhs(a, b_bf16):
    h, l = _split2(a)
    return _dot(h, b_bf16) + _dot(l, b_bf16)


def _rms(x):
    return x * lax.rsqrt(jnp.mean(x * x, axis=-1, keepdims=True) + NORM_EPS)


def _seg_bcast_sum(x, bd):
    d = x.shape[-1]
    outs = []
    for j in range(d // MXU_DIM):
        outs.append(_dot_exact_rhs(x[:, j * MXU_DIM:(j + 1) * MXU_DIM], bd))
    return jnp.concatenate(outs, axis=-1)


def _params(sem):
    return pltpu.CompilerParams(dimension_semantics=sem, vmem_limit_bytes=VMEM_LIMIT)


def _ada_kernel(c_ref, w_ref, b_ref, o_ref):
    c = c_ref[...]
    s = c * jax.nn.sigmoid(c)
    o_ref[0] = _dot_x3(s, w_ref[0]) + b_ref[0]


def _ada(c_pad, w, b, tn=512):
    n, d, nn = w.shape
    return pl.pallas_call(
        _ada_kernel,
        out_shape=jax.ShapeDtypeStruct((n, 8, nn), F32),
        grid=(n, nn // tn),
        in_specs=[pl.BlockSpec((8, d), lambda i, j: (0, 0)),
                  pl.BlockSpec((1, d, tn), lambda i, j: (i, 0, j)),
                  pl.BlockSpec((1, 1, tn), lambda i, j: (i, 0, j))],
        out_specs=pl.BlockSpec((1, 8, tn), lambda i, j: (i, 0, j)),
        compiler_params=_params(("parallel", "parallel")),
        name="ada",
    )(c_pad, w, b)


def _rwkv_pre_kernel(x_ref, xp_ref, mod_ref, g_ref, mu_ref, wrkv_ref, w1_ref, w2_ref, a1_ref, a2_ref,
                     g1_ref, g2_ref, vec_ref, bd_ref,
                     r_ref, k_ref, v_ref, an_ref, b_ref, lw_ref, gate_ref, bonus_ref, *, tiles_per_seq):
    d = x_ref.shape[-1]
    tm = x_ref.shape[0]
    i = pl.program_id(0)
    shift = mod_ref[:, 0:d]
    scale = mod_ref[:, d:2 * d]
    gn = g_ref[...]

    def modulate(xv):
        return _rms(xv) * gn * (1.0 + scale) + shift

    h = modulate(x_ref[...])
    hp_last = modulate(xp_ref[...])[7:8, :]
    hp_last = jnp.where(i % tiles_per_seq == 0, 0.0, hp_last)
    row = lax.broadcasted_iota(jnp.int32, (tm, d), 0)
    h_prev = jnp.where(row == 0, hp_last, pltpu.roll(h, 1, axis=0))
    xx = h_prev - h

    def mix(j):
        return (h + xx * mu_ref[j:j + 1, :]).astype(BF16)

    w0 = vec_ref[0:1, :]
    a0 = vec_ref[1:2, :]
    k_k = vec_ref[2:3, :]
    k_a = vec_ref[3:4, :]
    r_k = vec_ref[4:5, :]
    bd = bd_ref[...]

    r = _dot(mix(0), wrkv_ref[0])
    k = _dot(mix(2), wrkv_ref[1])
    v = _dot(mix(3), wrkv_ref[2])
    z = w0 + _dot(jnp.tanh(_dot(mix(1), w1_ref[...])).astype(BF16), w2_ref[...])
    w_log = -(jnp.maximum(-z, 0.0) + jnp.log(1.0 + jnp.exp(-jnp.abs(z)))) - 0.5
    lw_ref[...] = -jnp.exp(w_log)
    a = jax.nn.sigmoid(a0 + _dot(_dot(mix(4), a1_ref[...]).astype(BF16), a2_ref[...]))
    gate_ref[...] = _dot(jax.nn.sigmoid(_dot(mix(5), g1_ref[...])).astype(BF16), g2_ref[...]).astype(BF16)

    kk = k * k_k
    kk = kk / jnp.maximum(jnp.sqrt(_seg_bcast_sum(kk * kk, bd)), 1e-12)
    km = k * (1.0 + (a - 1.0) * k_a)
    bonus_ref[...] = (_seg_bcast_sum(r * km * r_k, bd) * v).astype(BF16)
    r_ref[...] = r.astype(BF16)
    k_ref[...] = km.astype(BF16)
    v_ref[...] = v.astype(BF16)
    an_ref[...] = (-kk).astype(BF16)
    b_ref[...] = (kk * a).astype(BF16)


def _rwkv_pre(x2d, mod, norm_g, mu8, wrkv, w1, w2, a1, a2, g1, g2, vecs, bd, seq, tm):
    t, d = x2d.shape
    tps = seq // tm
    full = lambda a: pl.BlockSpec(a.shape, lambda i: (0,) * a.ndim)
    tok = pl.BlockSpec((tm, d), lambda i: (i, 0))
    out_bf = jax.ShapeDtypeStruct((t, d), BF16)
    return pl.pallas_call(
        functools.partial(_rwkv_pre_kernel, tiles_per_seq=tps),
        out_shape=(out_bf, out_bf, out_bf, out_bf, out_bf, jax.ShapeDtypeStruct((t, d), F32), out_bf, out_bf),
        grid=(t // tm,),
        in_specs=[tok,
                  pl.BlockSpec((8, d), lambda i: (jnp.maximum(i * (tm // 8) - 1, 0), 0)),
                  pl.BlockSpec((None, 1, mod.shape[-1]), lambda i: (i // tps, 0, 0)),
                  full(norm_g), full(mu8), full(wrkv), full(w1), full(w2), full(a1), full(a2),
                  full(g1), full(g2), full(vecs), full(bd)],
        out_specs=(tok,) * 8,
        compiler_params=_params(("parallel",)),
        name="rwkv_pre",
    )(x2d, x2d, mod, norm_g, mu8, wrkv, w1, w2, a1, a2, g1, g2, vecs, bd)


def _wkv_a_kernel(r_ref, k_ref, v_ref, an_ref, b_ref, lw_ref, q_ref, y0_ref, g_ref, c_ref, *, n_chunks):
    L = WKV_CHUNK
    L2 = 2 * L
    lane = lax.broadcasted_iota(jnp.int32, (L, LANES), 1)
    head0 = lane < RWKV_HEAD_DIM
    ri = lax.broadcasted_iota(jnp.int32, (L2, L2), 0)
    ci = lax.broadcasted_iota(jnp.int32, (L2, L2), 1)
    strict = ci < ri
    incl = ci <= ri
    eye = ci == ri
    tri = (lax.broadcasted_iota(jnp.int32, (L, L), 1) <= lax.broadcasted_iota(jnp.int32, (L, L), 0)).astype(BF16)

    def stack(xv):
        return jnp.concatenate([jnp.where(head0, xv, 0.0), jnp.where(head0, 0.0, xv)], axis=0)

    chunks = range(n_chunks)
    pre = []
    for c in chunks:
        sl = pl.ds(c * L, L)
        lw = lw_ref[sl, :]
        cum = _dot_exact_rhs_lhs(tri, lw)
        cl = cum[L - 1:L, :]
        g_in = jnp.exp(cum)
        g_ex = jnp.exp(cum - lw)
        g_inv = jnp.exp(-cum)
        g_end = jnp.exp(cl - cum)
        kf = k_ref[sl, :].astype(F32)
        bf = b_ref[sl, :].astype(F32)
        pre.append(dict(
            a2=stack(an_ref[sl, :].astype(F32) * g_ex).astype(BF16),
            r2=stack(r_ref[sl, :].astype(F32) * g_in),
            b2=stack(bf * g_inv).astype(BF16),
            k2=stack(kf * g_inv).astype(BF16),
            v2=stack(v_ref[sl, :].astype(F32)).astype(BF16),
            bg2t=stack(bf * g_end).T.astype(BF16),
            kg2t=stack(kf * g_end).T.astype(BF16),
            g_last=jnp.exp(cl)))

    gram = [_dot_nt(jnp.concatenate([p["a2"], p["r2"].astype(BF16)], axis=0),
                    jnp.concatenate([p["b2"], p["k2"]], axis=0)) for p in pre]
    m_ab = [jnp.where(strict, g[:L2, :L2], 0.0) for g in gram]
    m_ak = [jnp.where(strict, g[:L2, L2:], 0.0) for g in gram]
    m_rb = [jnp.where(incl, g[L2:, :L2], 0.0) for g in gram]
    m_rk = [jnp.where(incl, g[L2:, L2:], 0.0) for g in gram]
    mv = [_dot(jnp.concatenate([m_ak[c], m_rk[c]], axis=0).astype(BF16), pre[c]["v2"]) for c in chunks]
    kgv = [_dot(p["kg2t"], p["v2"]) for p in pre]

    nb = [n.astype(BF16) for n in m_ab]
    nk = [_dot(n, n) for n in nb]
    tinv = [jnp.where(eye, 1.0, n) for n in m_ab]
    for step in range(1, 6):
        nb = [n.astype(BF16) for n in nk]
        if step < 5:
            both = [_dot(jnp.concatenate([nb[c], tinv[c].astype(BF16)], axis=0), nb[c]) for c in chunks]
            nk = [m[:L2] for m in both]
            tinv = [tinv[c] + both[c][L2:] for c in chunks]
        else:
            tinv = [tinv[c] + _dot(tinv[c].astype(BF16), nb[c]) for c in chunks]

    wu = [_dot(tinv[c].astype(BF16), jnp.concatenate([pre[c]["a2"], mv[c][:L2].astype(BF16)], axis=1)) for c in chunks]
    x = [_dot(jnp.concatenate([m_rb[c].astype(BF16), pre[c]["bg2t"]], axis=0), wu[c].astype(BF16)) for c in chunks]

    for c in chunks:
        q_ref[c, 0] = (pre[c]["r2"] + x[c][:L2, :L2]).astype(BF16)
        y02 = x[c][:L2, L2:] + mv[c][L2:]
        y0_ref[pl.ds(c * L, L), :] = y02[:L] + y02[L:]
        g_ref[c, 0] = (jnp.where(eye, jnp.broadcast_to(pre[c]["g_last"], (L2, L2)), 0.0) + x[c][L2:, :L2]).astype(BF16)
        c_ref[c, 0] = (x[c][L2:, L2:] + kgv[c]).astype(BF16)


def _dot_exact_rhs_lhs(tri_bf16, x):
    h, m, l = _split3(x)
    return _dot(tri_bf16, h) + (_dot(tri_bf16, m) + _dot(tri_bf16, l))


def _wkv_a(r, k, v, an, b, lw, n_chunks_step):
    bsz, s, d = r.shape
    L = WKV_CHUNK
    L2 = 2 * L
    pairs = d // LANES
    nck = s // L
    rows = n_chunks_step * L
    tok = pl.BlockSpec((None, rows, LANES), lambda bi, ci, pi: (bi, ci, pi))
    mat = pl.BlockSpec((n_chunks_step, 1, L2, L2), lambda bi, ci, pi: (ci, bi * pairs + pi, 0, 0))
    mshape = jax.ShapeDtypeStruct((nck, bsz * pairs, L2, L2), BF16)
    return pl.pallas_call(
        functools.partial(_wkv_a_kernel, n_chunks=n_chunks_step),
        out_shape=(mshape, jax.ShapeDtypeStruct((bsz, s, d), F32), mshape, mshape),
        grid=(bsz, nck // n_chunks_step, pairs),
        in_specs=[tok] * 6,
        out_specs=(mat, tok, mat, mat),
        compiler_params=_params(("parallel", "parallel", "parallel")),
        name="wkv_a",
    )(r, k, v, an, b, lw)


def _wkv_b_kernel(q_ref, y0_ref, g_ref, c_ref, y_ref, s_ref, *, n_chunks, bsz, pairs):
    L = WKV_CHUNK

    @pl.when(pl.program_id(0) == 0)
    def _():
        s_ref[...] = jnp.zeros_like(s_ref)

    for c in range(n_chunks):
        for bi in range(bsz):
            for pi in range(pairs):
                n = bi * pairs + pi
                sb = s_ref[n].astype(BF16)
                y2 = _dot(q_ref[c, n], sb)
                y_ref[bi, pl.ds(c * L, L), pl.ds(pi * LANES, LANES)] = (
                    y2[:L] + y2[L:] + y0_ref[bi, pl.ds(c * L, L), pl.ds(pi * LANES, LANES)])
                s_ref[n] = _dot(g_ref[c, n], sb) + c_ref[c, n].astype(F32)


def _wkv_b(q, y0, g, cmat, n_chunks_step):
    nck, bp, L2, _ = q.shape
    bsz, s, d = y0.shape
    pairs = d // LANES
    L = WKV_CHUNK
    rows = n_chunks_step * L
    mat = pl.BlockSpec((n_chunks_step, bp, L2, L2), lambda ci: (ci, 0, 0, 0))
    tok = pl.BlockSpec((bsz, rows, d), lambda ci: (0, ci, 0))
    return pl.pallas_call(
        functools.partial(_wkv_b_kernel, n_chunks=n_chunks_step, bsz=bsz, pairs=pairs),
        out_shape=jax.ShapeDtypeStruct((bsz, s, d), F32),
        grid=(nck // n_chunks_step,),
        in_specs=[mat, tok, mat, mat],
        out_specs=tok,
        scratch_shapes=[pltpu.VMEM((bp, L2, L2), F32)],
        compiler_params=_params(("arbitrary",)),
        name="wkv_b",
    )(q, y0, g, cmat)


def _router_combine(h, rw_ref, rb_ref):
    rw = rw_ref[...]
    hh, hl = _split2(h)
    wh, wl = _split2(rw)
    logits = _dot_nt(wh, hh) + (_dot_nt(wh, hl) + _dot_nt(wl, hh))
    aff = jax.nn.sigmoid(logits)
    sel = aff + rb_ref[...]
    affr = [aff[e:e + 1, :] for e in range(N_EXPERTS)]
    selr = [sel[e:e + 1, :] for e in range(N_EXPERTS)]
    top = []
    score = []
    for g in range(N_GROUPS):
        es = range(g * EXPERTS_PER_GROUP, (g + 1) * EXPERTS_PER_GROUP)
        sc = None
        for e in es:
            rank = None
            for o in es:
                if o == e:
                    continue
                beats = ((selr[o] >= selr[e]) if o < e else (selr[o] > selr[e])).astype(F32)
                rank = beats if rank is None else rank + beats
            t = rank < float(TOP_K)
            top.append(t)
            contrib = jnp.where(t, selr[e], 0.0)
            sc = contrib if sc is None else sc + contrib
        score.append(sc)
    best = score[0]
    for g in range(1, N_GROUPS):
        best = jnp.maximum(best, score[g])
    taken = None
    rows = []
    for g in range(N_GROUPS):
        is_best = score[g] == best
        if taken is not None:
            is_best = is_best & jnp.logical_not(taken)
        taken = is_best if taken is None else (taken | is_best)
        for e in range(g * EXPERTS_PER_GROUP, (g + 1) * EXPERTS_PER_GROUP):
            rows.append(jnp.where(is_best & top[e], affr[e], 0.0))
    comb = jnp.concatenate(rows, axis=0)
    comb = comb / jnp.sum(comb, axis=0, keepdims=True)
    return jnp.concatenate(_split3(comb), axis=0).T.astype(BF16)


def _moe_prologue(x_new, mod2_ref, g2_ref, rw_ref, rb_ref, h_ref, comb_ref):
    d = x_new.shape[-1]
    h = _rms(x_new) * g2_ref[...] * (1.0 + mod2_ref[:, d:2 * d]) + mod2_ref[:, 0:d]
    h_ref[...] = h.astype(BF16)
    comb_ref[...] = _router_combine(h, rw_ref, rb_ref)


def _rwkv_post_kernel(x_ref, y_ref, bonus_ref, gate_ref, mod_ref, mod2_ref, lnw_ref, lnb_ref, wo_ref, bd_ref,
                      g2_ref, rw_ref, rb_ref, xo_ref, h_ref, comb_ref):
    d = x_ref.shape[-1]
    bd = bd_ref[...]
    y = y_ref[...]
    inv_n = 1.0 / RWKV_HEAD_DIM
    yc = y - _seg_bcast_sum(y, bd) * inv_n
    var = _seg_bcast_sum(yc * yc, bd) * inv_n
    yn = yc * lax.rsqrt(var + GN_EPS) * lnw_ref[...] + lnb_ref[...]
    o = (yn + bonus_ref[...].astype(F32)) * gate_ref[...].astype(F32)
    mixed = _dot(o.astype(BF16), wo_ref[...])
    x_new = x_ref[...] + mod_ref[:, 2 * d:3 * d] * mixed
    xo_ref[...] = x_new
    _moe_prologue(x_new, mod2_ref, g2_ref, rw_ref, rb_ref, h_ref, comb_ref)


def _mla_post_kernel(x_ref, o_ref, mod_ref, mod2_ref, wo_ref, g2_ref, rw_ref, rb_ref, xo_ref, h_ref, comb_ref):
    d = x_ref.shape[-1]
    mixed = _dot(o_ref[...], wo_ref[...])
    x_new = x_ref[...] + mod_ref[:, 2 * d:3 * d] * mixed
    xo_ref[...] = x_new
    _moe_prologue(x_new, mod2_ref, g2_ref, rw_ref, rb_ref, h_ref, comb_ref)


def _post_call(kern, name, tok_inputs, mods, consts, seq, tm):
    t, d = tok_inputs[0].shape
    tps = seq // tm
    full = lambda a: pl.BlockSpec(a.shape, lambda i: (0,) * a.ndim)
    tok = pl.BlockSpec((tm, d), lambda i: (i, 0))
    modspec = lambda m: pl.BlockSpec((None, 1, m.shape[-1]), lambda i: (i // tps, 0, 0))
    return pl.pallas_call(
        kern,
        out_shape=(jax.ShapeDtypeStruct((t, d), F32), jax.ShapeDtypeStruct((t, d), BF16),
                   jax.ShapeDtypeStruct((t, 3 * N_EXPERTS), BF16)),
        grid=(t // tm,),
        in_specs=[tok] * len(tok_inputs) + [modspec(m) for m in mods] + [full(a) for a in consts],
        out_specs=(tok, tok, pl.BlockSpec((tm, 3 * N_EXPERTS), lambda i: (i, 0))),
        compiler_params=_params(("parallel",)),
        name=name,
    )(*tok_inputs, *mods, *consts)


def _moe_kernel(x_ref, h_ref, comb_ref, mod_ref, wg_ref, wu_ref, wd_ref, ex_ref, fg_ref, o_ref, acc_ref, *, final):
    d = x_ref.shape[-1]
    j = pl.program_id(1)

    @pl.when(j == 0)
    def _():
        acc_ref[...] = jnp.zeros_like(acc_ref)

    h = h_ref[...]
    gt = _dot(h, wg_ref[...])
    up = _dot(h, wu_ref[...])
    cs = _dot(comb_ref[...], ex_ref[0])
    act = (gt * jax.nn.sigmoid(gt)) * up * cs
    acc_ref[...] += _dot(act.astype(BF16), wd_ref[...])

    @pl.when(j == pl.num_programs(1) - 1)
    def _():
        x_new = x_ref[...] + mod_ref[:, 2 * d:3 * d] * acc_ref[...]
        o_ref[...] = _rms(x_new) * fg_ref[...] if final else x_new


def _moe(x2d, h, comb, mod, wg, wu, wd, expand, final_g, final, seq, tm, fc):
    t, d = x2d.shape
    f = wg.shape[1]
    tps = seq // tm
    tok = lambda w: pl.BlockSpec((tm, w), lambda i, j: (i, 0))
    return pl.pallas_call(
        functools.partial(_moe_kernel, final=final),
        out_shape=jax.ShapeDtypeStruct((t, d), F32),
        grid=(t // tm, f // fc),
        in_specs=[tok(d), tok(d), tok(3 * N_EXPERTS),
                  pl.BlockSpec((None, 1, mod.shape[-1]), lambda i, j: (i // tps, 0, 0)),
                  pl.BlockSpec((d, fc), lambda i, j: (0, j)),
                  pl.BlockSpec((d, fc), lambda i, j: (0, j)),
                  pl.BlockSpec((fc, d), lambda i, j: (j, 0)),
                  pl.BlockSpec((1, 3 * N_EXPERTS, fc), lambda i, j: (j, 0, 0)),
                  pl.BlockSpec((1, d), lambda i, j: (0, 0))],
        out_specs=tok(d),
        scratch_shapes=[pltpu.VMEM((tm, d), F32)],
        compiler_params=_params(("parallel", "arbitrary")),
        name="moe",
    )(x2d, h, comb, mod, wg, wu, wd, expand, final_g)


def _mla_pre_kernel(x_ref, pos_ref, modkv_ref, modq_ref, gkv_ref, gq_ref, wdkv_ref, gckv_ref, wuk_ref, wuv_ref,
                    wkr_ref, wkrr_ref, wdq_ref, gcq_ref, wq_ref, wqr_ref, freq_ref, vone_ref,
                    q_ref, k_ref, v_ref):
    d = x_ref.shape[-1]
    xn = _rms(x_ref[...])
    ang = pos_ref[...].astype(F32) * freq_ref[...]
    cos = jnp.cos(ang)
    sin = jnp.sin(ang)

    hkv = (xn * gkv_ref[...] * (1.0 + modkv_ref[:, d:2 * d]) + modkv_ref[:, 0:d]).astype(BF16)
    ckv = (_rms(_dot(hkv, wdkv_ref[...])) * gckv_ref[...]).astype(BF16)
    v_ref[...] = (_dot(ckv, wuv_ref[...]) + vone_ref[...]).astype(BF16)
    kr = _dot(hkv, wkr_ref[...]) * cos + _dot(hkv, wkrr_ref[...]) * sin
    kn = _dot(ckv, wuk_ref[...])

    hq = (xn * gq_ref[...] * (1.0 + modq_ref[:, d:2 * d]) + modq_ref[:, 0:d]).astype(BF16)
    cq = (_rms(_dot(hq, wdq_ref[...])) * gcq_ref[...]).astype(BF16)
    qa = _dot(cq, wq_ref[...])
    qb = _dot(cq, wqr_ref[...])
    for hh in range(MLA_HEADS):
        sl = slice(hh * LANES, (hh + 1) * LANES)
        k_ref[:, sl] = (kn[:, sl] + kr).astype(BF16)
        q_ref[:, sl] = (qa[:, sl] * cos + qb[:, sl] * sin).astype(BF16)


def _mla_pre(x2d, pos2d, modkv, modq, consts, seq, tm):
    t, d = x2d.shape
    tps = seq // tm
    full = lambda a: pl.BlockSpec(a.shape, lambda i: (0,) * a.ndim)
    modspec = lambda m: pl.BlockSpec((None, 1, m.shape[-1]), lambda i: (i // tps, 0, 0))
    hq = MLA_HEADS * LANES
    return pl.pallas_call(
        _mla_pre_kernel,
        out_shape=(jax.ShapeDtypeStruct((t, hq), BF16), jax.ShapeDtypeStruct((t, hq), BF16),
                   jax.ShapeDtypeStruct((t, hq), BF16)),
        grid=(t // tm,),
        in_specs=[pl.BlockSpec((tm, d), lambda i: (i, 0)), pl.BlockSpec((tm, 1), lambda i: (i, 0)),
                  modspec(modkv), modspec(modq)] + [full(a) for a in consts],
        out_specs=(pl.BlockSpec((tm, hq), lambda i: (i, 0)), pl.BlockSpec((tm, hq), lambda i: (i, 0)),
                   pl.BlockSpec((tm, hq), lambda i: (i, 0))),
        compiler_params=_params(("parallel",)),
        name="mla_pre",
    )(x2d, pos2d, modkv, modq, *consts)


def _attn_kernel(q_ref, k_ref, v_ref, o_ref, m_ref, acc_ref, *, tq, tk):
    qi = pl.program_id(2)
    m_ref[...] = jnp.full_like(m_ref, -jnp.inf)
    acc_ref[...] = jnp.zeros_like(acc_ref)

    def block(k0, width, diag_offset=None):
        for hh in range(2):
            sl = slice(hh * LANES, (hh + 1) * LANES)
            s = _dot_nt(q_ref[:, sl], k_ref[pl.ds(k0, width), sl])
            if diag_offset is not None:
                rq = lax.broadcasted_iota(jnp.int32, (tq, width), 0)
                ck = diag_offset + lax.broadcasted_iota(jnp.int32, (tq, width), 1)
                s = jnp.where(ck <= rq, s, -jnp.inf)
            m_old = m_ref[hh]
            m_new = jnp.maximum(m_old, jnp.max(s, axis=-1, keepdims=True))
            p = jnp.exp2(s - jnp.concatenate([m_new] * (width // LANES), axis=1))
            acc_ref[hh] = acc_ref[hh] * jnp.exp2(m_old - m_new) + _dot(p.astype(BF16), v_ref[pl.ds(k0, width), sl])
            m_ref[hh] = m_new

    n_full = qi * (tq // tk)

    def body(j, carry):
        block(pl.multiple_of(j * (2 * tk), 2 * tk), 2 * tk)
        return carry

    lax.fori_loop(0, n_full // 2, body, 0)

    @pl.when(n_full % 2 == 1)
    def _():
        block(pl.multiple_of((n_full - 1) * tk, tk), tk)

    for jj in range(tq // tk):
        block(pl.multiple_of((n_full + jj) * tk, tk), tk, diag_offset=jj * tk)

    outs = []
    for hh in range(2):
        acc = acc_ref[hh]
        outs.append((acc / acc[:, V_HEAD:V_HEAD + 1])[:, :V_HEAD])
    o_ref[...] = jnp.concatenate(outs, axis=1).astype(BF16)


def _attn(q, k, v, tq, tk):
    bsz, s, _ = q.shape
    hp = MLA_HEADS // 2
    kv = pl.BlockSpec((None, s, 2 * LANES), lambda b, h, qi: (b, 0, h))
    return pl.pallas_call(
        functools.partial(_attn_kernel, tq=tq, tk=tk),
        out_shape=jax.ShapeDtypeStruct((bsz, s, MLA_HEADS * V_HEAD), BF16),
        grid=(bsz, hp, s // tq),
        in_specs=[pl.BlockSpec((None, tq, 2 * LANES), lambda b, h, qi: (b, qi, h)), kv, kv],
        out_specs=pl.BlockSpec((None, tq, LANES), lambda b, h, qi: (b, qi, h)),
        scratch_shapes=[pltpu.VMEM((2, tq, LANES), F32), pltpu.VMEM((2, tq, LANES), F32)],
        compiler_params=_params(("parallel", "parallel", "arbitrary")),
        name="attn",
    )(q, k, v)


def _rot_half(w):
    w1, w2 = jnp.split(w, 2, axis=-1)
    return jnp.concatenate([-w2, w1], axis=-1)


def _tile_sizes(seq):
    tm = min(512, seq)
    return tm


def kernel(x, c, positions, ada_w, ada_b, norm_g, rwkv_mu, rwkv_w_rkv, rwkv_w0, rwkv_w1, rwkv_w2, rwkv_a0, rwkv_a1, rwkv_a2, rwkv_g1, rwkv_g2, rwkv_k_k, rwkv_k_a, rwkv_r_k, rwkv_lnx_w, rwkv_lnx_b, rwkv_w_o, kv_ada_w, kv_ada_b, kv_norm_g, mla_w_dkv, mla_g_kv, mla_w_uk, mla_w_uv, mla_w_kr, mla_w_dq, mla_g_q, mla_w_uq, mla_w_qr, mla_w_o, router_w, router_b, moe_w_gu, moe_w_down, final_g):
    bsz, seq, d = x.shape
    depth = ada_w.shape[0]
    n_a = rwkv_mu.shape[0]
    t = bsz * seq
    tm = _tile_sizes(seq)
    row = lambda a: a.reshape(1, -1).astype(F32)

    c_pad = jnp.pad(c, ((0, 8 - bsz), (0, 0)))
    mods = _ada(c_pad, ada_w.reshape(depth * 2, d, 3 * d), ada_b.reshape(depth * 2, 1, 3 * d))
    mods = mods[:, :bsz].reshape(depth, 2, bsz, 1, 3 * d)
    mod_kv = _ada(c_pad, kv_ada_w[None], kv_ada_b.reshape(1, 1, 2 * d))[0, :bsz].reshape(bsz, 1, 2 * d)

    idx = jnp.arange(MXU_DIM) // RWKV_HEAD_DIM
    bd = (idx[:, None] == idx[None, :]).astype(BF16)
    rw_t = router_w.T.astype(F32)
    rb_col = router_b.reshape(N_EXPERTS, 1).astype(F32)
    d_exp = moe_w_down.shape[2]
    f_all = N_EXPERTS * d_exp
    fc = 2 * d_exp
    exp_id = jnp.arange(f_all) // d_exp
    expand = (jnp.arange(N_EXPERTS)[:, None] == exp_id[None, :]).astype(BF16)
    expand = expand.reshape(N_EXPERTS, f_all // fc, fc).transpose(1, 0, 2)
    expand = jnp.concatenate([expand] * 3, axis=1)

    def moe_weights(layer):
        wgu = moe_w_gu[layer]
        wg = wgu[:, :, :d_exp].transpose(1, 0, 2).reshape(d, f_all).astype(BF16)
        wu = wgu[:, :, d_exp:].transpose(1, 0, 2).reshape(d, f_all).astype(BF16)
        wd = moe_w_down[layer].reshape(f_all, d).astype(BF16)
        return wg, wu, wd

    inv_freq = ROPE_THETA ** (-jnp.arange(0, QK_ROPE, 2, dtype=F32) / QK_ROPE)
    freq = jnp.zeros((1, LANES), F32).at[0, QK_NOPE:QK_NOPE + QK_ROPE].set(jnp.concatenate([inv_freq, inv_freq]))
    pos2d = positions.reshape(t, 1)

    x2d = x.reshape(t, d)
    assert depth - n_a == 1
    for layer in range(depth):
        mod_mix = mods[layer, 0]
        mod_ffn = mods[layer, 1]
        if layer < n_a:
            a = layer
            mu8 = jnp.pad(rwkv_mu[a], ((0, 2), (0, 0)))
            vecs = jnp.stack([rwkv_w0[a], rwkv_a0[a], rwkv_k_k[a], rwkv_k_a[a], rwkv_r_k[a].reshape(d),
                              jnp.zeros((d,), F32), jnp.zeros((d,), F32), jnp.zeros((d,), F32)])
            r, k, v, an, b, lw, gate, bonus = _rwkv_pre(
                x2d, mod_mix, row(norm_g[layer, 0]), mu8, rwkv_w_rkv[a].astype(BF16),
                rwkv_w1[a].astype(BF16), rwkv_w2[a].astype(BF16), rwkv_a1[a].astype(BF16), rwkv_a2[a].astype(BF16),
                rwkv_g1[a].astype(BF16), rwkv_g2[a].astype(BF16), vecs, bd, seq, tm)
            sh = lambda z: z.reshape(bsz, seq, d)
            ncs = min(8, seq // WKV_CHUNK)
            qm, y0, gm, cm = _wkv_a(sh(r), sh(k), sh(v), sh(an), sh(b), sh(lw), ncs)
            y = _wkv_b(qm, y0, gm, cm, min(2, seq // WKV_CHUNK)).reshape(t, d)
            x2d, h, comb = _post_call(
                _rwkv_post_kernel, "rwkv_post", [x2d, y, bonus, gate], [mod_mix, mod_ffn],
                [row(rwkv_lnx_w[a]), row(rwkv_lnx_b[a]), rwkv_w_o[a].astype(BF16), bd,
                 row(norm_g[layer, 1]), rw_t, rb_col], seq, tm)
        else:
            bl = layer - n_a
            scale = (QK_NOPE + QK_ROPE) ** -0.5 * LOG2_E
            hd = MLA_HEADS
            zpad = lambda w, lo, hi: jnp.pad(w, ((0, 0), (0, 0), (lo, hi)))
            kvl = mla_w_uk.shape[0]
            wuk = zpad(mla_w_uk, 0, LANES - QK_NOPE).reshape(kvl, hd * LANES).astype(BF16)
            wuv = zpad(mla_w_uv, 0, LANES - V_HEAD).reshape(kvl, hd * LANES).astype(BF16)
            vone = jnp.tile(jnp.zeros((LANES,), F32).at[V_HEAD].set(1.0), hd).reshape(1, hd * LANES)
            rope_pad = lambda w: jnp.pad(w, ((0, 0), (QK_NOPE, LANES - QK_NOPE - QK_ROPE))).astype(BF16)
            wkr = rope_pad(mla_w_kr)
            wkrr = rope_pad(_rot_half(mla_w_kr))
            ql = mla_w_uq.shape[1]
            wq = jnp.concatenate([mla_w_uq[bl], mla_w_qr[bl]], axis=-1) * scale
            wq = zpad(wq, 0, LANES - QK_NOPE - QK_ROPE).reshape(ql, hd * LANES).astype(BF16)
            wqr = zpad(_rot_half(mla_w_qr[bl]) * scale, QK_NOPE, LANES - QK_NOPE - QK_ROPE)
            wqr = wqr.reshape(ql, hd * LANES).astype(BF16)
            consts = [row(kv_norm_g), row(norm_g[layer, 0]), mla_w_dkv.astype(BF16), row(mla_g_kv), wuk, wuv,
                      wkr, wkrr, mla_w_dq[bl].astype(BF16), row(mla_g_q[bl]), wq, wqr, freq, vone]
            q, kf, vv = _mla_pre(x2d, pos2d, mod_kv, mod_mix, consts, seq, tm)
            ta = min(512, seq)
            o = _attn(q.reshape(bsz, seq, -1), kf.reshape(bsz, seq, -1), vv.reshape(bsz, seq, -1), ta, ta)
            x2d, h, comb = _post_call(
                _mla_post_kernel, "mla_post", [x2d, o.reshape(t, d)], [mod_mix, mod_ffn],
                [mla_w_o[bl].astype(BF16), row(norm_g[layer, 1]), rw_t, rb_col], seq, tm)
        wg, wu, wd = moe_weights(layer)
        x2d = _moe(x2d, h, comb, mod_ffn, wg, wu, wd, expand, row(final_g), layer == depth - 1, seq,
                   min(1024, seq), fc)
    return x2d.reshape(bsz, seq, d)
```

```python
import functools

import jax
import jax.numpy as jnp
from jax import lax
from jax.experimental import pallas as pl
from jax.experimental.pallas import tpu as pltpu

F32 = jnp.float32
BF16 = jnp.bfloat16

NORM_EPS = 1e-6
GN_EPS = 64e-5
ROPE_THETA = 10000.0
LOG2_E = 1.4426950408889634
RWKV_HEAD_DIM = 64
MLA_HEADS = 16
QK_NOPE = 64
QK_ROPE = 32
V_HEAD = 64
N_EXPERTS = 16
N_GROUPS = 4
EXPERTS_PER_GROUP = N_EXPERTS // N_GROUPS
TOP_K = 2

LANES = 128
MXU_DIM = 256
WKV_CHUNK = 64
ATTN_BLOCK = 512
ATTN_HEADS = 4
VMEM_LIMIT = 56 * 1024 * 1024


def _dot(a, b):
    return jnp.dot(a, b, preferred_element_type=F32)


def _dot_nt(a, b):
    return lax.dot_general(a, b, (((1,), (1,)), ((), ())), preferred_element_type=F32)


def _split2(x):
    hi = x.astype(BF16)
    lo = (x - hi.astype(F32)).astype(BF16)
    return hi, lo


def _split3(x):
    hi = x.astype(BF16)
    r1 = x - hi.astype(F32)
    mid = r1.astype(BF16)
    lo = (r1 - mid.astype(F32)).astype(BF16)
    return hi, mid, lo


def _dot_x3(a, b):
    ah, al = _split2(a)
    bh, bl = _split2(b)
    return _dot(ah, bh) + (_dot(ah, bl) + _dot(al, bh))


def _dot_exact_rhs(a, b_bf16):
    h, l = _split2(a)
    return _dot(h, b_bf16) + _dot(l, b_bf16)


def _rms(x):
    return x * lax.rsqrt(jnp.mean(x * x, axis=-1, keepdims=True) + NORM_EPS)


def _seg_bcast_sum(x, bd):
    d = x.shape[-1]
    outs = []
    for j in range(d // MXU_DIM):
        outs.append(_dot_exact_rhs(x[:, j * MXU_DIM:(j + 1) * MXU_DIM], bd))
    return jnp.concatenate(outs, axis=-1)


def _params(sem):
    return pltpu.CompilerParams(dimension_semantics=sem, vmem_limit_bytes=VMEM_LIMIT)


def _ada_kernel(c_ref, w_ref, b_ref, o_ref):
    c = c_ref[...]
    s = c * jax.nn.sigmoid(c)
    o_ref[0] = _dot_x3(s, w_ref[0]) + b_ref[0]


def _ada(c_pad, w, b, tn=512):
    n, d, nn = w.shape
    return pl.pallas_call(
        _ada_kernel,
        out_shape=jax.ShapeDtypeStruct((n, 8, nn), F32),
        grid=(n, nn // tn),
        in_specs=[pl.BlockSpec((8, d), lambda i, j: (0, 0)),
                  pl.BlockSpec((1, d, tn), lambda i, j: (i, 0, j)),
                  pl.BlockSpec((1, 1, tn), lambda i, j: (i, 0, j))],
        out_specs=pl.BlockSpec((1, 8, tn), lambda i, j: (i, 0, j)),
        compiler_params=_params(("parallel", "parallel")),
        name="ada",
    )(c_pad, w, b)


def _rwkv_pre_kernel(x_ref, xp_ref, mod_ref, g_ref, mu_ref, wrkv_ref, w1_ref, w2_ref, a1_ref, a2_ref,
                     g1_ref, g2_ref, vec_ref, bd_ref,
                     r_ref, k_ref, v_ref, an_ref, b_ref, lw_ref, gate_ref, bonus_ref, *, tiles_per_seq):
    d = x_ref.shape[-1]
    tm = x_ref.shape[0]
    i = pl.program_id(0)
    shift = mod_ref[:, 0:d]
    scale = mod_ref[:, d:2 * d]
    gn = g_ref[...]

    def modulate(xv):
        return _rms(xv) * gn * (1.0 + scale) + shift

    h = modulate(x_ref[...])
    hp_last = modulate(xp_ref[...])[7:8, :]
    hp_last = jnp.where(i % tiles_per_seq == 0, 0.0, hp_last)
    row = lax.broadcasted_iota(jnp.int32, (tm, d), 0)
    h_prev = jnp.where(row == 0, hp_last, pltpu.roll(h, 1, axis=0))
    xx = h_prev - h

    def mix(j):
        return (h + xx * mu_ref[j:j + 1, :]).astype(BF16)

    w0 = vec_ref[0:1, :]
    a0 = vec_ref[1:2, :]
    k_k = vec_ref[2:3, :]
    k_a = vec_ref[3:4, :]
    r_k = vec_ref[4:5, :]
    bd = bd_ref[...]

    r = _dot(mix(0), wrkv_ref[0])
    k = _dot(mix(2), wrkv_ref[1])
    v = _dot(mix(3), wrkv_ref[2])
    z = w0 + _dot(jnp.tanh(_dot(mix(1), w1_ref[...])).astype(BF16), w2_ref[...])
    w_log = -(jnp.maximum(-z, 0.0) + jnp.log(1.0 + jnp.exp(-jnp.abs(z)))) - 0.5
    lw_ref[...] = -jnp.exp(w_log)
    a = jax.nn.sigmoid(a0 + _dot(_dot(mix(4), a1_ref[...]).astype(BF16), a2_ref[...]))
    gate_ref[...] = _dot(jax.nn.sigmoid(_dot(mix(5), g1_ref[...])).astype(BF16), g2_ref[...]).astype(BF16)

    kk = k * k_k
    kk = kk / jnp.maximum(jnp.sqrt(_seg_bcast_sum(kk * kk, bd)), 1e-12)
    km = k * (1.0 + (a - 1.0) * k_a)
    bonus_ref[...] = (_seg_bcast_sum(r * km * r_k, bd) * v).astype(BF16)
    r_ref[...] = r.astype(BF16)
    k_ref[...] = km.astype(BF16)
    v_ref[...] = v.astype(BF16)
    an_ref[...] = (-kk).astype(BF16)
    b_ref[...] = (kk * a).astype(BF16)


def _rwkv_pre(x2d, mod, norm_g, mu8, wrkv, w1, w2, a1, a2, g1, g2, vecs, bd, seq, tm):
    t, d = x2d.shape
    tps = seq // tm
    full = lambda a: pl.BlockSpec(a.shape, lambda i: (0,) * a.ndim)
    tok = pl.BlockSpec((tm, d), lambda i: (i, 0))
    out_bf = jax.ShapeDtypeStruct((t, d), BF16)
    return pl.pallas_call(
        functools.partial(_rwkv_pre_kernel, tiles_per_seq=tps),
        out_shape=(out_bf, out_bf, out_bf, out_bf, out_bf, jax.ShapeDtypeStruct((t, d), F32), out_bf, out_bf),
        grid=(t // tm,),
        in_specs=[tok,
                  pl.BlockSpec((8, d), lambda i: (jnp.maximum(i * (tm // 8) - 1, 0), 0)),
                  pl.BlockSpec((None, 1, mod.shape[-1]), lambda i: (i // tps, 0, 0)),
                  full(norm_g), full(mu8), full(wrkv), full(w1), full(w2), full(a1), full(a2),
                  full(g1), full(g2), full(vecs), full(bd)],
        out_specs=(tok,) * 8,
        compiler_params=_params(("parallel",)),
        name="rwkv_pre",
    )(x2d, x2d, mod, norm_g, mu8, wrkv, w1, w2, a1, a2, g1, g2, vecs, bd)


def _wkv_a_kernel(r_ref, k_ref, v_ref, an_ref, b_ref, lw_ref, q_ref, y0_ref, g_ref, c_ref, *, n_chunks):
    L = WKV_CHUNK
    L2 = 2 * L
    lane = lax.broadcasted_iota(jnp.int32, (L, LANES), 1)
    head0 = lane < RWKV_HEAD_DIM
    ri = lax.broadcasted_iota(jnp.int32, (L2, L2), 0)
    ci = lax.broadcasted_iota(jnp.int32, (L2, L2), 1)
    strict = ci < ri
    incl = ci <= ri
    eye = ci == ri
    tri = (lax.broadcasted_iota(jnp.int32, (L, L), 1) <= lax.broadcasted_iota(jnp.int32, (L, L), 0)).astype(BF16)

    def stack(xv):
        return jnp.concatenate([jnp.where(head0, xv, 0.0), jnp.where(head0, 0.0, xv)], axis=0)

    chunks = range(n_chunks)
    pre = []
    for c in chunks:
        sl = pl.ds(c * L, L)
        lw = lw_ref[sl, :]
        cum = _dot_exact_rhs_lhs(tri, lw)
        cl = cum[L - 1:L, :]
        g_in = jnp.exp(cum)
        g_ex = jnp.exp(cum - lw)
        g_inv = jnp.exp(-cum)
        g_end = jnp.exp(cl - cum)
        kf = k_ref[sl, :].astype(F32)
        bf = b_ref[sl, :].astype(F32)
        pre.append(dict(
            a2=stack(an_ref[sl, :].astype(F32) * g_ex).astype(BF16),
            r2=stack(r_ref[sl, :].astype(F32) * g_in),
            b2=stack(bf * g_inv).astype(BF16),
            k2=stack(kf * g_inv).astype(BF16),
            v2=stack(v_ref[sl, :].astype(F32)).astype(BF16),
            bg2t=stack(bf * g_end).T.astype(BF16),
            kg2t=stack(kf * g_end).T.astype(BF16),
            g_last=jnp.exp(cl)))

    gram = [_dot_nt(jnp.concatenate([p["a2"], p["r2"].astype(BF16)], axis=0),
                    jnp.concatenate([p["b2"], p["k2"]], axis=0)) for p in pre]
    m_ab = [jnp.where(strict, g[:L2, :L2], 0.0) for g in gram]
    m_ak = [jnp.where(strict, g[:L2, L2:], 0.0) for g in gram]
    m_rb = [jnp.where(incl, g[L2:, :L2], 0.0) for g in gram]
    m_rk = [jnp.where(incl, g[L2:, L2:], 0.0) for g in gram]
    mv = [_dot(jnp.concatenate([m_ak[c], m_rk[c]], axis=0).astype(BF16), pre[c]["v2"]) for c in chunks]
    kgv = [_dot(p["kg2t"], p["v2"]) for p in pre]

    nb = [n.astype(BF16) for n in m_ab]
    nk = [_dot(n, n) for n in nb]
    tinv = [jnp.where(eye, 1.0, n) for n in m_ab]
    for step in range(1, 6):
        nb = [n.astype(BF16) for n in nk]
        if step < 5:
            both = [_dot(jnp.concatenate([nb[c], tinv[c].astype(BF16)], axis=0), nb[c]) for c in chunks]
            nk = [m[:L2] for m in both]
            tinv = [tinv[c] + both[c][L2:] for c in chunks]
        else:
            tinv = [tinv[c] + _dot(tinv[c].astype(BF16), nb[c]) for c in chunks]

    wu = [_dot(tinv[c].astype(BF16), jnp.concatenate([pre[c]["a2"], mv[c][:L2].astype(BF16)], axis=1)) for c in chunks]
    x = [_dot(jnp.concatenate([m_rb[c].astype(BF16), pre[c]["bg2t"]], axis=0), wu[c].astype(BF16)) for c in chunks]

    for c in chunks:
        q_ref[c, 0] = (pre[c]["r2"] + x[c][:L2, :L2]).astype(BF16)
        y02 = x[c][:L2, L2:] + mv[c][L2:]
        y0_ref[pl.ds(c * L, L), :] = y02[:L] + y02[L:]
        g_ref[c, 0] = (jnp.where(eye, jnp.broadcast_to(pre[c]["g_last"], (L2, L2)), 0.0) + x[c][L2:, :L2]).astype(BF16)
        c_ref[c, 0] = (x[c][L2:, L2:] + kgv[c]).astype(BF16)


def _dot_exact_rhs_lhs(tri_bf16, x):
    h, m, l = _split3(x)
    return _dot(tri_bf16, h) + (_dot(tri_bf16, m) + _dot(tri_bf16, l))


def _wkv_a(r, k, v, an, b, lw, n_chunks_step):
    bsz, s, d = r.shape
    L = WKV_CHUNK
    L2 = 2 * L
    pairs = d // LANES
    nck = s // L
    rows = n_chunks_step * L
    tok = pl.BlockSpec((None, rows, LANES), lambda bi, ci, pi: (bi, ci, pi))
    mat = pl.BlockSpec((n_chunks_step, 1, L2, L2), lambda bi, ci, pi: (ci, bi * pairs + pi, 0, 0))
    mshape = jax.ShapeDtypeStruct((nck, bsz * pairs, L2, L2), BF16)
    return pl.pallas_call(
        functools.partial(_wkv_a_kernel, n_chunks=n_chunks_step),
        out_shape=(mshape, jax.ShapeDtypeStruct((bsz, s, d), F32), mshape, mshape),
        grid=(bsz, nck // n_chunks_step, pairs),
        in_specs=[tok] * 6,
        out_specs=(mat, tok, mat, mat),
        compiler_params=_params(("parallel", "parallel", "parallel")),
        name="wkv_a",
    )(r, k, v, an, b, lw)


def _wkv_b_kernel(q_ref, y0_ref, g_ref, c_ref, y_ref, s_ref, *, n_chunks, bsz, pairs):
    L = WKV_CHUNK

    @pl.when(pl.program_id(0) == 0)
    def _():
        s_ref[...] = jnp.zeros_like(s_ref)

    for c in range(n_chunks):
        for bi in range(bsz):
            for pi in range(pairs):
                n = bi * pairs + pi
                sb = s_ref[n].astype(BF16)
                y2 = _dot(q_ref[c, n], sb)
                y_ref[bi, pl.ds(c * L, L), pl.ds(pi * LANES, LANES)] = (
                    y2[:L] + y2[L:] + y0_ref[bi, pl.ds(c * L, L), pl.ds(pi * LANES, LANES)])
                s_ref[n] = _dot(g_ref[c, n], sb) + c_ref[c, n].astype(F32)


def _wkv_b(q, y0, g, cmat, n_chunks_step):
    nck, bp, L2, _ = q.shape
    bsz, s, d = y0.shape
    pairs = d // LANES
    L = WKV_CHUNK
    rows = n_chunks_step * L
    mat = pl.BlockSpec((n_chunks_step, bp, L2, L2), lambda ci: (ci, 0, 0, 0))
    tok = pl.BlockSpec((bsz, rows, d), lambda ci: (0, ci, 0))
    return pl.pallas_call(
        functools.partial(_wkv_b_kernel, n_chunks=n_chunks_step, bsz=bsz, pairs=pairs),
        out_shape=jax.ShapeDtypeStruct((bsz, s, d), F32),
        grid=(nck // n_chunks_step,),
        in_specs=[mat, tok, mat, mat],
        out_specs=tok,
        scratch_shapes=[pltpu.VMEM((bp, L2, L2), F32)],
        compiler_params=_params(("arbitrary",)),
        name="wkv_b",
    )(q, y0, g, cmat)


def _router_combine(h, rw_ref, rb_ref):
    rw = rw_ref[...]
    hh, hl = _split2(h)
    wh, wl = _split2(rw)
    logits = _dot_nt(wh, hh) + (_dot_nt(wh, hl) + _dot_nt(wl, hh))
    aff = jax.nn.sigmoid(logits)
    sel = aff + rb_ref[...]
    affr = [aff[e:e + 1, :] for e in range(N_EXPERTS)]
    selr = [sel[e:e + 1, :] for e in range(N_EXPERTS)]
    top = []
    score = []
    for g in range(N_GROUPS):
        es = range(g * EXPERTS_PER_GROUP, (g + 1) * EXPERTS_PER_GROUP)
        sc = None
        for e in es:
            rank = None
            for o in es:
                if o == e:
                    continue
                beats = ((selr[o] >= selr[e]) if o < e else (selr[o] > selr[e])).astype(F32)
                rank = beats if rank is None else rank + beats
            t = rank < float(TOP_K)
            top.append(t)
            contrib = jnp.where(t, selr[e], 0.0)
            sc = contrib if sc is None else sc + contrib
        score.append(sc)
    best = score[0]
    for g in range(1, N_GROUPS):
        best = jnp.maximum(best, score[g])
    taken = None
    rows = []
    for g in range(N_GROUPS):
        is_best = score[g] == best
        if taken is not None:
            is_best = is_best & jnp.logical_not(taken)
        taken = is_best if taken is None else (taken | is_best)
        for e in range(g * EXPERTS_PER_GROUP, (g + 1) * EXPERTS_PER_GROUP):
            rows.append(jnp.where(is_best & top[e], affr[e], 0.0))
    comb = jnp.concatenate(rows, axis=0)
    comb = comb / jnp.sum(comb, axis=0, keepdims=True)
    return jnp.concatenate(_split3(comb), axis=0).T.astype(BF16)


def _moe_prologue(x_new, mod2_ref, g2_ref, rw_ref, rb_ref, h_ref, comb_ref):
    d = x_new.shape[-1]
    h = _rms(x_new) * g2_ref[...] * (1.0 + mod2_ref[:, d:2 * d]) + mod2_ref[:, 0:d]
    h_ref[...] = h.astype(BF16)
    comb_ref[...] = _router_combine(h, rw_ref, rb_ref)


def _rwkv_post_kernel(x_ref, y_ref, bonus_ref, gate_ref, mod_ref, mod2_ref, lnw_ref, lnb_ref, wo_ref, bd_ref,
                      g2_ref, rw_ref, rb_ref, xo_ref, h_ref, comb_ref):
    d = x_ref.shape[-1]
    bd = bd_ref[...]
    y = y_ref[...]
    inv_n = 1.0 / RWKV_HEAD_DIM
    yc = y - _seg_bcast_sum(y, bd) * inv_n
    var = _seg_bcast_sum(yc * yc, bd) * inv_n
    yn = yc * lax.rsqrt(var + GN_EPS) * lnw_ref[...] + lnb_ref[...]
    o = (yn + bonus_ref[...].astype(F32)) * gate_ref[...].astype(F32)
    mixed = _dot(o.astype(BF16), wo_ref[...])
    x_new = x_ref[...] + mod_ref[:, 2 * d:3 * d] * mixed
    xo_ref[...] = x_new
    _moe_prologue(x_new, mod2_ref, g2_ref, rw_ref, rb_ref, h_ref, comb_ref)


def _mla_post_kernel(x_ref, o_ref, mod_ref, mod2_ref, wo_ref, g2_ref, rw_ref, rb_ref, xo_ref, h_ref, comb_ref):
    d = x_ref.shape[-1]
    mixed = _dot(o_ref[...], wo_ref[...])
    x_new = x_ref[...] + mod_ref[:, 2 * d:3 * d] * mixed
    xo_ref[...] = x_new
    _moe_prologue(x_new, mod2_ref, g2_ref, rw_ref, rb_ref, h_ref, comb_ref)


def _post_call(kern, name, tok_inputs, mods, consts, seq, tm):
    t, d = tok_inputs[0].shape
    tps = seq // tm
    full = lambda a: pl.BlockSpec(a.shape, lambda i: (0,) * a.ndim)
    tok = pl.BlockSpec((tm, d), lambda i: (i, 0))
    modspec = lambda m: pl.BlockSpec((None, 1, m.shape[-1]), lambda i: (i // tps, 0, 0))
    return pl.pallas_call(
        kern,
        out_shape=(jax.ShapeDtypeStruct((t, d), F32), jax.ShapeDtypeStruct((t, d), BF16),
                   jax.ShapeDtypeStruct((t, 3 * N_EXPERTS), BF16)),
        grid=(t // tm,),
        in_specs=[tok] * len(tok_inputs) + [modspec(m) for m in mods] + [full(a) for a in consts],
        out_specs=(tok, tok, pl.BlockSpec((tm, 3 * N_EXPERTS), lambda i: (i, 0))),
        compiler_params=_params(("parallel",)),
        name=name,
    )(*tok_inputs, *mods, *consts)


def _moe_kernel(x_ref, h_ref, comb_ref, mod_ref, wg_ref, wu_ref, wd_ref, ex_ref, fg_ref, o_ref, acc_ref, *, final):
    d = x_ref.shape[-1]
    j = pl.program_id(1)

    @pl.when(j == 0)
    def _():
        acc_ref[...] = jnp.zeros_like(acc_ref)

    h = h_ref[...]
    gt = _dot(h, wg_ref[...])
    up = _dot(h, wu_ref[...])
    cs = _dot(comb_ref[...], ex_ref[0])
    act = (gt * jax.nn.sigmoid(gt)) * up * cs
    acc_ref[...] += _dot(act.astype(BF16), wd_ref[...])

    @pl.when(j == pl.num_programs(1) - 1)
    def _():
        x_new = x_ref[...] + mod_ref[:, 2 * d:3 * d] * acc_ref[...]
        o_ref[...] = _rms(x_new) * fg_ref[...] if final else x_new


def _moe(x2d, h, comb, mod, wg, wu, wd, expand, final_g, final, seq, tm, fc):
    t, d = x2d.shape
    f = wg.shape[1]
    tps = seq // tm
    tok = lambda w: pl.BlockSpec((tm, w), lambda i, j: (i, 0))
    return pl.pallas_call(
        functools.partial(_moe_kernel, final=final),
        out_shape=jax.ShapeDtypeStruct((t, d), F32),
        grid=(t // tm, f // fc),
        in_specs=[tok(d), tok(d), tok(3 * N_EXPERTS),
                  pl.BlockSpec((None, 1, mod.shape[-1]), lambda i, j: (i // tps, 0, 0)),
                  pl.BlockSpec((d, fc), lambda i, j: (0, j)),
                  pl.BlockSpec((d, fc), lambda i, j: (0, j)),
                  pl.BlockSpec((fc, d), lambda i, j: (j, 0)),
                  pl.BlockSpec((1, 3 * N_EXPERTS, fc), lambda i, j: (j, 0, 0)),
                  pl.BlockSpec((1, d), lambda i, j: (0, 0))],
        out_specs=tok(d),
        scratch_shapes=[pltpu.VMEM((tm, d), F32)],
        compiler_params=_params(("parallel", "arbitrary")),
        name="moe",
    )(x2d, h, comb, mod, wg, wu, wd, expand, final_g)


def _mla_pre_kernel(x_ref, pos_ref, modkv_ref, modq_ref, gkv_ref, gq_ref, wdkv_ref, gckv_ref, wuk_ref, wuvt_ref,
                    wka_ref, wkb_ref, wdq_ref, gcq_ref, wq_ref, freq_ref,
                    q_ref, k_ref, vt_ref):
    d = x_ref.shape[-1]
    tm = x_ref.shape[0]
    xn = _rms(x_ref[...])
    ang = pos_ref[...].astype(F32) * freq_ref[...]
    cos = jnp.cos(ang)
    sin = jnp.sin(ang)
    lane = lax.broadcasted_iota(jnp.int32, (tm, LANES), 1)
    trig_q = jnp.where(lane < QK_NOPE + QK_ROPE, cos, sin)

    hkv = (xn * gkv_ref[...] * (1.0 + modkv_ref[:, d:2 * d]) + modkv_ref[:, 0:d]).astype(BF16)
    ckv = (_rms(_dot(hkv, wdkv_ref[...])) * gckv_ref[...]).astype(BF16)
    vt = _dot_nt(wuvt_ref[...], ckv)
    row = lax.broadcasted_iota(jnp.int32, vt.shape, 0)
    vt_ref[...] = jnp.where((row & (LANES - 1)) == V_HEAD, 1.0, vt).astype(BF16)
    kr = _dot(hkv, wka_ref[...]) * cos + _dot(hkv, wkb_ref[...]) * sin
    kn = _dot(ckv, wuk_ref[...])

    hq = (xn * gq_ref[...] * (1.0 + modq_ref[:, d:2 * d]) + modq_ref[:, 0:d]).astype(BF16)
    cq = (_rms(_dot(hq, wdq_ref[...])) * gcq_ref[...]).astype(BF16)
    qa = _dot(cq, wq_ref[...])
    for hh in range(MLA_HEADS):
        sl = slice(hh * LANES, (hh + 1) * LANES)
        k_ref[:, sl] = (kn[:, sl] + kr).astype(BF16)
        q_ref[:, sl] = (qa[:, sl] * trig_q).astype(BF16)


def _mla_pre(x2d, pos2d, modkv, modq, consts, seq, tm):
    t, d = x2d.shape
    tps = seq // tm
    full = lambda a: pl.BlockSpec(a.shape, lambda i: (0,) * a.ndim)
    modspec = lambda m: pl.BlockSpec((None, 1, m.shape[-1]), lambda i: (i // tps, 0, 0))
    hq = MLA_HEADS * LANES
    return pl.pallas_call(
        _mla_pre_kernel,
        out_shape=(jax.ShapeDtypeStruct((t, hq), BF16), jax.ShapeDtypeStruct((t, hq), BF16),
                   jax.ShapeDtypeStruct((t // seq, tps, hq, tm), BF16)),
        grid=(t // tm,),
        in_specs=[pl.BlockSpec((tm, d), lambda i: (i, 0)), pl.BlockSpec((tm, 1), lambda i: (i, 0)),
                  modspec(modkv), modspec(modq)] + [full(a) for a in consts],
        out_specs=(pl.BlockSpec((tm, hq), lambda i: (i, 0)), pl.BlockSpec((tm, hq), lambda i: (i, 0)),
                   pl.BlockSpec((None, None, hq, tm), lambda i: (i // tps, i % tps, 0, 0))),
        compiler_params=_params(("parallel",)),
        name="mla_pre",
    )(x2d, pos2d, modkv, modq, *consts)


def _attn_kernel(q_ref, k_ref, vt_ref, o_ref, m_ref, acc_ref, *, blk):
    qi = pl.program_id(2)
    m_ref[...] = jnp.full_like(m_ref, -jnp.inf)
    acc_ref[...] = jnp.zeros_like(acc_ref)

    def block(j0, n_blk, diagonal=False):
        k0 = pl.multiple_of(j0 * blk, blk)
        width = n_blk * blk
        heads = range(ATTN_HEADS)
        sls = [slice(hh * LANES, (hh + 1) * LANES) for hh in heads]
        s = [_dot_nt(k_ref[pl.ds(k0, width), sl], q_ref[:, sl]) for sl in sls]
        if diagonal:
            key = lax.broadcasted_iota(jnp.int32, (width, blk), 0)
            qry = lax.broadcasted_iota(jnp.int32, (width, blk), 1) + (width - blk)
            s = [jnp.where(key <= qry, sh, -jnp.inf) for sh in s]
        m_old = [m_ref[hh] for hh in heads]
        m_new = [jnp.maximum(m_old[hh], jnp.max(s[hh], axis=0, keepdims=True)) for hh in heads]
        p = [jnp.exp2(s[hh] - m_new[hh]).astype(BF16) for hh in heads]
        for hh in heads:
            pv = _dot(vt_ref[j0, sls[hh], :], p[hh][:blk])
            for i in range(1, n_blk):
                pv = pv + _dot(vt_ref[j0 + i, sls[hh], :], p[hh][i * blk:(i + 1) * blk])
            acc_ref[hh] = acc_ref[hh] * jnp.exp2(m_old[hh] - m_new[hh]) + pv
            m_ref[hh] = m_new[hh]

    def body(j, carry):
        block(2 * j, 2)
        return carry

    lax.fori_loop(0, qi // 2, body, 0)

    @pl.when(qi % 2 == 0)
    def _():
        block(qi, 1, diagonal=True)

    @pl.when(qi % 2 == 1)
    def _():
        block(qi - 1, 2, diagonal=True)

    outs = []
    for hh in range(ATTN_HEADS):
        acc = acc_ref[hh].T
        outs.append((acc / acc[:, V_HEAD:V_HEAD + 1])[:, :V_HEAD])
    o_ref[...] = jnp.concatenate(outs, axis=1).astype(BF16)


def _attn(q, k, vt, blk):
    bsz, s, _ = q.shape
    hp = MLA_HEADS // ATTN_HEADS
    nh = ATTN_HEADS
    return pl.pallas_call(
        functools.partial(_attn_kernel, blk=blk),
        out_shape=jax.ShapeDtypeStruct((bsz, s, MLA_HEADS * V_HEAD), BF16),
        grid=(bsz, hp, s // blk),
        in_specs=[pl.BlockSpec((None, blk, nh * LANES), lambda b, h, qi: (b, qi, h)),
                  pl.BlockSpec((None, s, nh * LANES), lambda b, h, qi: (b, 0, h)),
                  pl.BlockSpec((None, s // blk, nh * LANES, blk), lambda b, h, qi: (b, 0, h, 0))],
        out_specs=pl.BlockSpec((None, blk, nh * V_HEAD), lambda b, h, qi: (b, qi, h)),
        scratch_shapes=[pltpu.VMEM((nh, 1, blk), F32), pltpu.VMEM((nh, LANES, blk), F32)],
        compiler_params=_params(("parallel", "parallel", "arbitrary")),
        name="attn",
    )(q, k, vt)


def kernel(x, c, positions, ada_w, ada_b, norm_g, rwkv_mu, rwkv_w_rkv, rwkv_w0, rwkv_w1, rwkv_w2, rwkv_a0, rwkv_a1, rwkv_a2, rwkv_g1, rwkv_g2, rwkv_k_k, rwkv_k_a, rwkv_r_k, rwkv_lnx_w, rwkv_lnx_b, rwkv_w_o, kv_ada_w, kv_ada_b, kv_norm_g, mla_w_dkv, mla_g_kv, mla_w_uk, mla_w_uv, mla_w_kr, mla_w_dq, mla_g_q, mla_w_uq, mla_w_qr, mla_w_o, router_w, router_b, moe_w_gu, moe_w_down, final_g):
    bsz, seq, d = x.shape
    depth = ada_w.shape[0]
    n_a = rwkv_mu.shape[0]
    t = bsz * seq
    tm = min(512, seq)
    row = lambda a: a.reshape(1, -1).astype(F32)

    c_pad = jnp.pad(c, ((0, 8 - bsz), (0, 0)))
    mods = _ada(c_pad, ada_w.reshape(depth * 2, d, 3 * d), ada_b.reshape(depth * 2, 1, 3 * d))
    mods = mods[:, :bsz].reshape(depth, 2, bsz, 1, 3 * d)
    mod_kv = _ada(c_pad, kv_ada_w[None], kv_ada_b.reshape(1, 1, 2 * d))[0, :bsz].reshape(bsz, 1, 2 * d)

    idx = jnp.arange(MXU_DIM) // RWKV_HEAD_DIM
    bd = (idx[:, None] == idx[None, :]).astype(BF16)
    rw_t = router_w.T.astype(F32)
    rb_col = router_b.reshape(N_EXPERTS, 1).astype(F32)
    d_exp = moe_w_down.shape[2]
    f_all = N_EXPERTS * d_exp
    fc = 2 * d_exp
    exp_id = jnp.arange(f_all) // d_exp
    expand = (jnp.arange(N_EXPERTS)[:, None] == exp_id[None, :]).astype(BF16)
    expand = expand.reshape(N_EXPERTS, f_all // fc, fc).transpose(1, 0, 2)
    expand = jnp.concatenate([expand] * 3, axis=1)

    def moe_weights(layer):
        wgu = moe_w_gu[layer]
        wg = wgu[:, :, :d_exp].transpose(1, 0, 2).reshape(d, f_all).astype(BF16)
        wu = wgu[:, :, d_exp:].transpose(1, 0, 2).reshape(d, f_all).astype(BF16)
        wd = moe_w_down[layer].reshape(f_all, d).astype(BF16)
        return wg, wu, wd

    inv_freq = ROPE_THETA ** (-jnp.arange(0, QK_ROPE, 2, dtype=F32) / QK_ROPE)
    freq = jnp.concatenate([jnp.zeros((QK_NOPE,), F32)] + [inv_freq] * 4).reshape(1, LANES)
    pos2d = positions.reshape(t, 1)

    x2d = x.reshape(t, d)
    assert depth - n_a == 1
    for layer in range(depth):
        mod_mix = mods[layer, 0]
        mod_ffn = mods[layer, 1]
        if layer < n_a:
            a = layer
            mu8 = jnp.pad(rwkv_mu[a], ((0, 2), (0, 0)))
            vecs = jnp.stack([rwkv_w0[a], rwkv_a0[a], rwkv_k_k[a], rwkv_k_a[a], rwkv_r_k[a].reshape(d),
                              jnp.zeros((d,), F32), jnp.zeros((d,), F32), jnp.zeros((d,), F32)])
            r, k, v, an, b, lw, gate, bonus = _rwkv_pre(
                x2d, mod_mix, row(norm_g[layer, 0]), mu8, rwkv_w_rkv[a].astype(BF16),
                rwkv_w1[a].astype(BF16), rwkv_w2[a].astype(BF16), rwkv_a1[a].astype(BF16), rwkv_a2[a].astype(BF16),
                rwkv_g1[a].astype(BF16), rwkv_g2[a].astype(BF16), vecs, bd, seq, tm)
            sh = lambda z: z.reshape(bsz, seq, d)
            ncs = min(8, seq // WKV_CHUNK)
            qm, y0, gm, cm = _wkv_a(sh(r), sh(k), sh(v), sh(an), sh(b), sh(lw), ncs)
            y = _wkv_b(qm, y0, gm, cm, min(2, seq // WKV_CHUNK)).reshape(t, d)
            x2d, h, comb = _post_call(
                _rwkv_post_kernel, "rwkv_post", [x2d, y, bonus, gate], [mod_mix, mod_ffn],
                [row(rwkv_lnx_w[a]), row(rwkv_lnx_b[a]), rwkv_w_o[a].astype(BF16), bd,
                 row(norm_g[layer, 1]), rw_t, rb_col], seq, tm)
        else:
            bl = layer - n_a
            scale = (QK_NOPE + QK_ROPE) ** -0.5 * LOG2_E
            hd = MLA_HEADS
            zpad = lambda w, lo, hi: jnp.pad(w, ((0, 0), (0, 0), (lo, hi)))
            kvl = mla_w_uk.shape[0]
            wuk = zpad(mla_w_uk, 0, LANES - QK_NOPE).reshape(kvl, hd * LANES).astype(BF16)
            wuvt = zpad(mla_w_uv, 0, LANES - V_HEAD).reshape(kvl, hd * LANES).T.astype(BF16)
            k1, k2 = jnp.split(mla_w_kr, 2, axis=-1)
            nope_pad = lambda w: jnp.pad(w, ((0, 0), (QK_NOPE, 0))).astype(BF16)
            wka = nope_pad(jnp.concatenate([k1, k2, k2, -k1], axis=-1))
            wkb = nope_pad(jnp.concatenate([-k2, k1, k1, k2], axis=-1))
            ql = mla_w_uq.shape[1]
            wq = jnp.concatenate([mla_w_uq[bl], mla_w_qr[bl], mla_w_qr[bl]], axis=-1) * scale
            wq = wq.reshape(ql, hd * LANES).astype(BF16)
            consts = [row(kv_norm_g), row(norm_g[layer, 0]), mla_w_dkv.astype(BF16), row(mla_g_kv), wuk, wuvt,
                      wka, wkb, mla_w_dq[bl].astype(BF16), row(mla_g_q[bl]), wq, freq]
            ta = min(ATTN_BLOCK, seq)
            q, kf, vt = _mla_pre(x2d, pos2d, mod_kv, mod_mix, consts, seq, ta)
            o = _attn(q.reshape(bsz, seq, -1), kf.reshape(bsz, seq, -1), vt, ta)
            x2d, h, comb = _post_call(
                _mla_post_kernel, "mla_post", [x2d, o.reshape(t, d)], [mod_mix, mod_ffn],
                [mla_w_o[bl].astype(BF16), row(norm_g[layer, 1]), rw_t, rb_col], seq, tm)
        wg, wu, wd = moe_weights(layer)
        x2d = _moe(x2d, h, comb, mod_ffn, wg, wu, wd, expand, row(final_g), layer == depth - 1, seq,
                   min(1024, seq), fc)
    return x2d.reshape(bsz, seq, d)
```

```python
import functools

import jax
import jax.numpy as jnp
from jax import lax
from jax.experimental import pallas as pl
from jax.experimental.pallas import tpu as pltpu

F32 = jnp.float32
BF16 = jnp.bfloat16

NORM_EPS = 1e-6
GN_EPS = 64e-5
ROPE_THETA = 10000.0
LOG2_E = 1.4426950408889634
RWKV_HEAD_DIM = 64
MLA_HEADS = 16
QK_NOPE = 64
QK_ROPE = 32
V_HEAD = 64
N_EXPERTS = 16
N_GROUPS = 4
EXPERTS_PER_GROUP = N_EXPERTS // N_GROUPS
TOP_K = 2

LANES = 128
MXU_DIM = 256
WKV_CHUNK = 64
ATTN_BLOCK = 512
ATTN_HEADS = 4
VMEM_LIMIT = 56 * 1024 * 1024


def _dot(a, b):
    return jnp.dot(a, b, preferred_element_type=F32)


def _dot_nt(a, b):
    return lax.dot_general(a, b, (((1,), (1,)), ((), ())), preferred_element_type=F32)


def _split2(x):
    hi = x.astype(BF16)
    lo = (x - hi.astype(F32)).astype(BF16)
    return hi, lo


def _split3(x):
    hi = x.astype(BF16)
    r1 = x - hi.astype(F32)
    mid = r1.astype(BF16)
    lo = (r1 - mid.astype(F32)).astype(BF16)
    return hi, mid, lo


def _dot_x3(a, b):
    ah, al = _split2(a)
    bh, bl = _split2(b)
    return _dot(ah, bh) + (_dot(ah, bl) + _dot(al, bh))


def _rms(x):
    return x * lax.rsqrt(jnp.mean(x * x, axis=-1, keepdims=True) + NORM_EPS)


def _seg_bcast_sum(x, bd, two_terms=False):
    d = x.shape[-1]
    terms = _split2(x) if two_terms else (x.astype(BF16),)
    outs = []
    for j in range(d // MXU_DIM):
        sl = slice(j * MXU_DIM, (j + 1) * MXU_DIM)
        acc = _dot(terms[0][:, sl], bd)
        for t in terms[1:]:
            acc = acc + _dot(t[:, sl], bd)
        outs.append(acc)
    return jnp.concatenate(outs, axis=-1)


def _params(sem):
    return pltpu.CompilerParams(dimension_semantics=sem, vmem_limit_bytes=VMEM_LIMIT)


def _ada_kernel(c_ref, w_ref, b_ref, o_ref):
    c = c_ref[...]
    s = c * jax.nn.sigmoid(c)
    o_ref[0] = _dot_x3(s, w_ref[0]) + b_ref[0]


def _ada(c_pad, w, b, tn=512):
    n, d, nn = w.shape
    return pl.pallas_call(
        _ada_kernel,
        out_shape=jax.ShapeDtypeStruct((n, 8, nn), F32),
        grid=(n, nn // tn),
        in_specs=[pl.BlockSpec((8, d), lambda i, j: (0, 0)),
                  pl.BlockSpec((1, d, tn), lambda i, j: (i, 0, j)),
                  pl.BlockSpec((1, 1, tn), lambda i, j: (i, 0, j))],
        out_specs=pl.BlockSpec((1, 8, tn), lambda i, j: (i, 0, j)),
        compiler_params=_params(("parallel", "parallel")),
        name="ada",
    )(c_pad, w, b)


def _rwkv_pre_kernel(x_ref, xp_ref, mod_ref, g_ref, mu_ref, wrkv_ref, w1_ref, w2_ref, a1_ref, a2_ref,
                     g1_ref, g2_ref, vec_ref, bd_ref,
                     r_ref, k_ref, v_ref, an_ref, b_ref, lw_ref, gate_ref, bonus_ref, *, tiles_per_seq):
    d = x_ref.shape[-1]
    tm = x_ref.shape[0]
    i = pl.program_id(0)
    shift = mod_ref[:, 0:d]
    scale = mod_ref[:, d:2 * d]
    gn = g_ref[...]

    def modulate(xv):
        return _rms(xv) * gn * (1.0 + scale) + shift

    h = modulate(x_ref[...])
    hp_last = modulate(xp_ref[...])[7:8, :]
    hp_last = jnp.where(i % tiles_per_seq == 0, 0.0, hp_last)
    row = lax.broadcasted_iota(jnp.int32, (tm, d), 0)
    h_prev = jnp.where(row == 0, hp_last, pltpu.roll(h, 1, axis=0))
    xx = h_prev - h

    def mix(j):
        return (h + xx * mu_ref[j:j + 1, :]).astype(BF16)

    w0 = vec_ref[0:1, :]
    a0 = vec_ref[1:2, :]
    k_k = vec_ref[2:3, :]
    k_a = vec_ref[3:4, :]
    r_k = vec_ref[4:5, :]
    bd = bd_ref[...]

    r = _dot(mix(0), wrkv_ref[0])
    k = _dot(mix(2), wrkv_ref[1])
    v = _dot(mix(3), wrkv_ref[2])
    z = w0 + _dot(jnp.tanh(_dot(mix(1), w1_ref[...])).astype(BF16), w2_ref[...])
    w_log = -(jnp.maximum(-z, 0.0) + jnp.log(1.0 + jnp.exp(-jnp.abs(z)))) - 0.5
    lw_ref[...] = -jnp.exp(w_log)
    a = jax.nn.sigmoid(a0 + _dot(_dot(mix(4), a1_ref[...]).astype(BF16), a2_ref[...]))
    gate_ref[...] = _dot(jax.nn.sigmoid(_dot(mix(5), g1_ref[...])).astype(BF16), g2_ref[...]).astype(BF16)

    kk = k * k_k
    kk = kk / jnp.maximum(jnp.sqrt(_seg_bcast_sum(kk * kk, bd)), 1e-12)
    km = k * (1.0 + (a - 1.0) * k_a)
    bonus_ref[...] = (_seg_bcast_sum(r * km * r_k, bd) * v).astype(BF16)
    r_ref[...] = r.astype(BF16)
    k_ref[...] = km.astype(BF16)
    v_ref[...] = v.astype(BF16)
    an_ref[...] = (-kk).astype(BF16)
    b_ref[...] = (kk * a).astype(BF16)


def _rwkv_pre(x2d, mod, norm_g, mu8, wrkv, w1, w2, a1, a2, g1, g2, vecs, bd, seq, tm):
    t, d = x2d.shape
    tps = seq // tm
    full = lambda a: pl.BlockSpec(a.shape, lambda i: (0,) * a.ndim)
    tok = pl.BlockSpec((tm, d), lambda i: (i, 0))
    out_bf = jax.ShapeDtypeStruct((t, d), BF16)
    return pl.pallas_call(
        functools.partial(_rwkv_pre_kernel, tiles_per_seq=tps),
        out_shape=(out_bf, out_bf, out_bf, out_bf, out_bf, jax.ShapeDtypeStruct((t, d), F32), out_bf, out_bf),
        grid=(t // tm,),
        in_specs=[tok,
                  pl.BlockSpec((8, d), lambda i: (jnp.maximum(i * (tm // 8) - 1, 0), 0)),
                  pl.BlockSpec((None, 1, mod.shape[-1]), lambda i: (i // tps, 0, 0)),
                  full(norm_g), full(mu8), full(wrkv), full(w1), full(w2), full(a1), full(a2),
                  full(g1), full(g2), full(vecs), full(bd)],
        out_specs=(tok,) * 8,
        compiler_params=_params(("parallel",)),
        name="rwkv_pre",
    )(x2d, x2d, mod, norm_g, mu8, wrkv, w1, w2, a1, a2, g1, g2, vecs, bd)


def _wkv_a_kernel(r_ref, k_ref, v_ref, an_ref, b_ref, lw_ref, q_ref, y0_ref, g_ref, c_ref, *, n_chunks):
    L = WKV_CHUNK
    L2 = 2 * L
    lane = lax.broadcasted_iota(jnp.int32, (L, LANES), 1)
    head0 = lane < RWKV_HEAD_DIM
    ri = lax.broadcasted_iota(jnp.int32, (L2, L2), 0)
    ci = lax.broadcasted_iota(jnp.int32, (L2, L2), 1)
    strict = ci < ri
    incl = ci <= ri
    eye = ci == ri
    tri = (lax.broadcasted_iota(jnp.int32, (L, L), 1) <= lax.broadcasted_iota(jnp.int32, (L, L), 0)).astype(BF16)

    def stack(xv):
        return jnp.concatenate([jnp.where(head0, xv, 0.0), jnp.where(head0, 0.0, xv)], axis=0)

    chunks = range(n_chunks)
    pre = []
    for c in chunks:
        sl = pl.ds(c * L, L)
        lw = lw_ref[sl, :]
        cum = _dot_exact_rhs_lhs(tri, lw)
        cl = cum[L - 1:L, :]
        g_in = jnp.exp(cum)
        g_ex = jnp.exp(cum - lw)
        g_inv = jnp.exp(-cum)
        g_end = jnp.exp(cl - cum)
        kf = k_ref[sl, :].astype(F32)
        bf = b_ref[sl, :].astype(F32)
        pre.append(dict(
            a2=stack(an_ref[sl, :].astype(F32) * g_ex).astype(BF16),
            r2=stack(r_ref[sl, :].astype(F32) * g_in),
            b2=stack(bf * g_inv).astype(BF16),
            k2=stack(kf * g_inv).astype(BF16),
            v2=stack(v_ref[sl, :].astype(F32)).astype(BF16),
            bg2t=stack(bf * g_end).T.astype(BF16),
            kg2t=stack(kf * g_end).T.astype(BF16),
            g_last=jnp.exp(cl)))

    gram = [_dot_nt(jnp.concatenate([p["a2"], p["r2"].astype(BF16)], axis=0),
                    jnp.concatenate([p["b2"], p["k2"]], axis=0)) for p in pre]
    m_ab = [jnp.where(strict, g[:L2, :L2], 0.0) for g in gram]
    m_ak = [jnp.where(strict, g[:L2, L2:], 0.0) for g in gram]
    m_rb = [jnp.where(incl, g[L2:, :L2], 0.0) for g in gram]
    m_rk = [jnp.where(incl, g[L2:, L2:], 0.0) for g in gram]
    mv = [_dot(jnp.concatenate([m_ak[c], m_rk[c]], axis=0).astype(BF16), pre[c]["v2"]) for c in chunks]
    kgv = [_dot(p["kg2t"], p["v2"]) for p in pre]

    nb = [n.astype(BF16) for n in m_ab]
    nk = [_dot(n, n) for n in nb]
    tinv = [jnp.where(eye, 1.0, n) for n in m_ab]
    for step in range(1, 6):
        nb = [n.astype(BF16) for n in nk]
        if step < 5:
            both = [_dot(jnp.concatenate([nb[c], tinv[c].astype(BF16)], axis=0), nb[c]) for c in chunks]
            nk = [m[:L2] for m in both]
            tinv = [tinv[c] + both[c][L2:] for c in chunks]
        else:
            tinv = [tinv[c] + _dot(tinv[c].astype(BF16), nb[c]) for c in chunks]

    wu = [_dot(tinv[c].astype(BF16), jnp.concatenate([pre[c]["a2"], mv[c][:L2].astype(BF16)], axis=1)) for c in chunks]
    x = [_dot(jnp.concatenate([m_rb[c].astype(BF16), pre[c]["bg2t"]], axis=0), wu[c].astype(BF16)) for c in chunks]

    for c in chunks:
        q_ref[c, 0] = (pre[c]["r2"] + x[c][:L2, :L2]).astype(BF16)
        y02 = x[c][:L2, L2:] + mv[c][L2:]
        y0_ref[pl.ds(c * L, L), :] = y02[:L] + y02[L:]
        g_ref[c, 0] = (jnp.where(eye, jnp.broadcast_to(pre[c]["g_last"], (L2, L2)), 0.0) + x[c][L2:, :L2]).astype(BF16)
        c_ref[c, 0] = (x[c][L2:, L2:] + kgv[c]).astype(BF16)


def _dot_exact_rhs_lhs(tri_bf16, x):
    h, m, l = _split3(x)
    return _dot(tri_bf16, h) + (_dot(tri_bf16, m) + _dot(tri_bf16, l))


def _wkv_a(r, k, v, an, b, lw, n_chunks_step):
    bsz, s, d = r.shape
    L = WKV_CHUNK
    L2 = 2 * L
    pairs = d // LANES
    nck = s // L
    rows = n_chunks_step * L
    tok = pl.BlockSpec((None, rows, LANES), lambda bi, ci, pi: (bi, ci, pi))
    mat = pl.BlockSpec((n_chunks_step, 1, L2, L2), lambda bi, ci, pi: (ci, bi * pairs + pi, 0, 0))
    mshape = jax.ShapeDtypeStruct((nck, bsz * pairs, L2, L2), BF16)
    return pl.pallas_call(
        functools.partial(_wkv_a_kernel, n_chunks=n_chunks_step),
        out_shape=(mshape, jax.ShapeDtypeStruct((bsz, s, d), F32), mshape, mshape),
        grid=(bsz, nck // n_chunks_step, pairs),
        in_specs=[tok] * 6,
        out_specs=(mat, tok, mat, mat),
        compiler_params=_params(("parallel", "parallel", "parallel")),
        name="wkv_a",
    )(r, k, v, an, b, lw)


def _wkv_b_kernel(q_ref, y0_ref, g_ref, c_ref, y_ref, s_ref, *, n_chunks, bsz, pairs):
    L = WKV_CHUNK

    @pl.when(pl.program_id(0) == 0)
    def _():
        s_ref[...] = jnp.zeros_like(s_ref)

    for c in range(n_chunks):
        for bi in range(bsz):
            for pi in range(pairs):
                n = bi * pairs + pi
                sb = s_ref[n].astype(BF16)
                y2 = _dot(q_ref[c, n], sb)
                y_ref[bi, pl.ds(c * L, L), pl.ds(pi * LANES, LANES)] = (
                    y2[:L] + y2[L:] + y0_ref[bi, pl.ds(c * L, L), pl.ds(pi * LANES, LANES)])
                s_ref[n] = _dot(g_ref[c, n], sb) + c_ref[c, n].astype(F32)


def _wkv_b(q, y0, g, cmat, n_chunks_step):
    nck, bp, L2, _ = q.shape
    bsz, s, d = y0.shape
    pairs = d // LANES
    L = WKV_CHUNK
    rows = n_chunks_step * L
    mat = pl.BlockSpec((n_chunks_step, bp, L2, L2), lambda ci: (ci, 0, 0, 0))
    tok = pl.BlockSpec((bsz, rows, d), lambda ci: (0, ci, 0))
    return pl.pallas_call(
        functools.partial(_wkv_b_kernel, n_chunks=n_chunks_step, bsz=bsz, pairs=pairs),
        out_shape=jax.ShapeDtypeStruct((bsz, s, d), F32),
        grid=(nck // n_chunks_step,),
        in_specs=[mat, tok, mat, mat],
        out_specs=tok,
        scratch_shapes=[pltpu.VMEM((bp, L2, L2), F32)],
        compiler_params=_params(("arbitrary",)),
        name="wkv_b",
    )(q, y0, g, cmat)


def _router_combine(h, rw_ref, rb_ref):
    rw = rw_ref[...]
    hh, hl = _split2(h)
    wh, wl = _split2(rw)
    logits = _dot_nt(wh, hh) + (_dot_nt(wh, hl) + _dot_nt(wl, hh))
    aff = jax.nn.sigmoid(logits)
    sel = aff + rb_ref[...]
    affr = [aff[e:e + 1, :] for e in range(N_EXPERTS)]
    selr = [sel[e:e + 1, :] for e in range(N_EXPERTS)]
    top = []
    score = []
    for g in range(N_GROUPS):
        es = range(g * EXPERTS_PER_GROUP, (g + 1) * EXPERTS_PER_GROUP)
        sc = None
        for e in es:
            rank = None
            for o in es:
                if o == e:
                    continue
                beats = ((selr[o] >= selr[e]) if o < e else (selr[o] > selr[e])).astype(F32)
                rank = beats if rank is None else rank + beats
            t = rank < float(TOP_K)
            top.append(t)
            contrib = jnp.where(t, selr[e], 0.0)
            sc = contrib if sc is None else sc + contrib
        score.append(sc)
    best = score[0]
    for g in range(1, N_GROUPS):
        best = jnp.maximum(best, score[g])
    taken = None
    rows = []
    for g in range(N_GROUPS):
        is_best = score[g] == best
        if taken is not None:
            is_best = is_best & jnp.logical_not(taken)
        taken = is_best if taken is None else (taken | is_best)
        for e in range(g * EXPERTS_PER_GROUP, (g + 1) * EXPERTS_PER_GROUP):
            rows.append(jnp.where(is_best & top[e], affr[e], 0.0))
    comb = jnp.concatenate(rows, axis=0)
    comb = comb / jnp.sum(comb, axis=0, keepdims=True)
    return jnp.concatenate(_split3(comb), axis=0).T.astype(BF16)


def _moe_prologue(x_new, mod2_ref, g2_ref, rw_ref, rb_ref, h_ref, comb_ref):
    d = x_new.shape[-1]
    h = _rms(x_new) * g2_ref[...] * (1.0 + mod2_ref[:, d:2 * d]) + mod2_ref[:, 0:d]
    h_ref[...] = h.astype(BF16)
    comb_ref[...] = _router_combine(h, rw_ref, rb_ref)


def _rwkv_post_kernel(x_ref, y_ref, bonus_ref, gate_ref, mod_ref, mod2_ref, lnw_ref, lnb_ref, wo_ref, bd_ref,
                      g2_ref, rw_ref, rb_ref, xo_ref, h_ref, comb_ref):
    d = x_ref.shape[-1]
    bd = bd_ref[...]
    y = y_ref[...]
    inv_n = 1.0 / RWKV_HEAD_DIM
    yc = y - _seg_bcast_sum(y, bd, two_terms=True) * inv_n
    var = _seg_bcast_sum(yc * yc, bd) * inv_n
    yn = yc * lax.rsqrt(var + GN_EPS) * lnw_ref[...] + lnb_ref[...]
    o = (yn + bonus_ref[...].astype(F32)) * gate_ref[...].astype(F32)
    mixed = _dot(o.astype(BF16), wo_ref[...])
    x_new = x_ref[...] + mod_ref[:, 2 * d:3 * d] * mixed
    xo_ref[...] = x_new
    _moe_prologue(x_new, mod2_ref, g2_ref, rw_ref, rb_ref, h_ref, comb_ref)


def _mla_post_kernel(x_ref, o_ref, mod_ref, mod2_ref, wo_ref, g2_ref, rw_ref, rb_ref, xo_ref, h_ref, comb_ref):
    d = x_ref.shape[-1]
    mixed = _dot(o_ref[...], wo_ref[...])
    x_new = x_ref[...] + mod_ref[:, 2 * d:3 * d] * mixed
    xo_ref[...] = x_new
    _moe_prologue(x_new, mod2_ref, g2_ref, rw_ref, rb_ref, h_ref, comb_ref)


def _post_call(kern, name, tok_inputs, mods, consts, seq, tm):
    t, d = tok_inputs[0].shape
    tps = seq // tm
    full = lambda a: pl.BlockSpec(a.shape, lambda i: (0,) * a.ndim)
    tok = pl.BlockSpec((tm, d), lambda i: (i, 0))
    modspec = lambda m: pl.BlockSpec((None, 1, m.shape[-1]), lambda i: (i // tps, 0, 0))
    return pl.pallas_call(
        kern,
        out_shape=(jax.ShapeDtypeStruct((t, d), F32), jax.ShapeDtypeStruct((t, d), BF16),
                   jax.ShapeDtypeStruct((t, 3 * N_EXPERTS), BF16)),
        grid=(t // tm,),
        in_specs=[tok] * len(tok_inputs) + [modspec(m) for m in mods] + [full(a) for a in consts],
        out_specs=(tok, tok, pl.BlockSpec((tm, 3 * N_EXPERTS), lambda i: (i, 0))),
        compiler_params=_params(("parallel",)),
        name=name,
    )(*tok_inputs, *mods, *consts)


def _moe_kernel(x_ref, h_ref, comb_ref, mod_ref, wg_ref, wu_ref, wd_ref, ex_ref, fg_ref, o_ref, acc_ref, *, final):
    d = x_ref.shape[-1]
    j = pl.program_id(1)

    @pl.when(j == 0)
    def _():
        acc_ref[...] = jnp.zeros_like(acc_ref)

    h = h_ref[...]
    gt = _dot(h, wg_ref[...])
    up = _dot(h, wu_ref[...])
    cs = _dot(comb_ref[...], ex_ref[0])
    act = (gt * jax.nn.sigmoid(gt)) * up * cs
    acc_ref[...] += _dot(act.astype(BF16), wd_ref[...])

    @pl.when(j == pl.num_programs(1) - 1)
    def _():
        x_new = x_ref[...] + mod_ref[:, 2 * d:3 * d] * acc_ref[...]
        o_ref[...] = _rms(x_new) * fg_ref[...] if final else x_new


def _moe(x2d, h, comb, mod, wg, wu, wd, expand, final_g, final, seq, tm, fc):
    t, d = x2d.shape
    f = wg.shape[1]
    tps = seq // tm
    tok = lambda w: pl.BlockSpec((tm, w), lambda i, j: (i, 0))
    return pl.pallas_call(
        functools.partial(_moe_kernel, final=final),
        out_shape=jax.ShapeDtypeStruct((t, d), F32),
        grid=(t // tm, f // fc),
        in_specs=[tok(d), tok(d), tok(3 * N_EXPERTS),
                  pl.BlockSpec((None, 1, mod.shape[-1]), lambda i, j: (i // tps, 0, 0)),
                  pl.BlockSpec((d, fc), lambda i, j: (0, j)),
                  pl.BlockSpec((d, fc), lambda i, j: (0, j)),
                  pl.BlockSpec((fc, d), lambda i, j: (j, 0)),
                  pl.BlockSpec((1, 3 * N_EXPERTS, fc), lambda i, j: (j, 0, 0)),
                  pl.BlockSpec((1, d), lambda i, j: (0, 0))],
        out_specs=tok(d),
        scratch_shapes=[pltpu.VMEM((tm, d), F32)],
        compiler_params=_params(("parallel", "arbitrary")),
        name="moe",
    )(x2d, h, comb, mod, wg, wu, wd, expand, final_g)


def _mla_pre_kernel(x_ref, pos_ref, modkv_ref, modq_ref, gkv_ref, gq_ref, wdkv_ref, gckv_ref, wuk_ref, wuvt_ref,
                    wka_ref, wkb_ref, wdq_ref, gcq_ref, wq_ref, freq_ref,
                    q_ref, k_ref, vt_ref):
    d = x_ref.shape[-1]
    tm = x_ref.shape[0]
    xn = _rms(x_ref[...])
    ang = pos_ref[...].astype(F32) * freq_ref[...]
    cos = jnp.cos(ang)
    sin = jnp.sin(ang)
    lane = lax.broadcasted_iota(jnp.int32, (tm, LANES), 1)
    trig_q = jnp.where(lane < QK_NOPE + QK_ROPE, cos, sin)

    hkv = (xn * gkv_ref[...] * (1.0 + modkv_ref[:, d:2 * d]) + modkv_ref[:, 0:d]).astype(BF16)
    ckv = (_rms(_dot(hkv, wdkv_ref[...])) * gckv_ref[...]).astype(BF16)
    vt = _dot_nt(wuvt_ref[...], ckv)
    row = lax.broadcasted_iota(jnp.int32, vt.shape, 0)
    vt_ref[...] = jnp.where((row & (LANES - 1)) == V_HEAD, 1.0, vt).astype(BF16)
    kr = _dot(hkv, wka_ref[...]) * cos + _dot(hkv, wkb_ref[...]) * sin
    kn = _dot(ckv, wuk_ref[...])

    hq = (xn * gq_ref[...] * (1.0 + modq_ref[:, d:2 * d]) + modq_ref[:, 0:d]).astype(BF16)
    cq = (_rms(_dot(hq, wdq_ref[...])) * gcq_ref[...]).astype(BF16)
    qa = _dot(cq, wq_ref[...])
    for hh in range(MLA_HEADS):
        sl = slice(hh * LANES, (hh + 1) * LANES)
        k_ref[:, sl] = (kn[:, sl] + kr).astype(BF16)
        q_ref[:, sl] = (qa[:, sl] * trig_q).astype(BF16)


def _mla_pre(x2d, pos2d, modkv, modq, consts, seq, tm):
    t, d = x2d.shape
    tps = seq // tm
    full = lambda a: pl.BlockSpec(a.shape, lambda i: (0,) * a.ndim)
    modspec = lambda m: pl.BlockSpec((None, 1, m.shape[-1]), lambda i: (i // tps, 0, 0))
    hq = MLA_HEADS * LANES
    return pl.pallas_call(
        _mla_pre_kernel,
        out_shape=(jax.ShapeDtypeStruct((t, hq), BF16), jax.ShapeDtypeStruct((t, hq), BF16),
                   jax.ShapeDtypeStruct((t // seq, tps, hq, tm), BF16)),
        grid=(t // tm,),
        in_specs=[pl.BlockSpec((tm, d), lambda i: (i, 0)), pl.BlockSpec((tm, 1), lambda i: (i, 0)),
                  modspec(modkv), modspec(modq)] + [full(a) for a in consts],
        out_specs=(pl.BlockSpec((tm, hq), lambda i: (i, 0)), pl.BlockSpec((tm, hq), lambda i: (i, 0)),
                   pl.BlockSpec((None, None, hq, tm), lambda i: (i // tps, i % tps, 0, 0))),
        compiler_params=_params(("parallel",)),
        name="mla_pre",
    )(x2d, pos2d, modkv, modq, *consts)


def _attn_kernel(q_ref, k_ref, vt_ref, o_ref, m_ref, acc_ref, *, blk):
    qi = pl.program_id(2)
    m_ref[...] = jnp.full_like(m_ref, -jnp.inf)
    acc_ref[...] = jnp.zeros_like(acc_ref)

    def block(j0, n_blk, diagonal=False):
        k0 = pl.multiple_of(j0 * blk, blk)
        width = n_blk * blk
        heads = range(ATTN_HEADS)
        sls = [slice(hh * LANES, (hh + 1) * LANES) for hh in heads]
        s = [_dot_nt(k_ref[pl.ds(k0, width), sl], q_ref[:, sl]) for sl in sls]
        if diagonal:
            key = lax.broadcasted_iota(jnp.int32, (width, blk), 0)
            qry = lax.broadcasted_iota(jnp.int32, (width, blk), 1) + (width - blk)
            s = [jnp.where(key <= qry, sh, -jnp.inf) for sh in s]
        for hh in heads:
            m_old = m_ref[hh]
            m_new = jnp.maximum(m_old, jnp.max(s[hh], axis=0, keepdims=True))
            p = jnp.exp2(s[hh] - m_new).astype(BF16)
            pv = _dot(vt_ref[j0, sls[hh], :], p[:blk])
            for i in range(1, n_blk):
                pv = pv + _dot(vt_ref[j0 + i, sls[hh], :], p[i * blk:(i + 1) * blk])
            acc_ref[hh] = acc_ref[hh] * jnp.exp2(m_old - m_new) + pv
            m_ref[hh] = m_new

    n_pairs = qi // 2

    def body(j, carry):
        block(4 * j, 4)
        return carry

    lax.fori_loop(0, n_pairs // 2, body, 0)

    @pl.when(n_pairs % 2 == 1)
    def _():
        block(2 * (n_pairs - 1), 2)

    @pl.when(qi % 2 == 0)
    def _():
        block(qi, 1, diagonal=True)

    @pl.when(qi % 2 == 1)
    def _():
        block(qi - 1, 2, diagonal=True)

    outs = []
    for hh in range(ATTN_HEADS):
        acc = acc_ref[hh].T
        outs.append((acc / acc[:, V_HEAD:V_HEAD + 1])[:, :V_HEAD])
    o_ref[...] = jnp.concatenate(outs, axis=1).astype(BF16)


def _attn(q, k, vt, blk):
    bsz, s, _ = q.shape
    hp = MLA_HEADS // ATTN_HEADS
    nh = ATTN_HEADS
    return pl.pallas_call(
        functools.partial(_attn_kernel, blk=blk),
        out_shape=jax.ShapeDtypeStruct((bsz, s, MLA_HEADS * V_HEAD), BF16),
        grid=(bsz, hp, s // blk),
        in_specs=[pl.BlockSpec((None, blk, nh * LANES), lambda b, h, qi: (b, qi, h)),
                  pl.BlockSpec((None, s, nh * LANES), lambda b, h, qi: (b, 0, h), pipeline_mode=pl.Buffered(1)),
                  pl.BlockSpec((None, s // blk, nh * LANES, blk), lambda b, h, qi: (b, 0, h, 0),
                               pipeline_mode=pl.Buffered(1))],
        out_specs=pl.BlockSpec((None, blk, nh * V_HEAD), lambda b, h, qi: (b, qi, h)),
        scratch_shapes=[pltpu.VMEM((nh, 1, blk), F32), pltpu.VMEM((nh, LANES, blk), F32)],
        compiler_params=_params(("parallel", "parallel", "arbitrary")),
        name="attn",
    )(q, k, vt)


def kernel(x, c, positions, ada_w, ada_b, norm_g, rwkv_mu, rwkv_w_rkv, rwkv_w0, rwkv_w1, rwkv_w2, rwkv_a0, rwkv_a1, rwkv_a2, rwkv_g1, rwkv_g2, rwkv_k_k, rwkv_k_a, rwkv_r_k, rwkv_lnx_w, rwkv_lnx_b, rwkv_w_o, kv_ada_w, kv_ada_b, kv_norm_g, mla_w_dkv, mla_g_kv, mla_w_uk, mla_w_uv, mla_w_kr, mla_w_dq, mla_g_q, mla_w_uq, mla_w_qr, mla_w_o, router_w, router_b, moe_w_gu, moe_w_down, final_g):
    bsz, seq, d = x.shape
    depth = ada_w.shape[0]
    n_a = rwkv_mu.shape[0]
    t = bsz * seq
    tm = min(512, seq)
    row = lambda a: a.reshape(1, -1).astype(F32)

    c_pad = jnp.pad(c, ((0, 8 - bsz), (0, 0)))
    mods = _ada(c_pad, ada_w.reshape(depth * 2, d, 3 * d), ada_b.reshape(depth * 2, 1, 3 * d))
    mods = mods[:, :bsz].reshape(depth, 2, bsz, 1, 3 * d)
    mod_kv = _ada(c_pad, kv_ada_w[None], kv_ada_b.reshape(1, 1, 2 * d))[0, :bsz].reshape(bsz, 1, 2 * d)

    idx = jnp.arange(MXU_DIM) // RWKV_HEAD_DIM
    bd = (idx[:, None] == idx[None, :]).astype(BF16)
    rw_t = router_w.T.astype(F32)
    rb_col = router_b.reshape(N_EXPERTS, 1).astype(F32)
    d_exp = moe_w_down.shape[2]
    f_all = N_EXPERTS * d_exp
    fc = 2 * d_exp
    exp_id = jnp.arange(f_all) // d_exp
    expand = (jnp.arange(N_EXPERTS)[:, None] == exp_id[None, :]).astype(BF16)
    expand = expand.reshape(N_EXPERTS, f_all // fc, fc).transpose(1, 0, 2)
    expand = jnp.concatenate([expand] * 3, axis=1)

    def moe_weights(layer):
        wgu = moe_w_gu[layer]
        wg = wgu[:, :, :d_exp].transpose(1, 0, 2).reshape(d, f_all).astype(BF16)
        wu = wgu[:, :, d_exp:].transpose(1, 0, 2).reshape(d, f_all).astype(BF16)
        wd = moe_w_down[layer].reshape(f_all, d).astype(BF16)
        return wg, wu, wd

    inv_freq = ROPE_THETA ** (-jnp.arange(0, QK_ROPE, 2, dtype=F32) / QK_ROPE)
    freq = jnp.concatenate([jnp.zeros((QK_NOPE,), F32)] + [inv_freq] * 4).reshape(1, LANES)
    pos2d = positions.reshape(t, 1)

    x2d = x.reshape(t, d)
    assert depth - n_a == 1
    for layer in range(depth):
        mod_mix = mods[layer, 0]
        mod_ffn = mods[layer, 1]
        if layer < n_a:
            a = layer
            mu8 = jnp.pad(rwkv_mu[a], ((0, 2), (0, 0)))
            vecs = jnp.stack([rwkv_w0[a], rwkv_a0[a], rwkv_k_k[a], rwkv_k_a[a], rwkv_r_k[a].reshape(d),
                              jnp.zeros((d,), F32), jnp.zeros((d,), F32), jnp.zeros((d,), F32)])
            r, k, v, an, b, lw, gate, bonus = _rwkv_pre(
                x2d, mod_mix, row(norm_g[layer, 0]), mu8, rwkv_w_rkv[a].astype(BF16),
                rwkv_w1[a].astype(BF16), rwkv_w2[a].astype(BF16), rwkv_a1[a].astype(BF16), rwkv_a2[a].astype(BF16),
                rwkv_g1[a].astype(BF16), rwkv_g2[a].astype(BF16), vecs, bd, seq, tm)
            sh = lambda z: z.reshape(bsz, seq, d)
            ncs = min(8, seq // WKV_CHUNK)
            qm, y0, gm, cm = _wkv_a(sh(r), sh(k), sh(v), sh(an), sh(b), sh(lw), ncs)
            y = _wkv_b(qm, y0, gm, cm, min(2, seq // WKV_CHUNK)).reshape(t, d)
            x2d, h, comb = _post_call(
                _rwkv_post_kernel, "rwkv_post", [x2d, y, bonus, gate], [mod_mix, mod_ffn],
                [row(rwkv_lnx_w[a]), row(rwkv_lnx_b[a]), rwkv_w_o[a].astype(BF16), bd,
                 row(norm_g[layer, 1]), rw_t, rb_col], seq, tm)
        else:
            bl = layer - n_a
            scale = (QK_NOPE + QK_ROPE) ** -0.5 * LOG2_E
            hd = MLA_HEADS
            zpad = lambda w, lo, hi: jnp.pad(w, ((0, 0), (0, 0), (lo, hi)))
            kvl = mla_w_uk.shape[0]
            wuk = zpad(mla_w_uk, 0, LANES - QK_NOPE).reshape(kvl, hd * LANES).astype(BF16)
            wuvt = zpad(mla_w_uv, 0, LANES - V_HEAD).reshape(kvl, hd * LANES).T.astype(BF16)
            k1, k2 = jnp.split(mla_w_kr, 2, axis=-1)
            nope_pad = lambda w: jnp.pad(w, ((0, 0), (QK_NOPE, 0))).astype(BF16)
            wka = nope_pad(jnp.concatenate([k1, k2, k2, -k1], axis=-1))
            wkb = nope_pad(jnp.concatenate([-k2, k1, k1, k2], axis=-1))
            ql = mla_w_uq.shape[1]
            wq = jnp.concatenate([mla_w_uq[bl], mla_w_qr[bl], mla_w_qr[bl]], axis=-1) * scale
            wq = wq.reshape(ql, hd * LANES).astype(BF16)
            consts = [row(kv_norm_g), row(norm_g[layer, 0]), mla_w_dkv.astype(BF16), row(mla_g_kv), wuk, wuvt,
                      wka, wkb, mla_w_dq[bl].astype(BF16), row(mla_g_q[bl]), wq, freq]
            ta = min(ATTN_BLOCK, seq)
            q, kf, vt = _mla_pre(x2d, pos2d, mod_kv, mod_mix, consts, seq, ta)
            o = _attn(q.reshape(bsz, seq, -1), kf.reshape(bsz, seq, -1), vt, ta)
            x2d, h, comb = _post_call(
                _mla_post_kernel, "mla_post", [x2d, o.reshape(t, d)], [mod_mix, mod_ffn],
                [mla_w_o[bl].astype(BF16), row(norm_g[layer, 1]), rw_t, rb_col], seq, tm)
        wg, wu, wd = moe_weights(layer)
        x2d = _moe(x2d, h, comb, mod_ffn, wg, wu, wd, expand, row(final_g), layer == depth - 1, seq,
                   min(1024, seq), fc)
    return x2d.reshape(bsz, seq, d)
```

```python
import functools

import jax
import jax.numpy as jnp
from jax import lax
from jax.experimental import pallas as pl
from jax.experimental.pallas import tpu as pltpu

F32 = jnp.float32
BF16 = jnp.bfloat16

NORM_EPS = 1e-6
GN_EPS = 64e-5
ROPE_THETA = 10000.0
LOG2_E = 1.4426950408889634
RWKV_HEAD_DIM = 64
MLA_HEADS = 16
QK_NOPE = 64
QK_ROPE = 32
V_HEAD = 64
N_EXPERTS = 16
N_GROUPS = 4
EXPERTS_PER_GROUP = N_EXPERTS // N_GROUPS
TOP_K = 2

LANES = 128
MXU_DIM = 256
WKV_CHUNK = 64
ATTN_BLOCK = 512
ATTN_HEADS = 4
MOE_ROWS = 256
MOE_EXTRA_ROWS = 64
VMEM_LIMIT = 56 * 1024 * 1024


def _dot(a, b):
    return jnp.dot(a, b, preferred_element_type=F32)


def _dot_nt(a, b):
    return lax.dot_general(a, b, (((1,), (1,)), ((), ())), preferred_element_type=F32)


def _split2(x):
    hi = x.astype(BF16)
    lo = (x - hi.astype(F32)).astype(BF16)
    return hi, lo


def _split3(x):
    hi = x.astype(BF16)
    r1 = x - hi.astype(F32)
    mid = r1.astype(BF16)
    lo = (r1 - mid.astype(F32)).astype(BF16)
    return hi, mid, lo


def _dot_x3(a, b):
    ah, al = _split2(a)
    bh, bl = _split2(b)
    return _dot(ah, bh) + (_dot(ah, bl) + _dot(al, bh))


def _rms(x):
    return x * lax.rsqrt(jnp.mean(x * x, axis=-1, keepdims=True) + NORM_EPS)


def _seg_bcast_sum(x, bd, two_terms=False):
    d = x.shape[-1]
    terms = _split2(x) if two_terms else (x.astype(BF16),)
    outs = []
    for j in range(d // MXU_DIM):
        sl = slice(j * MXU_DIM, (j + 1) * MXU_DIM)
        acc = _dot(terms[0][:, sl], bd)
        for t in terms[1:]:
            acc = acc + _dot(t[:, sl], bd)
        outs.append(acc)
    return jnp.concatenate(outs, axis=-1)


def _params(sem):
    return pltpu.CompilerParams(dimension_semantics=sem, vmem_limit_bytes=VMEM_LIMIT)


def _ada_kernel(c_ref, w_ref, b_ref, o_ref):
    c = c_ref[...]
    s = c * jax.nn.sigmoid(c)
    o_ref[0] = _dot_x3(s, w_ref[0]) + b_ref[0]


def _ada(c_pad, w, b, tn=512):
    n, d, nn = w.shape
    return pl.pallas_call(
        _ada_kernel,
        out_shape=jax.ShapeDtypeStruct((n, 8, nn), F32),
        grid=(n, nn // tn),
        in_specs=[pl.BlockSpec((8, d), lambda i, j: (0, 0)),
                  pl.BlockSpec((1, d, tn), lambda i, j: (i, 0, j)),
                  pl.BlockSpec((1, 1, tn), lambda i, j: (i, 0, j))],
        out_specs=pl.BlockSpec((1, 8, tn), lambda i, j: (i, 0, j)),
        compiler_params=_params(("parallel", "parallel")),
        name="ada",
    )(c_pad, w, b)


def _rwkv_pre_kernel(x_ref, xp_ref, mod_ref, g_ref, mu_ref, wrkv_ref, w1_ref, w2_ref, a1_ref, a2_ref,
                     g1_ref, g2_ref, vec_ref, bd_ref,
                     r_ref, k_ref, v_ref, an_ref, b_ref, lw_ref, gate_ref, bonus_ref, *, tiles_per_seq):
    d = x_ref.shape[-1]
    tm = x_ref.shape[0]
    i = pl.program_id(0)
    shift = mod_ref[:, 0:d]
    scale = mod_ref[:, d:2 * d]
    gn = g_ref[...]

    def modulate(xv):
        return _rms(xv) * gn * (1.0 + scale) + shift

    h = modulate(x_ref[...])
    hp_last = modulate(xp_ref[...])[7:8, :]
    hp_last = jnp.where(i % tiles_per_seq == 0, 0.0, hp_last)
    row = lax.broadcasted_iota(jnp.int32, (tm, d), 0)
    h_prev = jnp.where(row == 0, hp_last, pltpu.roll(h, 1, axis=0))
    xx = h_prev - h

    def mix(j):
        return (h + xx * mu_ref[j:j + 1, :]).astype(BF16)

    w0 = vec_ref[0:1, :]
    a0 = vec_ref[1:2, :]
    k_k = vec_ref[2:3, :]
    k_a = vec_ref[3:4, :]
    r_k = vec_ref[4:5, :]
    bd = bd_ref[...]

    r = _dot(mix(0), wrkv_ref[0])
    k = _dot(mix(2), wrkv_ref[1])
    v = _dot(mix(3), wrkv_ref[2])
    z = w0 + _dot(jnp.tanh(_dot(mix(1), w1_ref[...])).astype(BF16), w2_ref[...])
    w_log = -(jnp.maximum(-z, 0.0) + jnp.log(1.0 + jnp.exp(-jnp.abs(z)))) - 0.5
    lw_ref[...] = -jnp.exp(w_log)
    a = jax.nn.sigmoid(a0 + _dot(_dot(mix(4), a1_ref[...]).astype(BF16), a2_ref[...]))
    gate_ref[...] = _dot(jax.nn.sigmoid(_dot(mix(5), g1_ref[...])).astype(BF16), g2_ref[...]).astype(BF16)

    kk = k * k_k
    kk = kk / jnp.maximum(jnp.sqrt(_seg_bcast_sum(kk * kk, bd)), 1e-12)
    km = k * (1.0 + (a - 1.0) * k_a)
    bonus_ref[...] = (_seg_bcast_sum(r * km * r_k, bd) * v).astype(BF16)
    r_ref[...] = r.astype(BF16)
    k_ref[...] = km.astype(BF16)
    v_ref[...] = v.astype(BF16)
    an_ref[...] = (-kk).astype(BF16)
    b_ref[...] = (kk * a).astype(BF16)


def _rwkv_pre(x2d, mod, norm_g, mu8, wrkv, w1, w2, a1, a2, g1, g2, vecs, bd, seq, tm):
    t, d = x2d.shape
    tps = seq // tm
    full = lambda a: pl.BlockSpec(a.shape, lambda i: (0,) * a.ndim)
    tok = pl.BlockSpec((tm, d), lambda i: (i, 0))
    out_bf = jax.ShapeDtypeStruct((t, d), BF16)
    return pl.pallas_call(
        functools.partial(_rwkv_pre_kernel, tiles_per_seq=tps),
        out_shape=(out_bf, out_bf, out_bf, out_bf, out_bf, jax.ShapeDtypeStruct((t, d), F32), out_bf, out_bf),
        grid=(t // tm,),
        in_specs=[tok,
                  pl.BlockSpec((8, d), lambda i: (jnp.maximum(i * (tm // 8) - 1, 0), 0)),
                  pl.BlockSpec((None, 1, mod.shape[-1]), lambda i: (i // tps, 0, 0)),
                  full(norm_g), full(mu8), full(wrkv), full(w1), full(w2), full(a1), full(a2),
                  full(g1), full(g2), full(vecs), full(bd)],
        out_specs=(tok,) * 8,
        compiler_params=_params(("parallel",)),
        name="rwkv_pre",
    )(x2d, x2d, mod, norm_g, mu8, wrkv, w1, w2, a1, a2, g1, g2, vecs, bd)


def _wkv_a_kernel(r_ref, k_ref, v_ref, an_ref, b_ref, lw_ref, q_ref, y0_ref, g_ref, c_ref, *, n_chunks):
    L = WKV_CHUNK
    L2 = 2 * L
    lane = lax.broadcasted_iota(jnp.int32, (L, LANES), 1)
    head0 = lane < RWKV_HEAD_DIM
    ri = lax.broadcasted_iota(jnp.int32, (L2, L2), 0)
    ci = lax.broadcasted_iota(jnp.int32, (L2, L2), 1)
    strict = ci < ri
    incl = ci <= ri
    eye = ci == ri
    tri = (lax.broadcasted_iota(jnp.int32, (L, L), 1) <= lax.broadcasted_iota(jnp.int32, (L, L), 0)).astype(BF16)

    def stack(xv):
        return jnp.concatenate([jnp.where(head0, xv, 0.0), jnp.where(head0, 0.0, xv)], axis=0)

    chunks = range(n_chunks)
    pre = []
    for c in chunks:
        sl = pl.ds(c * L, L)
        lw = lw_ref[sl, :]
        cum = _dot_exact_rhs_lhs(tri, lw)
        cl = cum[L - 1:L, :]
        g_in = jnp.exp(cum)
        g_ex = jnp.exp(cum - lw)
        g_inv = jnp.exp(-cum)
        g_end = jnp.exp(cl - cum)
        kf = k_ref[sl, :].astype(F32)
        bf = b_ref[sl, :].astype(F32)
        pre.append(dict(
            a2=stack(an_ref[sl, :].astype(F32) * g_ex).astype(BF16),
            r2=stack(r_ref[sl, :].astype(F32) * g_in),
            b2=stack(bf * g_inv).astype(BF16),
            k2=stack(kf * g_inv).astype(BF16),
            v2=stack(v_ref[sl, :].astype(F32)).astype(BF16),
            bg2t=stack(bf * g_end).T.astype(BF16),
            kg2t=stack(kf * g_end).T.astype(BF16),
            g_last=jnp.exp(cl)))

    gram = [_dot_nt(jnp.concatenate([p["a2"], p["r2"].astype(BF16)], axis=0),
                    jnp.concatenate([p["b2"], p["k2"]], axis=0)) for p in pre]
    m_ab = [jnp.where(strict, g[:L2, :L2], 0.0) for g in gram]
    m_ak = [jnp.where(strict, g[:L2, L2:], 0.0) for g in gram]
    m_rb = [jnp.where(incl, g[L2:, :L2], 0.0) for g in gram]
    m_rk = [jnp.where(incl, g[L2:, L2:], 0.0) for g in gram]
    mv = [_dot(jnp.concatenate([m_ak[c], m_rk[c]], axis=0).astype(BF16), pre[c]["v2"]) for c in chunks]
    kgv = [_dot(p["kg2t"], p["v2"]) for p in pre]

    nb = [n.astype(BF16) for n in m_ab]
    nk = [_dot(n, n) for n in nb]
    tinv = [jnp.where(eye, 1.0, n) for n in m_ab]
    for step in range(1, 6):
        nb = [n.astype(BF16) for n in nk]
        if step < 5:
            both = [_dot(jnp.concatenate([nb[c], tinv[c].astype(BF16)], axis=0), nb[c]) for c in chunks]
            nk = [m[:L2] for m in both]
            tinv = [tinv[c] + both[c][L2:] for c in chunks]
        else:
            tinv = [tinv[c] + _dot(tinv[c].astype(BF16), nb[c]) for c in chunks]

    wu = [_dot(tinv[c].astype(BF16), jnp.concatenate([pre[c]["a2"], mv[c][:L2].astype(BF16)], axis=1)) for c in chunks]
    x = [_dot(jnp.concatenate([m_rb[c].astype(BF16), pre[c]["bg2t"]], axis=0), wu[c].astype(BF16)) for c in chunks]

    for c in chunks:
        q_ref[c, 0] = (pre[c]["r2"] + x[c][:L2, :L2]).astype(BF16)
        y02 = x[c][:L2, L2:] + mv[c][L2:]
        y0_ref[pl.ds(c * L, L), :] = y02[:L] + y02[L:]
        g_ref[c, 0] = (jnp.where(eye, jnp.broadcast_to(pre[c]["g_last"], (L2, L2)), 0.0) + x[c][L2:, :L2]).astype(BF16)
        c_ref[c, 0] = (x[c][L2:, L2:] + kgv[c]).astype(BF16)


def _dot_exact_rhs_lhs(tri_bf16, x):
    h, m, l = _split3(x)
    return _dot(tri_bf16, h) + (_dot(tri_bf16, m) + _dot(tri_bf16, l))


def _wkv_a(r, k, v, an, b, lw, n_chunks_step):
    bsz, s, d = r.shape
    L = WKV_CHUNK
    L2 = 2 * L
    pairs = d // LANES
    nck = s // L
    rows = n_chunks_step * L
    tok = pl.BlockSpec((None, rows, LANES), lambda bi, ci, pi: (bi, ci, pi))
    mat = pl.BlockSpec((n_chunks_step, 1, L2, L2), lambda bi, ci, pi: (ci, bi * pairs + pi, 0, 0))
    mshape = jax.ShapeDtypeStruct((nck, bsz * pairs, L2, L2), BF16)
    return pl.pallas_call(
        functools.partial(_wkv_a_kernel, n_chunks=n_chunks_step),
        out_shape=(mshape, jax.ShapeDtypeStruct((bsz, s, d), F32), mshape, mshape),
        grid=(bsz, nck // n_chunks_step, pairs),
        in_specs=[tok] * 6,
        out_specs=(mat, tok, mat, mat),
        compiler_params=_params(("parallel", "parallel", "parallel")),
        name="wkv_a",
    )(r, k, v, an, b, lw)


def _wkv_b_kernel(q_ref, y0_ref, g_ref, c_ref, y_ref, s_ref, *, n_chunks, bsz, pairs):
    L = WKV_CHUNK

    @pl.when(pl.program_id(0) == 0)
    def _():
        s_ref[...] = jnp.zeros_like(s_ref)

    for c in range(n_chunks):
        for bi in range(bsz):
            for pi in range(pairs):
                n = bi * pairs + pi
                sb = s_ref[n].astype(BF16)
                y2 = _dot(q_ref[c, n], sb)
                y_ref[bi, pl.ds(c * L, L), pl.ds(pi * LANES, LANES)] = (
                    y2[:L] + y2[L:] + y0_ref[bi, pl.ds(c * L, L), pl.ds(pi * LANES, LANES)])
                s_ref[n] = _dot(g_ref[c, n], sb) + c_ref[c, n].astype(F32)


def _wkv_b(q, y0, g, cmat, n_chunks_step):
    nck, bp, L2, _ = q.shape
    bsz, s, d = y0.shape
    pairs = d // LANES
    L = WKV_CHUNK
    rows = n_chunks_step * L
    mat = pl.BlockSpec((n_chunks_step, bp, L2, L2), lambda ci: (ci, 0, 0, 0))
    tok = pl.BlockSpec((bsz, rows, d), lambda ci: (0, ci, 0))
    return pl.pallas_call(
        functools.partial(_wkv_b_kernel, n_chunks=n_chunks_step, bsz=bsz, pairs=pairs),
        out_shape=jax.ShapeDtypeStruct((bsz, s, d), F32),
        grid=(nck // n_chunks_step,),
        in_specs=[mat, tok, mat, mat],
        out_specs=tok,
        scratch_shapes=[pltpu.VMEM((bp, L2, L2), F32)],
        compiler_params=_params(("arbitrary",)),
        name="wkv_b",
    )(q, y0, g, cmat)


def _router_combine(h, rw_ref, rb_ref):
    rw = rw_ref[...]
    hh, hl = _split2(h)
    wh, wl = _split2(rw)
    logits = _dot_nt(wh, hh) + (_dot_nt(wh, hl) + _dot_nt(wl, hh))
    aff = jax.nn.sigmoid(logits)
    sel = aff + rb_ref[...]
    affr = [aff[e:e + 1, :] for e in range(N_EXPERTS)]
    selr = [sel[e:e + 1, :] for e in range(N_EXPERTS)]
    top = []
    score = []
    for g in range(N_GROUPS):
        es = range(g * EXPERTS_PER_GROUP, (g + 1) * EXPERTS_PER_GROUP)
        sc = None
        for e in es:
            rank = None
            for o in es:
                if o == e:
                    continue
                beats = ((selr[o] >= selr[e]) if o < e else (selr[o] > selr[e])).astype(F32)
                rank = beats if rank is None else rank + beats
            t = rank < float(TOP_K)
            top.append(t)
            contrib = jnp.where(t, selr[e], 0.0)
            sc = contrib if sc is None else sc + contrib
        score.append(sc)
    best = score[0]
    for g in range(1, N_GROUPS):
        best = jnp.maximum(best, score[g])
    taken = None
    rows = []
    groups = []
    for g in range(N_GROUPS):
        is_best = score[g] == best
        if taken is not None:
            is_best = is_best & jnp.logical_not(taken)
        taken = is_best if taken is None else (taken | is_best)
        groups.append(is_best.astype(F32))
        for e in range(g * EXPERTS_PER_GROUP, (g + 1) * EXPERTS_PER_GROUP):
            rows.append(jnp.where(is_best & top[e], affr[e], 0.0))
    comb = jnp.concatenate(rows, axis=0)
    comb = comb / jnp.sum(comb, axis=0, keepdims=True)
    tm = comb.shape[1]
    grp = jnp.concatenate(groups, axis=0)
    pad = jnp.zeros((LANES - 3 * N_EXPERTS - N_GROUPS, tm), F32)
    route = jnp.concatenate(list(_split3(comb)) + [grp, pad], axis=0).T.astype(BF16)
    grow = jnp.concatenate([grp, jnp.zeros((8 - N_GROUPS, tm), F32)], axis=0)
    return route, grow


def _moe_prologue(x_new, mod2_ref, g2_ref, rw_ref, rb_ref, h_ref, route_ref, grow_ref):
    d = x_new.shape[-1]
    h = _rms(x_new) * g2_ref[...] * (1.0 + mod2_ref[:, d:2 * d]) + mod2_ref[:, 0:d]
    h_ref[...] = h.astype(BF16)
    route_ref[...], grow_ref[...] = _router_combine(h, rw_ref, rb_ref)


def _rwkv_post_kernel(x_ref, y_ref, bonus_ref, gate_ref, mod_ref, mod2_ref, lnw_ref, lnb_ref, wo_ref, bd_ref,
                      g2_ref, rw_ref, rb_ref, xo_ref, h_ref, route_ref, grow_ref):
    d = x_ref.shape[-1]
    bd = bd_ref[...]
    y = y_ref[...]
    inv_n = 1.0 / RWKV_HEAD_DIM
    yc = y - _seg_bcast_sum(y, bd, two_terms=True) * inv_n
    var = _seg_bcast_sum(yc * yc, bd) * inv_n
    yn = yc * lax.rsqrt(var + GN_EPS) * lnw_ref[...] + lnb_ref[...]
    o = (yn + bonus_ref[...].astype(F32)) * gate_ref[...].astype(F32)
    mixed = _dot(o.astype(BF16), wo_ref[...])
    x_new = x_ref[...] + mod_ref[:, 2 * d:3 * d] * mixed
    xo_ref[...] = x_new
    _moe_prologue(x_new, mod2_ref, g2_ref, rw_ref, rb_ref, h_ref, route_ref, grow_ref)


def _mla_post_kernel(x_ref, o_ref, mod_ref, mod2_ref, wo_ref, g2_ref, rw_ref, rb_ref,
                     xo_ref, h_ref, route_ref, grow_ref):
    d = x_ref.shape[-1]
    mixed = _dot(o_ref[...], wo_ref[...])
    x_new = x_ref[...] + mod_ref[:, 2 * d:3 * d] * mixed
    xo_ref[...] = x_new
    _moe_prologue(x_new, mod2_ref, g2_ref, rw_ref, rb_ref, h_ref, route_ref, grow_ref)


def _post_call(kern, name, tok_inputs, mods, consts, seq, tm):
    t, d = tok_inputs[0].shape
    tps = seq // tm
    full = lambda a: pl.BlockSpec(a.shape, lambda i: (0,) * a.ndim)
    tok = pl.BlockSpec((tm, d), lambda i: (i, 0))
    modspec = lambda m: pl.BlockSpec((None, 1, m.shape[-1]), lambda i: (i // tps, 0, 0))
    return pl.pallas_call(
        kern,
        out_shape=(jax.ShapeDtypeStruct((t, d), F32), jax.ShapeDtypeStruct((t, d), BF16),
                   jax.ShapeDtypeStruct((t, LANES), BF16), jax.ShapeDtypeStruct((8, t), F32)),
        grid=(t // tm,),
        in_specs=[tok] * len(tok_inputs) + [modspec(m) for m in mods] + [full(a) for a in consts],
        out_specs=(tok, tok, pl.BlockSpec((tm, LANES), lambda i: (i, 0)), pl.BlockSpec((8, tm), lambda i: (0, i))),
        compiler_params=_params(("parallel",)),
        name=name,
    )(*tok_inputs, *mods, *consts)


def _group_count_kernel(grow_ref, o_ref):
    o_ref[0] = jnp.broadcast_to(jnp.sum(grow_ref[...], axis=1, keepdims=True), o_ref.shape[1:])


def _group_counts(grow, tm):
    n_tiles = grow.shape[1] // tm
    sums = pl.pallas_call(
        _group_count_kernel,
        out_shape=jax.ShapeDtypeStruct((n_tiles, 8, LANES), F32),
        grid=(n_tiles,),
        in_specs=[pl.BlockSpec((8, tm), lambda i: (0, i))],
        out_specs=pl.BlockSpec((1, 8, LANES), lambda i: (i, 0, 0)),
        compiler_params=_params(("parallel",)),
        name="moe_counts",
    )(grow)
    return sums[:, :N_GROUPS, 0].astype(jnp.int32).reshape(-1)


def _moe_kernel(cnt_ref, x_ref, h_ref, route_ref, grow_ref, mod_ref, wg_ref, wu_ref, wd_ref, ex_ref, tri_ref, fg_ref,
                o_ref, acc_ref, posc_ref, posr_ref, *, final):
    d = x_ref.shape[-1]
    tm = x_ref.shape[0]
    i = pl.program_id(0)
    g = pl.program_id(1)
    grp_lane0 = 3 * N_EXPERTS

    @pl.when(g == 0)
    def _():
        acc_ref[...] = jnp.zeros_like(acc_ref)
        route = route_ref[...]
        posc_ref[...] = jnp.where(route > 0, _dot(tri_ref[...], route), -1.0)
        grow = grow_ref[...]
        grow16 = jnp.concatenate([grow, jnp.zeros_like(grow)], axis=0).astype(BF16)
        posr_ref[...] = jnp.where(grow > 0, _dot_nt(grow16, tri_ref[...])[:8], -1.0)

    n = cnt_ref[i * N_GROUPS + g]
    h = h_ref[...]
    route = route_ref[...]
    lane = lax.broadcasted_iota(jnp.int32, (tm, LANES), 1)
    pos_col = jnp.sum(jnp.where(lane == grp_lane0 + g, posc_ref[...], 0.0), axis=1, keepdims=True)
    pos_row = posr_ref[pl.ds(g, 1), :]

    def run_rows(r0, rows):
        rank_r = (r0 + lax.broadcasted_iota(jnp.int32, (rows, tm), 0)).astype(F32)
        sel = jnp.where(pos_row == rank_r, 1.0, 0.0).astype(BF16)
        hs = _dot(sel, h).astype(BF16)
        cb = _dot(sel, route).astype(BF16)
        gt = _dot(hs, wg_ref[...])
        up = _dot(hs, wu_ref[...])
        cs = _dot(cb, ex_ref[...])
        act = (gt * jax.nn.sigmoid(gt)) * up * cs
        ys = _dot(act.astype(BF16), wd_ref[...]).astype(BF16)
        rank_c = (r0 + lax.broadcasted_iota(jnp.int32, (tm, rows), 1)).astype(F32)
        sel_t = jnp.where(pos_col == rank_c, 1.0, 0.0).astype(BF16)
        acc_ref[...] += _dot(sel_t, ys)

    @pl.when(n > 0)
    def _():
        run_rows(0, MOE_ROWS)

    def extra(b, carry):
        run_rows(MOE_ROWS + b * MOE_EXTRA_ROWS, MOE_EXTRA_ROWS)
        return carry

    lax.fori_loop(0, (jnp.maximum(n - MOE_ROWS, 0) + MOE_EXTRA_ROWS - 1) // MOE_EXTRA_ROWS, extra, 0)

    @pl.when(g == pl.num_programs(1) - 1)
    def _():
        x_new = x_ref[...] + mod_ref[:, 2 * d:3 * d] * acc_ref[...]
        o_ref[...] = _rms(x_new) * fg_ref[...] if final else x_new


def _moe(x2d, h, route, grow, mod, wg, wu, wd, expand, tri, final_g, final, seq, tm):
    t, d = x2d.shape
    fg = wg.shape[1] // N_GROUPS
    tps = seq // tm
    counts = _group_counts(grow, tm)
    tok = lambda w: pl.BlockSpec((tm, w), lambda i, g, c: (i, 0))
    return pl.pallas_call(
        functools.partial(_moe_kernel, final=final),
        out_shape=jax.ShapeDtypeStruct((t, d), F32),
        grid_spec=pltpu.PrefetchScalarGridSpec(
            num_scalar_prefetch=1,
            grid=(t // tm, N_GROUPS),
            in_specs=[tok(d), tok(d), tok(LANES),
                      pl.BlockSpec((8, tm), lambda i, g, c: (0, i)),
                      pl.BlockSpec((None, 1, mod.shape[-1]), lambda i, g, c: (i // tps, 0, 0)),
                      pl.BlockSpec((d, fg), lambda i, g, c: (0, g)),
                      pl.BlockSpec((d, fg), lambda i, g, c: (0, g)),
                      pl.BlockSpec((fg, d), lambda i, g, c: (g, 0)),
                      pl.BlockSpec((LANES, fg), lambda i, g, c: (0, g)),
                      pl.BlockSpec((tm, tm), lambda i, g, c: (0, 0)),
                      pl.BlockSpec((1, d), lambda i, g, c: (0, 0))],
            out_specs=tok(d),
            scratch_shapes=[pltpu.VMEM((tm, d), F32), pltpu.VMEM((tm, LANES), F32), pltpu.VMEM((8, tm), F32)]),
        compiler_params=_params(("parallel", "arbitrary")),
        name="moe",
    )(counts, x2d, h, route, grow, mod, wg, wu, wd, expand, tri, final_g)


def _mla_pre_kernel(x_ref, pos_ref, modkv_ref, modq_ref, gkv_ref, gq_ref, wdkv_ref, gckv_ref, wuk_ref, wuvt_ref,
                    wka_ref, wkb_ref, wdq_ref, gcq_ref, wq_ref, freq_ref,
                    q_ref, k_ref, vt_ref):
    d = x_ref.shape[-1]
    tm = x_ref.shape[0]
    xn = _rms(x_ref[...])
    ang = pos_ref[...].astype(F32) * freq_ref[...]
    cos = jnp.cos(ang)
    sin = jnp.sin(ang)
    lane = lax.broadcasted_iota(jnp.int32, (tm, LANES), 1)
    trig_q = jnp.where(lane < QK_NOPE + QK_ROPE, cos, sin)

    hkv = (xn * gkv_ref[...] * (1.0 + modkv_ref[:, d:2 * d]) + modkv_ref[:, 0:d]).astype(BF16)
    ckv = (_rms(_dot(hkv, wdkv_ref[...])) * gckv_ref[...]).astype(BF16)
    vt = _dot_nt(wuvt_ref[...], ckv)
    row = lax.broadcasted_iota(jnp.int32, vt.shape, 0)
    vt_ref[...] = jnp.where((row & (LANES - 1)) == V_HEAD, 1.0, vt).astype(BF16)
    kr = _dot(hkv, wka_ref[...]) * cos + _dot(hkv, wkb_ref[...]) * sin
    kn = _dot(ckv, wuk_ref[...])

    hq = (xn * gq_ref[...] * (1.0 + modq_ref[:, d:2 * d]) + modq_ref[:, 0:d]).astype(BF16)
    cq = (_rms(_dot(hq, wdq_ref[...])) * gcq_ref[...]).astype(BF16)
    qa = _dot(cq, wq_ref[...])
    for hh in range(MLA_HEADS):
        sl = slice(hh * LANES, (hh + 1) * LANES)
        k_ref[:, sl] = (kn[:, sl] + kr).astype(BF16)
        q_ref[:, sl] = (qa[:, sl] * trig_q).astype(BF16)


def _mla_pre(x2d, pos2d, modkv, modq, consts, seq, tm):
    t, d = x2d.shape
    tps = seq // tm
    full = lambda a: pl.BlockSpec(a.shape, lambda i: (0,) * a.ndim)
    modspec = lambda m: pl.BlockSpec((None, 1, m.shape[-1]), lambda i: (i // tps, 0, 0))
    hq = MLA_HEADS * LANES
    return pl.pallas_call(
        _mla_pre_kernel,
        out_shape=(jax.ShapeDtypeStruct((t, hq), BF16), jax.ShapeDtypeStruct((t, hq), BF16),
                   jax.ShapeDtypeStruct((t // seq, tps, hq, tm), BF16)),
        grid=(t // tm,),
        in_specs=[pl.BlockSpec((tm, d), lambda i: (i, 0)), pl.BlockSpec((tm, 1), lambda i: (i, 0)),
                  modspec(modkv), modspec(modq)] + [full(a) for a in consts],
        out_specs=(pl.BlockSpec((tm, hq), lambda i: (i, 0)), pl.BlockSpec((tm, hq), lambda i: (i, 0)),
                   pl.BlockSpec((None, None, hq, tm), lambda i: (i // tps, i % tps, 0, 0))),
        compiler_params=_params(("parallel",)),
        name="mla_pre",
    )(x2d, pos2d, modkv, modq, *consts)


def _attn_kernel(q_ref, k_ref, vt_ref, o_ref, m_ref, acc_ref, *, blk):
    qi = pl.program_id(2)
    m_ref[...] = jnp.full_like(m_ref, -jnp.inf)
    acc_ref[...] = jnp.zeros_like(acc_ref)

    def block(j0, n_blk, diagonal=False):
        k0 = pl.multiple_of(j0 * blk, blk)
        width = n_blk * blk
        heads = range(ATTN_HEADS)
        sls = [slice(hh * LANES, (hh + 1) * LANES) for hh in heads]
        s = [_dot_nt(k_ref[pl.ds(k0, width), sl], q_ref[:, sl]) for sl in sls]
        if diagonal:
            key = lax.broadcasted_iota(jnp.int32, (width, blk), 0)
            qry = lax.broadcasted_iota(jnp.int32, (width, blk), 1) + (width - blk)
            s = [jnp.where(key <= qry, sh, -jnp.inf) for sh in s]
        for hh in heads:
            m_old = m_ref[hh]
            m_new = jnp.maximum(m_old, jnp.max(s[hh], axis=0, keepdims=True))
            p = jnp.exp2(s[hh] - m_new).astype(BF16)
            pv = _dot(vt_ref[j0, sls[hh], :], p[:blk])
            for i in range(1, n_blk):
                pv = pv + _dot(vt_ref[j0 + i, sls[hh], :], p[i * blk:(i + 1) * blk])
            acc_ref[hh] = acc_ref[hh] * jnp.exp2(m_old - m_new) + pv
            m_ref[hh] = m_new

    n_pairs = qi // 2

    def body(j, carry):
        block(4 * j, 4)
        return carry

    lax.fori_loop(0, n_pairs // 2, body, 0)

    @pl.when(n_pairs % 2 == 1)
    def _():
        block(2 * (n_pairs - 1), 2)

    @pl.when(qi % 2 == 0)
    def _():
        block(qi, 1, diagonal=True)

    @pl.when(qi % 2 == 1)
    def _():
        block(qi - 1, 2, diagonal=True)

    outs = []
    for hh in range(ATTN_HEADS):
        acc = acc_ref[hh].T
        outs.append((acc / acc[:, V_HEAD:V_HEAD + 1])[:, :V_HEAD])
    o_ref[...] = jnp.concatenate(outs, axis=1).astype(BF16)


def _attn(q, k, vt, blk):
    bsz, s, _ = q.shape
    hp = MLA_HEADS // ATTN_HEADS
    nh = ATTN_HEADS
    return pl.pallas_call(
        functools.partial(_attn_kernel, blk=blk),
        out_shape=jax.ShapeDtypeStruct((bsz, s, MLA_HEADS * V_HEAD), BF16),
        grid=(bsz, hp, s // blk),
        in_specs=[pl.BlockSpec((None, blk, nh * LANES), lambda b, h, qi: (b, qi, h)),
                  pl.BlockSpec((None, s, nh * LANES), lambda b, h, qi: (b, 0, h), pipeline_mode=pl.Buffered(1)),
                  pl.BlockSpec((None, s // blk, nh * LANES, blk), lambda b, h, qi: (b, 0, h, 0),
                               pipeline_mode=pl.Buffered(1))],
        out_specs=pl.BlockSpec((None, blk, nh * V_HEAD), lambda b, h, qi: (b, qi, h)),
        scratch_shapes=[pltpu.VMEM((nh, 1, blk), F32), pltpu.VMEM((nh, LANES, blk), F32)],
        compiler_params=_params(("parallel", "parallel", "arbitrary")),
        name="attn",
    )(q, k, vt)


def kernel(x, c, positions, ada_w, ada_b, norm_g, rwkv_mu, rwkv_w_rkv, rwkv_w0, rwkv_w1, rwkv_w2, rwkv_a0, rwkv_a1, rwkv_a2, rwkv_g1, rwkv_g2, rwkv_k_k, rwkv_k_a, rwkv_r_k, rwkv_lnx_w, rwkv_lnx_b, rwkv_w_o, kv_ada_w, kv_ada_b, kv_norm_g, mla_w_dkv, mla_g_kv, mla_w_uk, mla_w_uv, mla_w_kr, mla_w_dq, mla_g_q, mla_w_uq, mla_w_qr, mla_w_o, router_w, router_b, moe_w_gu, moe_w_down, final_g):
    bsz, seq, d = x.shape
    depth = ada_w.shape[0]
    n_a = rwkv_mu.shape[0]
    t = bsz * seq
    tm = min(512, seq)
    row = lambda a: a.reshape(1, -1).astype(F32)

    c_pad = jnp.pad(c, ((0, 8 - bsz), (0, 0)))
    mods = _ada(c_pad, ada_w.reshape(depth * 2, d, 3 * d), ada_b.reshape(depth * 2, 1, 3 * d))
    mods = mods[:, :bsz].reshape(depth, 2, bsz, 1, 3 * d)
    mod_kv = _ada(c_pad, kv_ada_w[None], kv_ada_b.reshape(1, 1, 2 * d))[0, :bsz].reshape(bsz, 1, 2 * d)

    idx = jnp.arange(MXU_DIM) // RWKV_HEAD_DIM
    bd = (idx[:, None] == idx[None, :]).astype(BF16)
    rw_t = router_w.T.astype(F32)
    rb_col = router_b.reshape(N_EXPERTS, 1).astype(F32)
    d_exp = moe_w_down.shape[2]
    f_all = N_EXPERTS * d_exp
    exp_id = jnp.arange(f_all) // d_exp
    expand = (jnp.arange(N_EXPERTS)[:, None] == exp_id[None, :]).astype(BF16)
    expand = jnp.concatenate([expand] * 3 + [jnp.zeros((LANES - 3 * N_EXPERTS, f_all), BF16)], axis=0)
    tm_moe = min(1024, seq)
    tri = (jnp.arange(tm_moe)[:, None] > jnp.arange(tm_moe)[None, :]).astype(BF16)

    def moe_weights(layer):
        wgu = moe_w_gu[layer]
        wg = wgu[:, :, :d_exp].transpose(1, 0, 2).reshape(d, f_all).astype(BF16)
        wu = wgu[:, :, d_exp:].transpose(1, 0, 2).reshape(d, f_all).astype(BF16)
        wd = moe_w_down[layer].reshape(f_all, d).astype(BF16)
        return wg, wu, wd

    inv_freq = ROPE_THETA ** (-jnp.arange(0, QK_ROPE, 2, dtype=F32) / QK_ROPE)
    freq = jnp.concatenate([jnp.zeros((QK_NOPE,), F32)] + [inv_freq] * 4).reshape(1, LANES)
    pos2d = positions.reshape(t, 1)

    x2d = x.reshape(t, d)
    assert depth - n_a == 1
    for layer in range(depth):
        mod_mix = mods[layer, 0]
        mod_ffn = mods[layer, 1]
        if layer < n_a:
            a = layer
            mu8 = jnp.pad(rwkv_mu[a], ((0, 2), (0, 0)))
            vecs = jnp.stack([rwkv_w0[a], rwkv_a0[a], rwkv_k_k[a], rwkv_k_a[a], rwkv_r_k[a].reshape(d),
                              jnp.zeros((d,), F32), jnp.zeros((d,), F32), jnp.zeros((d,), F32)])
            r, k, v, an, b, lw, gate, bonus = _rwkv_pre(
                x2d, mod_mix, row(norm_g[layer, 0]), mu8, rwkv_w_rkv[a].astype(BF16),
                rwkv_w1[a].astype(BF16), rwkv_w2[a].astype(BF16), rwkv_a1[a].astype(BF16), rwkv_a2[a].astype(BF16),
                rwkv_g1[a].astype(BF16), rwkv_g2[a].astype(BF16), vecs, bd, seq, tm)
            sh = lambda z: z.reshape(bsz, seq, d)
            ncs = min(8, seq // WKV_CHUNK)
            qm, y0, gm, cm = _wkv_a(sh(r), sh(k), sh(v), sh(an), sh(b), sh(lw), ncs)
            y = _wkv_b(qm, y0, gm, cm, min(2, seq // WKV_CHUNK)).reshape(t, d)
            x2d, h, route, grow = _post_call(
                _rwkv_post_kernel, "rwkv_post", [x2d, y, bonus, gate], [mod_mix, mod_ffn],
                [row(rwkv_lnx_w[a]), row(rwkv_lnx_b[a]), rwkv_w_o[a].astype(BF16), bd,
                 row(norm_g[layer, 1]), rw_t, rb_col], seq, tm)
        else:
            bl = layer - n_a
            scale = (QK_NOPE + QK_ROPE) ** -0.5 * LOG2_E
            hd = MLA_HEADS
            zpad = lambda w, lo, hi: jnp.pad(w, ((0, 0), (0, 0), (lo, hi)))
            kvl = mla_w_uk.shape[0]
            wuk = zpad(mla_w_uk, 0, LANES - QK_NOPE).reshape(kvl, hd * LANES).astype(BF16)
            wuvt = zpad(mla_w_uv, 0, LANES - V_HEAD).reshape(kvl, hd * LANES).T.astype(BF16)
            k1, k2 = jnp.split(mla_w_kr, 2, axis=-1)
            nope_pad = lambda w: jnp.pad(w, ((0, 0), (QK_NOPE, 0))).astype(BF16)
            wka = nope_pad(jnp.concatenate([k1, k2, k2, -k1], axis=-1))
            wkb = nope_pad(jnp.concatenate([-k2, k1, k1, k2], axis=-1))
            ql = mla_w_uq.shape[1]
            wq = jnp.concatenate([mla_w_uq[bl], mla_w_qr[bl], mla_w_qr[bl]], axis=-1) * scale
            wq = wq.reshape(ql, hd * LANES).astype(BF16)
            consts = [row(kv_norm_g), row(norm_g[layer, 0]), mla_w_dkv.astype(BF16), row(mla_g_kv), wuk, wuvt,
                      wka, wkb, mla_w_dq[bl].astype(BF16), row(mla_g_q[bl]), wq, freq]
            ta = min(ATTN_BLOCK, seq)
            q, kf, vt = _mla_pre(x2d, pos2d, mod_kv, mod_mix, consts, seq, ta)
            o = _attn(q.reshape(bsz, seq, -1), kf.reshape(bsz, seq, -1), vt, ta)
            x2d, h, route, grow = _post_call(
                _mla_post_kernel, "mla_post", [x2d, o.reshape(t, d)], [mod_mix, mod_ffn],
                [mla_w_o[bl].astype(BF16), row(norm_g[layer, 1]), rw_t, rb_col], seq, tm)
        wg, wu, wd = moe_weights(layer)
        x2d = _moe(x2d, h, route, grow, mod_ffn, wg, wu, wd, expand, tri, row(final_g), layer == depth - 1, seq,
                   tm_moe)
    return x2d.reshape(bsz, seq, d)
```

```python
import functools

import jax
import jax.numpy as jnp
from jax import lax
from jax.experimental import pallas as pl
from jax.experimental.pallas import tpu as pltpu

F32 = jnp.float32
BF16 = jnp.bfloat16

NORM_EPS = 1e-6
GN_EPS = 64e-5
ROPE_THETA = 10000.0
LOG2_E = 1.4426950408889634
RWKV_HEAD_DIM = 64
MLA_HEADS = 16
QK_NOPE = 64
QK_ROPE = 32
V_HEAD = 64
N_EXPERTS = 16
N_GROUPS = 4
EXPERTS_PER_GROUP = N_EXPERTS // N_GROUPS
TOP_K = 2

LANES = 128
MXU_DIM = 256
WKV_CHUNK = 64
ATTN_BLOCK = 512
ATTN_HEADS = 4
MOE_ROWS = 256
MOE_ROWS_WIDE = 320
MOE_EXTRA_ROWS = 64
VMEM_LIMIT = 56 * 1024 * 1024


def _dot(a, b):
    return jnp.dot(a, b, preferred_element_type=F32)


def _dot_nt(a, b):
    return lax.dot_general(a, b, (((1,), (1,)), ((), ())), preferred_element_type=F32)


def _split2(x):
    hi = x.astype(BF16)
    lo = (x - hi.astype(F32)).astype(BF16)
    return hi, lo


def _split3(x):
    hi = x.astype(BF16)
    r1 = x - hi.astype(F32)
    mid = r1.astype(BF16)
    lo = (r1 - mid.astype(F32)).astype(BF16)
    return hi, mid, lo


def _dot_x3(a, b):
    ah, al = _split2(a)
    bh, bl = _split2(b)
    return _dot(ah, bh) + (_dot(ah, bl) + _dot(al, bh))


def _rms(x):
    return x * lax.rsqrt(jnp.mean(x * x, axis=-1, keepdims=True) + NORM_EPS)


def _seg_bcast_sum(x, bd, two_terms=False):
    d = x.shape[-1]
    terms = _split2(x) if two_terms else (x.astype(BF16),)
    outs = []
    for j in range(d // MXU_DIM):
        sl = slice(j * MXU_DIM, (j + 1) * MXU_DIM)
        acc = _dot(terms[0][:, sl], bd)
        for t in terms[1:]:
            acc = acc + _dot(t[:, sl], bd)
        outs.append(acc)
    return jnp.concatenate(outs, axis=-1)


def _params(sem):
    return pltpu.CompilerParams(dimension_semantics=sem, vmem_limit_bytes=VMEM_LIMIT)


def _ada_kernel(c_ref, w_ref, b_ref, o_ref):
    c = c_ref[...]
    s = c * jax.nn.sigmoid(c)
    o_ref[0] = _dot_x3(s, w_ref[0]) + b_ref[0]


def _ada(c_pad, w, b, tn=1024):
    n, d, nn = w.shape
    return pl.pallas_call(
        _ada_kernel,
        out_shape=jax.ShapeDtypeStruct((n, 8, nn), F32),
        grid=(n, nn // tn),
        in_specs=[pl.BlockSpec((8, d), lambda i, j: (0, 0)),
                  pl.BlockSpec((1, d, tn), lambda i, j: (i, 0, j)),
                  pl.BlockSpec((1, 1, tn), lambda i, j: (i, 0, j))],
        out_specs=pl.BlockSpec((1, 8, tn), lambda i, j: (i, 0, j)),
        compiler_params=_params(("parallel", "parallel")),
        name="ada",
    )(c_pad, w, b)


def _rwkv_pre_kernel(x_ref, xp_ref, mod_ref, g_ref, mu_ref, wrkv_ref, w1_ref, w2_ref, a1_ref, a2_ref,
                     g1_ref, g2_ref, vec_ref, bd_ref,
                     r_ref, k_ref, v_ref, an_ref, b_ref, lw_ref, gate_ref, bonus_ref, *, tiles_per_seq):
    d = x_ref.shape[-1]
    tm = x_ref.shape[0]
    i = pl.program_id(0)
    shift = mod_ref[:, 0:d]
    scale = mod_ref[:, d:2 * d]
    gn = g_ref[...]

    def modulate(xv):
        return _rms(xv) * gn * (1.0 + scale) + shift

    h = modulate(x_ref[...])
    hp_last = modulate(xp_ref[...])[7:8, :]
    hp_last = jnp.where(i % tiles_per_seq == 0, 0.0, hp_last)
    row = lax.broadcasted_iota(jnp.int32, (tm, d), 0)
    h_prev = jnp.where(row == 0, hp_last, pltpu.roll(h, 1, axis=0))
    xx = h_prev - h

    def mix(j):
        return (h + xx * mu_ref[j:j + 1, :]).astype(BF16)

    w0 = vec_ref[0:1, :]
    a0 = vec_ref[1:2, :]
    k_k = vec_ref[2:3, :]
    k_a = vec_ref[3:4, :]
    r_k = vec_ref[4:5, :]
    bd = bd_ref[...]

    r = _dot(mix(0), wrkv_ref[0])
    k = _dot(mix(2), wrkv_ref[1])
    v = _dot(mix(3), wrkv_ref[2])
    z = w0 + _dot(jnp.tanh(_dot(mix(1), w1_ref[...])).astype(BF16), w2_ref[...])
    w_log = -(jnp.maximum(-z, 0.0) + jnp.log(1.0 + jnp.exp(-jnp.abs(z)))) - 0.5
    lw_ref[...] = -jnp.exp(w_log)
    a = jax.nn.sigmoid(a0 + _dot(_dot(mix(4), a1_ref[...]).astype(BF16), a2_ref[...]))
    gate_ref[...] = _dot(jax.nn.sigmoid(_dot(mix(5), g1_ref[...])).astype(BF16), g2_ref[...]).astype(BF16)

    kk = k * k_k
    kk = kk / jnp.maximum(jnp.sqrt(_seg_bcast_sum(kk * kk, bd)), 1e-12)
    km = k * (1.0 + (a - 1.0) * k_a)
    bonus_ref[...] = (_seg_bcast_sum(r * km * r_k, bd) * v).astype(BF16)
    r_ref[...] = r.astype(BF16)
    k_ref[...] = km.astype(BF16)
    v_ref[...] = v.astype(BF16)
    an_ref[...] = (-kk).astype(BF16)
    b_ref[...] = (kk * a).astype(BF16)


def _rwkv_pre(x2d, mod, norm_g, mu8, wrkv, w1, w2, a1, a2, g1, g2, vecs, bd, seq, tm):
    t, d = x2d.shape
    tps = seq // tm
    full = lambda a: pl.BlockSpec(a.shape, lambda i: (0,) * a.ndim)
    tok = pl.BlockSpec((tm, d), lambda i: (i, 0))
    out_bf = jax.ShapeDtypeStruct((t, d), BF16)
    return pl.pallas_call(
        functools.partial(_rwkv_pre_kernel, tiles_per_seq=tps),
        out_shape=(out_bf, out_bf, out_bf, out_bf, out_bf, jax.ShapeDtypeStruct((t, d), F32), out_bf, out_bf),
        grid=(t // tm,),
        in_specs=[tok,
                  pl.BlockSpec((8, d), lambda i: (jnp.maximum(i * (tm // 8) - 1, 0), 0)),
                  pl.BlockSpec((None, 1, mod.shape[-1]), lambda i: (i // tps, 0, 0)),
                  full(norm_g), full(mu8), full(wrkv), full(w1), full(w2), full(a1), full(a2),
                  full(g1), full(g2), full(vecs), full(bd)],
        out_specs=(tok,) * 8,
        compiler_params=_params(("parallel",)),
        name="rwkv_pre",
    )(x2d, x2d, mod, norm_g, mu8, wrkv, w1, w2, a1, a2, g1, g2, vecs, bd)


def _wkv_a_kernel(r_ref, k_ref, v_ref, an_ref, b_ref, lw_ref, q_ref, y0_ref, g_ref, c_ref, *, n_chunks):
    L = WKV_CHUNK
    L2 = 2 * L
    lane = lax.broadcasted_iota(jnp.int32, (L, LANES), 1)
    head0 = lane < RWKV_HEAD_DIM
    ri = lax.broadcasted_iota(jnp.int32, (L2, L2), 0)
    ci = lax.broadcasted_iota(jnp.int32, (L2, L2), 1)
    strict = ci < ri
    incl = ci <= ri
    eye = ci == ri
    tri = (lax.broadcasted_iota(jnp.int32, (L, L), 1) <= lax.broadcasted_iota(jnp.int32, (L, L), 0)).astype(BF16)
    tri3 = jnp.concatenate([tri] * 3, axis=1)

    def stack(xv):
        return jnp.concatenate([jnp.where(head0, xv, 0.0), jnp.where(head0, 0.0, xv)], axis=0)

    chunks = range(n_chunks)
    pre = []
    for c in chunks:
        sl = pl.ds(c * L, L)
        lw = lw_ref[sl, :]
        cum = _dot(tri3, jnp.concatenate(_split3(lw), axis=0))
        cl = cum[L - 1:L, :]
        g_in = jnp.exp(cum)
        g_ex = jnp.exp(cum - lw)
        g_inv = jnp.exp(-cum)
        g_end = jnp.exp(cl - cum)
        kf = k_ref[sl, :].astype(F32)
        bf = b_ref[sl, :].astype(F32)
        pre.append(dict(
            a2=stack(an_ref[sl, :].astype(F32) * g_ex).astype(BF16),
            r2=stack(r_ref[sl, :].astype(F32) * g_in),
            b2=stack(bf * g_inv).astype(BF16),
            k2=stack(kf * g_inv).astype(BF16),
            v2=stack(v_ref[sl, :].astype(F32)).astype(BF16),
            bg2t=stack(bf * g_end).T.astype(BF16),
            kg2t=stack(kf * g_end).T.astype(BF16),
            g_last=jnp.exp(cl)))

    gram = [_dot_nt(jnp.concatenate([p["a2"], p["r2"].astype(BF16)], axis=0),
                    jnp.concatenate([p["b2"], p["k2"]], axis=0)) for p in pre]
    m_ab = [jnp.where(strict, g[:L2, :L2], 0.0) for g in gram]
    m_ak = [jnp.where(strict, g[:L2, L2:], 0.0) for g in gram]
    m_rb = [jnp.where(incl, g[L2:, :L2], 0.0) for g in gram]
    m_rk = [jnp.where(incl, g[L2:, L2:], 0.0) for g in gram]
    mv = [_dot(jnp.concatenate([m_ak[c], m_rk[c]], axis=0).astype(BF16), pre[c]["v2"]) for c in chunks]
    kgv = [_dot(p["kg2t"], p["v2"]) for p in pre]

    nb = [n.astype(BF16) for n in m_ab]
    nk = [_dot(n, n) for n in nb]
    tinv = [jnp.where(eye, 1.0, n) for n in m_ab]
    for step in range(1, 6):
        nb = [n.astype(BF16) for n in nk]
        if step < 5:
            both = [_dot(nb[c], jnp.concatenate([nb[c], tinv[c].astype(BF16)], axis=1)) for c in chunks]
            nk = [m[:, :L2] for m in both]
            tinv = [tinv[c] + both[c][:, L2:] for c in chunks]
        else:
            tinv = [tinv[c] + _dot(nb[c], tinv[c].astype(BF16)) for c in chunks]

    wu = [_dot(tinv[c].astype(BF16), jnp.concatenate([pre[c]["a2"], mv[c][:L2].astype(BF16)], axis=1)) for c in chunks]
    x = [_dot(jnp.concatenate([m_rb[c].astype(BF16), pre[c]["bg2t"]], axis=0), wu[c].astype(BF16)) for c in chunks]

    for c in chunks:
        q_ref[c, 0] = (pre[c]["r2"] + x[c][:L2, :L2]).astype(BF16)
        y02 = x[c][:L2, L2:] + mv[c][L2:]
        y0_ref[pl.ds(c * L, L), :] = y02[:L] + y02[L:]
        g_ref[c, 0] = (jnp.where(eye, jnp.broadcast_to(pre[c]["g_last"], (L2, L2)), 0.0) + x[c][L2:, :L2]).astype(BF16)
        c_ref[c, 0] = (x[c][L2:, L2:] + kgv[c]).astype(BF16)


def _wkv_a(r, k, v, an, b, lw, n_chunks_step):
    bsz, s, d = r.shape
    L = WKV_CHUNK
    L2 = 2 * L
    pairs = d // LANES
    nck = s // L
    rows = n_chunks_step * L
    tok = pl.BlockSpec((None, rows, LANES), lambda bi, ci, pi: (bi, ci, pi))
    mat = pl.BlockSpec((n_chunks_step, 1, L2, L2), lambda bi, ci, pi: (ci, bi * pairs + pi, 0, 0))
    mshape = jax.ShapeDtypeStruct((nck, bsz * pairs, L2, L2), BF16)
    return pl.pallas_call(
        functools.partial(_wkv_a_kernel, n_chunks=n_chunks_step),
        out_shape=(mshape, jax.ShapeDtypeStruct((bsz, s, d), F32), mshape, mshape),
        grid=(bsz, nck // n_chunks_step, pairs),
        in_specs=[tok] * 6,
        out_specs=(mat, tok, mat, mat),
        compiler_params=_params(("parallel", "parallel", "parallel")),
        name="wkv_a",
    )(r, k, v, an, b, lw)


def _wkv_b_kernel(q_ref, y0_ref, g_ref, c_ref, y_ref, s_ref, *, n_chunks, bsz, pairs):
    L = WKV_CHUNK

    @pl.when(pl.program_id(0) == 0)
    def _():
        s_ref[...] = jnp.zeros_like(s_ref)

    for c in range(n_chunks):
        for bi in range(bsz):
            for pi in range(pairs):
                n = bi * pairs + pi
                sb = s_ref[n].astype(BF16)
                y2 = _dot(q_ref[c, n], sb)
                y_ref[bi, pl.ds(c * L, L), pl.ds(pi * LANES, LANES)] = (
                    y2[:L] + y2[L:] + y0_ref[bi, pl.ds(c * L, L), pl.ds(pi * LANES, LANES)])
                s_ref[n] = _dot(g_ref[c, n], sb) + c_ref[c, n].astype(F32)


def _wkv_b(q, y0, g, cmat, n_chunks_step):
    nck, bp, L2, _ = q.shape
    bsz, s, d = y0.shape
    pairs = d // LANES
    L = WKV_CHUNK
    rows = n_chunks_step * L
    mat = pl.BlockSpec((n_chunks_step, bp, L2, L2), lambda ci: (ci, 0, 0, 0))
    tok = pl.BlockSpec((bsz, rows, d), lambda ci: (0, ci, 0))
    return pl.pallas_call(
        functools.partial(_wkv_b_kernel, n_chunks=n_chunks_step, bsz=bsz, pairs=pairs),
        out_shape=jax.ShapeDtypeStruct((bsz, s, d), F32),
        grid=(nck // n_chunks_step,),
        in_specs=[mat, tok, mat, mat],
        out_specs=tok,
        scratch_shapes=[pltpu.VMEM((bp, L2, L2), F32)],
        compiler_params=_params(("arbitrary",)),
        name="wkv_b",
    )(q, y0, g, cmat)


def _router_combine(h, rw_ref, rb_ref):
    rw = rw_ref[...]
    hh, hl = _split2(h)
    wh, wl = _split2(rw)
    logits = _dot_nt(wh, hh) + (_dot_nt(wh, hl) + _dot_nt(wl, hh))
    aff = jax.nn.sigmoid(logits)
    sel = aff + rb_ref[...]
    affr = [aff[e:e + 1, :] for e in range(N_EXPERTS)]
    selr = [sel[e:e + 1, :] for e in range(N_EXPERTS)]
    top = []
    score = []
    for g in range(N_GROUPS):
        es = range(g * EXPERTS_PER_GROUP, (g + 1) * EXPERTS_PER_GROUP)
        sc = None
        for e in es:
            rank = None
            for o in es:
                if o == e:
                    continue
                beats = ((selr[o] >= selr[e]) if o < e else (selr[o] > selr[e])).astype(F32)
                rank = beats if rank is None else rank + beats
            t = rank < float(TOP_K)
            top.append(t)
            contrib = jnp.where(t, selr[e], 0.0)
            sc = contrib if sc is None else sc + contrib
        score.append(sc)
    best = score[0]
    for g in range(1, N_GROUPS):
        best = jnp.maximum(best, score[g])
    taken = None
    rows = []
    groups = []
    for g in range(N_GROUPS):
        is_best = score[g] == best
        if taken is not None:
            is_best = is_best & jnp.logical_not(taken)
        taken = is_best if taken is None else (taken | is_best)
        groups.append(is_best.astype(F32))
        for e in range(g * EXPERTS_PER_GROUP, (g + 1) * EXPERTS_PER_GROUP):
            rows.append(jnp.where(is_best & top[e], affr[e], 0.0))
    comb = jnp.concatenate(rows, axis=0)
    comb = comb / jnp.sum(comb, axis=0, keepdims=True)
    tm = comb.shape[1]
    grp = jnp.concatenate(groups, axis=0)
    pad = jnp.zeros((LANES - 3 * N_EXPERTS - N_GROUPS, tm), F32)
    route = jnp.concatenate(list(_split3(comb)) + [grp, pad], axis=0).T.astype(BF16)
    grow = jnp.concatenate([grp, jnp.zeros((8 - N_GROUPS, tm), F32)], axis=0)
    return route, grow


def _moe_prologue(x_new, mod2_ref, g2_ref, rw_ref, rb_ref, h_ref, route_ref, grow_ref):
    d = x_new.shape[-1]
    h = _rms(x_new) * g2_ref[...] * (1.0 + mod2_ref[:, d:2 * d]) + mod2_ref[:, 0:d]
    h_ref[...] = h.astype(BF16)
    route_ref[...], grow_ref[...] = _router_combine(h, rw_ref, rb_ref)


def _rwkv_post_kernel(x_ref, y_ref, bonus_ref, gate_ref, mod_ref, mod2_ref, lnw_ref, lnb_ref, wo_ref, bd_ref,
                      g2_ref, rw_ref, rb_ref, xo_ref, h_ref, route_ref, grow_ref):
    d = x_ref.shape[-1]
    bd = bd_ref[...]
    y = y_ref[...]
    inv_n = 1.0 / RWKV_HEAD_DIM
    yc = y - _seg_bcast_sum(y, bd, two_terms=True) * inv_n
    var = _seg_bcast_sum(yc * yc, bd) * inv_n
    yn = yc * lax.rsqrt(var + GN_EPS) * lnw_ref[...] + lnb_ref[...]
    o = (yn + bonus_ref[...].astype(F32)) * gate_ref[...].astype(F32)
    mixed = _dot(o.astype(BF16), wo_ref[...])
    x_new = x_ref[...] + mod_ref[:, 2 * d:3 * d] * mixed
    xo_ref[...] = x_new
    _moe_prologue(x_new, mod2_ref, g2_ref, rw_ref, rb_ref, h_ref, route_ref, grow_ref)


def _mla_post_kernel(x_ref, o_ref, mod_ref, mod2_ref, wo_ref, g2_ref, rw_ref, rb_ref,
                     xo_ref, h_ref, route_ref, grow_ref):
    d = x_ref.shape[-1]
    mixed = _dot(o_ref[...], wo_ref[...])
    x_new = x_ref[...] + mod_ref[:, 2 * d:3 * d] * mixed
    xo_ref[...] = x_new
    _moe_prologue(x_new, mod2_ref, g2_ref, rw_ref, rb_ref, h_ref, route_ref, grow_ref)


def _post_call(kern, name, tok_inputs, mods, consts, seq, tm):
    t, d = tok_inputs[0].shape
    tps = seq // tm
    full = lambda a: pl.BlockSpec(a.shape, lambda i: (0,) * a.ndim)
    tok = pl.BlockSpec((tm, d), lambda i: (i, 0))
    modspec = lambda m: pl.BlockSpec((None, 1, m.shape[-1]), lambda i: (i // tps, 0, 0))
    return pl.pallas_call(
        kern,
        out_shape=(jax.ShapeDtypeStruct((t, d), F32), jax.ShapeDtypeStruct((t, d), BF16),
                   jax.ShapeDtypeStruct((t, LANES), BF16), jax.ShapeDtypeStruct((8, t), F32)),
        grid=(t // tm,),
        in_specs=[tok] * len(tok_inputs) + [modspec(m) for m in mods] + [full(a) for a in consts],
        out_specs=(tok, tok, pl.BlockSpec((tm, LANES), lambda i: (i, 0)), pl.BlockSpec((8, tm), lambda i: (0, i))),
        compiler_params=_params(("parallel",)),
        name=name,
    )(*tok_inputs, *mods, *consts)


def _group_count_kernel(grow_ref, o_ref):
    o_ref[0] = jnp.broadcast_to(jnp.sum(grow_ref[...], axis=1, keepdims=True), o_ref.shape[1:])


def _group_counts(grow, tm):
    n_tiles = grow.shape[1] // tm
    sums = pl.pallas_call(
        _group_count_kernel,
        out_shape=jax.ShapeDtypeStruct((n_tiles, 8, LANES), F32),
        grid=(n_tiles,),
        in_specs=[pl.BlockSpec((8, tm), lambda i: (0, i))],
        out_specs=pl.BlockSpec((1, 8, LANES), lambda i: (i, 0, 0)),
        compiler_params=_params(("parallel",)),
        name="moe_counts",
    )(grow)
    return sums[:, :N_GROUPS, 0].astype(jnp.int32).reshape(-1)


def _moe_kernel(cnt_ref, x_ref, h_ref, route_ref, grow_ref, mod_ref, wg_ref, wu_ref, wd_ref, ex_ref, tri_ref, fg_ref,
                o_ref, acc_ref, posc_ref, posr_ref, *, final):
    d = x_ref.shape[-1]
    tm = x_ref.shape[0]
    i = pl.program_id(0)
    g = pl.program_id(1)
    grp_lane0 = 3 * N_EXPERTS

    @pl.when(g == 0)
    def _():
        acc_ref[...] = jnp.zeros_like(acc_ref)
        route = route_ref[...]
        posc_ref[...] = jnp.where(route > 0, _dot(tri_ref[...], route), -1.0)
        grow = grow_ref[...]
        grow16 = jnp.concatenate([grow, jnp.zeros_like(grow)], axis=0).astype(BF16)
        posr_ref[...] = jnp.where(grow > 0, _dot_nt(grow16, tri_ref[...])[:8], -1.0)

    n = cnt_ref[i * N_GROUPS + g]
    h = h_ref[...]
    route = route_ref[...]
    lane = lax.broadcasted_iota(jnp.int32, (tm, LANES), 1)
    pos_col = jnp.sum(jnp.where(lane == grp_lane0 + g, posc_ref[...], 0.0), axis=1, keepdims=True)
    pos_row = posr_ref[pl.ds(g, 1), :]

    def run_rows(r0, rows):
        rank_r = (r0 + lax.broadcasted_iota(jnp.int32, (rows, tm), 0)).astype(F32)
        sel = jnp.where(pos_row == rank_r, 1.0, 0.0).astype(BF16)
        hs = _dot(sel, h).astype(BF16)
        cb = _dot(sel, route).astype(BF16)
        gt = _dot(hs, wg_ref[...])
        up = _dot(hs, wu_ref[...])
        cs = _dot(cb, ex_ref[...])
        act = (gt * jax.nn.sigmoid(gt)) * up * cs
        ys = _dot(act.astype(BF16), wd_ref[...]).astype(BF16)
        rank_c = (r0 + lax.broadcasted_iota(jnp.int32, (tm, rows), 1)).astype(F32)
        sel_t = jnp.where(pos_col == rank_c, 1.0, 0.0).astype(BF16)
        acc_ref[...] += _dot(sel_t, ys)

    @pl.when((n > 0) & (n <= MOE_ROWS))
    def _():
        run_rows(0, MOE_ROWS)

    @pl.when(n > MOE_ROWS)
    def _():
        run_rows(0, MOE_ROWS_WIDE)

    def extra(b, carry):
        run_rows(MOE_ROWS_WIDE + b * MOE_EXTRA_ROWS, MOE_EXTRA_ROWS)
        return carry

    lax.fori_loop(0, (jnp.maximum(n - MOE_ROWS_WIDE, 0) + MOE_EXTRA_ROWS - 1) // MOE_EXTRA_ROWS, extra, 0)

    @pl.when(g == pl.num_programs(1) - 1)
    def _():
        x_new = x_ref[...] + mod_ref[:, 2 * d:3 * d] * acc_ref[...]
        o_ref[...] = _rms(x_new) * fg_ref[...] if final else x_new


def _moe(x2d, h, route, grow, mod, wg, wu, wd, expand, tri, final_g, final, seq, tm):
    t, d = x2d.shape
    fg = wg.shape[1] // N_GROUPS
    tps = seq // tm
    counts = _group_counts(grow, tm)
    tok = lambda w: pl.BlockSpec((tm, w), lambda i, g, c: (i, 0))
    return pl.pallas_call(
        functools.partial(_moe_kernel, final=final),
        out_shape=jax.ShapeDtypeStruct((t, d), F32),
        grid_spec=pltpu.PrefetchScalarGridSpec(
            num_scalar_prefetch=1,
            grid=(t // tm, N_GROUPS),
            in_specs=[tok(d), tok(d), tok(LANES),
                      pl.BlockSpec((8, tm), lambda i, g, c: (0, i)),
                      pl.BlockSpec((None, 1, mod.shape[-1]), lambda i, g, c: (i // tps, 0, 0)),
                      pl.BlockSpec((d, fg), lambda i, g, c: (0, g)),
                      pl.BlockSpec((d, fg), lambda i, g, c: (0, g)),
                      pl.BlockSpec((fg, d), lambda i, g, c: (g, 0)),
                      pl.BlockSpec((LANES, fg), lambda i, g, c: (0, g)),
                      pl.BlockSpec((tm, tm), lambda i, g, c: (0, 0)),
                      pl.BlockSpec((1, d), lambda i, g, c: (0, 0))],
            out_specs=tok(d),
            scratch_shapes=[pltpu.VMEM((tm, d), F32), pltpu.VMEM((tm, LANES), F32), pltpu.VMEM((8, tm), F32)]),
        compiler_params=_params(("parallel", "arbitrary")),
        name="moe",
    )(counts, x2d, h, route, grow, mod, wg, wu, wd, expand, tri, final_g)


def _mla_pre_kernel(x_ref, pos_ref, modkv_ref, modq_ref, gkv_ref, gq_ref, wdkv_ref, gckv_ref, wuk_ref, wuvt_ref,
                    wka_ref, wkb_ref, wdq_ref, gcq_ref, wq_ref, freq_ref,
                    q_ref, k_ref, vt_ref):
    d = x_ref.shape[-1]
    tm = x_ref.shape[0]
    xn = _rms(x_ref[...])
    ang = pos_ref[...].astype(F32) * freq_ref[...]
    cos = jnp.cos(ang)
    sin = jnp.sin(ang)
    lane = lax.broadcasted_iota(jnp.int32, (tm, LANES), 1)
    trig_q = jnp.where(lane < QK_NOPE + QK_ROPE, cos, sin)

    hkv = (xn * gkv_ref[...] * (1.0 + modkv_ref[:, d:2 * d]) + modkv_ref[:, 0:d]).astype(BF16)
    ckv = (_rms(_dot(hkv, wdkv_ref[...])) * gckv_ref[...]).astype(BF16)
    vt = _dot_nt(wuvt_ref[...], ckv)
    row = lax.broadcasted_iota(jnp.int32, vt.shape, 0)
    vt_ref[...] = jnp.where((row & (LANES - 1)) == V_HEAD, 1.0, vt).astype(BF16)
    kr = _dot(hkv, wka_ref[...]) * cos + _dot(hkv, wkb_ref[...]) * sin
    kn = _dot(ckv, wuk_ref[...])

    hq = (xn * gq_ref[...] * (1.0 + modq_ref[:, d:2 * d]) + modq_ref[:, 0:d]).astype(BF16)
    cq = (_rms(_dot(hq, wdq_ref[...])) * gcq_ref[...]).astype(BF16)
    qa = _dot(cq, wq_ref[...])
    for hh in range(MLA_HEADS):
        sl = slice(hh * LANES, (hh + 1) * LANES)
        k_ref[:, sl] = (kn[:, sl] + kr).astype(BF16)
        q_ref[:, sl] = (qa[:, sl] * trig_q).astype(BF16)


def _mla_pre(x2d, pos2d, modkv, modq, consts, seq, tm):
    t, d = x2d.shape
    tps = seq // tm
    full = lambda a: pl.BlockSpec(a.shape, lambda i: (0,) * a.ndim)
    modspec = lambda m: pl.BlockSpec((None, 1, m.shape[-1]), lambda i: (i // tps, 0, 0))
    hq = MLA_HEADS * LANES
    return pl.pallas_call(
        _mla_pre_kernel,
        out_shape=(jax.ShapeDtypeStruct((t, hq), BF16), jax.ShapeDtypeStruct((t, hq), BF16),
                   jax.ShapeDtypeStruct((t // seq, tps, hq, tm), BF16)),
        grid=(t // tm,),
        in_specs=[pl.BlockSpec((tm, d), lambda i: (i, 0)), pl.BlockSpec((tm, 1), lambda i: (i, 0)),
                  modspec(modkv), modspec(modq)] + [full(a) for a in consts],
        out_specs=(pl.BlockSpec((tm, hq), lambda i: (i, 0)), pl.BlockSpec((tm, hq), lambda i: (i, 0)),
                   pl.BlockSpec((None, None, hq, tm), lambda i: (i // tps, i % tps, 0, 0))),
        compiler_params=_params(("parallel",)),
        name="mla_pre",
    )(x2d, pos2d, modkv, modq, *consts)


def _attn_kernel(q_ref, k_ref, vt_ref, o_ref, m_ref, acc_ref, *, blk):
    qi = pl.program_id(2)
    m_ref[...] = jnp.full_like(m_ref, -jnp.inf)
    acc_ref[...] = jnp.zeros_like(acc_ref)

    def block(j0, n_blk, diagonal=False):
        k0 = pl.multiple_of(j0 * blk, blk)
        width = n_blk * blk
        heads = range(ATTN_HEADS)
        sls = [slice(hh * LANES, (hh + 1) * LANES) for hh in heads]
        s = [_dot_nt(k_ref[pl.ds(k0, width), sl], q_ref[:, sl]) for sl in sls]
        if diagonal:
            key = lax.broadcasted_iota(jnp.int32, (width, blk), 0)
            qry = lax.broadcasted_iota(jnp.int32, (width, blk), 1) + (width - blk)
            s = [jnp.where(key <= qry, sh, -jnp.inf) for sh in s]
        for hh in heads:
            m_old = m_ref[hh]
            m_new = jnp.maximum(m_old, jnp.max(s[hh], axis=0, keepdims=True))
            p = jnp.exp2(s[hh] - m_new).astype(BF16)
            pv = _dot(vt_ref[j0, sls[hh], :], p[:blk])
            for i in range(1, n_blk):
                pv = pv + _dot(vt_ref[j0 + i, sls[hh], :], p[i * blk:(i + 1) * blk])
            acc_ref[hh] = acc_ref[hh] * jnp.exp2(m_old - m_new) + pv
            m_ref[hh] = m_new

    n_pairs = qi // 2

    def body(j, carry):
        block(4 * j, 4)
        return carry

    lax.fori_loop(0, n_pairs // 2, body, 0)

    @pl.when(n_pairs % 2 == 1)
    def _():
        block(2 * (n_pairs - 1), 2)

    @pl.when(qi % 2 == 0)
    def _():
        block(qi, 1, diagonal=True)

    @pl.when(qi % 2 == 1)
    def _():
        block(qi - 1, 2, diagonal=True)

    outs = []
    for hh in range(ATTN_HEADS):
        acc = acc_ref[hh].T
        outs.append((acc / acc[:, V_HEAD:V_HEAD + 1])[:, :V_HEAD])
    o_ref[...] = jnp.concatenate(outs, axis=1).astype(BF16)


def _attn(q, k, vt, blk):
    bsz, s, _ = q.shape
    hp = MLA_HEADS // ATTN_HEADS
    nh = ATTN_HEADS
    return pl.pallas_call(
        functools.partial(_attn_kernel, blk=blk),
        out_shape=jax.ShapeDtypeStruct((bsz, s, MLA_HEADS * V_HEAD), BF16),
        grid=(bsz, hp, s // blk),
        in_specs=[pl.BlockSpec((None, blk, nh * LANES), lambda b, h, qi: (b, qi, h)),
                  pl.BlockSpec((None, s, nh * LANES), lambda b, h, qi: (b, 0, h), pipeline_mode=pl.Buffered(1)),
                  pl.BlockSpec((None, s // blk, nh * LANES, blk), lambda b, h, qi: (b, 0, h, 0),
                               pipeline_mode=pl.Buffered(1))],
        out_specs=pl.BlockSpec((None, blk, nh * V_HEAD), lambda b, h, qi: (b, qi, h)),
        scratch_shapes=[pltpu.VMEM((nh, 1, blk), F32), pltpu.VMEM((nh, LANES, blk), F32)],
        compiler_params=_params(("parallel", "parallel", "arbitrary")),
        name="attn",
    )(q, k, vt)


def kernel(x, c, positions, ada_w, ada_b, norm_g, rwkv_mu, rwkv_w_rkv, rwkv_w0, rwkv_w1, rwkv_w2, rwkv_a0, rwkv_a1, rwkv_a2, rwkv_g1, rwkv_g2, rwkv_k_k, rwkv_k_a, rwkv_r_k, rwkv_lnx_w, rwkv_lnx_b, rwkv_w_o, kv_ada_w, kv_ada_b, kv_norm_g, mla_w_dkv, mla_g_kv, mla_w_uk, mla_w_uv, mla_w_kr, mla_w_dq, mla_g_q, mla_w_uq, mla_w_qr, mla_w_o, router_w, router_b, moe_w_gu, moe_w_down, final_g):
    bsz, seq, d = x.shape
    depth = ada_w.shape[0]
    n_a = rwkv_mu.shape[0]
    t = bsz * seq
    tm = min(512, seq)
    row = lambda a: a.reshape(1, -1).astype(F32)

    c_pad = jnp.pad(c, ((0, 8 - bsz), (0, 0)))
    mods = _ada(c_pad, ada_w.reshape(depth * 2, d, 3 * d), ada_b.reshape(depth * 2, 1, 3 * d))
    mods = mods[:, :bsz].reshape(depth, 2, bsz, 1, 3 * d)
    mod_kv = _ada(c_pad, kv_ada_w[None], kv_ada_b.reshape(1, 1, 2 * d))[0, :bsz].reshape(bsz, 1, 2 * d)

    idx = jnp.arange(MXU_DIM) // RWKV_HEAD_DIM
    bd = (idx[:, None] == idx[None, :]).astype(BF16)
    rw_t = router_w.T.astype(F32)
    rb_col = router_b.reshape(N_EXPERTS, 1).astype(F32)
    d_exp = moe_w_down.shape[2]
    f_all = N_EXPERTS * d_exp
    exp_id = jnp.arange(f_all) // d_exp
    expand = (jnp.arange(N_EXPERTS)[:, None] == exp_id[None, :]).astype(BF16)
    expand = jnp.concatenate([expand] * 3 + [jnp.zeros((LANES - 3 * N_EXPERTS, f_all), BF16)], axis=0)
    tm_moe = min(1024, seq)
    tri = (jnp.arange(tm_moe)[:, None] > jnp.arange(tm_moe)[None, :]).astype(BF16)

    def moe_weights(layer):
        wgu = moe_w_gu[layer]
        wg = wgu[:, :, :d_exp].transpose(1, 0, 2).reshape(d, f_all).astype(BF16)
        wu = wgu[:, :, d_exp:].transpose(1, 0, 2).reshape(d, f_all).astype(BF16)
        wd = moe_w_down[layer].reshape(f_all, d).astype(BF16)
        return wg, wu, wd

    inv_freq = ROPE_THETA ** (-jnp.arange(0, QK_ROPE, 2, dtype=F32) / QK_ROPE)
    freq = jnp.concatenate([jnp.zeros((QK_NOPE,), F32)] + [inv_freq] * 4).reshape(1, LANES)
    pos2d = positions.reshape(t, 1)

    x2d = x.reshape(t, d)
    assert depth - n_a == 1
    for layer in range(depth):
        mod_mix = mods[layer, 0]
        mod_ffn = mods[layer, 1]
        if layer < n_a:
            a = layer
            mu8 = jnp.pad(rwkv_mu[a], ((0, 2), (0, 0)))
            vecs = jnp.stack([rwkv_w0[a], rwkv_a0[a], rwkv_k_k[a], rwkv_k_a[a], rwkv_r_k[a].reshape(d),
                              jnp.zeros((d,), F32), jnp.zeros((d,), F32), jnp.zeros((d,), F32)])
            r, k, v, an, b, lw, gate, bonus = _rwkv_pre(
                x2d, mod_mix, row(norm_g[layer, 0]), mu8, rwkv_w_rkv[a].astype(BF16),
                rwkv_w1[a].astype(BF16), rwkv_w2[a].astype(BF16), rwkv_a1[a].astype(BF16), rwkv_a2[a].astype(BF16),
                rwkv_g1[a].astype(BF16), rwkv_g2[a].astype(BF16), vecs, bd, seq, tm)
            sh = lambda z: z.reshape(bsz, seq, d)
            ncs = min(16, seq // WKV_CHUNK)
            qm, y0, gm, cm = _wkv_a(sh(r), sh(k), sh(v), sh(an), sh(b), sh(lw), ncs)
            y = _wkv_b(qm, y0, gm, cm, min(4, seq // WKV_CHUNK)).reshape(t, d)
            x2d, h, route, grow = _post_call(
                _rwkv_post_kernel, "rwkv_post", [x2d, y, bonus, gate], [mod_mix, mod_ffn],
                [row(rwkv_lnx_w[a]), row(rwkv_lnx_b[a]), rwkv_w_o[a].astype(BF16), bd,
                 row(norm_g[layer, 1]), rw_t, rb_col], seq, tm)
        else:
            bl = layer - n_a
            scale = (QK_NOPE + QK_ROPE) ** -0.5 * LOG2_E
            hd = MLA_HEADS
            zpad = lambda w, lo, hi: jnp.pad(w, ((0, 0), (0, 0), (lo, hi)))
            kvl = mla_w_uk.shape[0]
            wuk = zpad(mla_w_uk, 0, LANES - QK_NOPE).reshape(kvl, hd * LANES).astype(BF16)
            wuvt = zpad(mla_w_uv, 0, LANES - V_HEAD).reshape(kvl, hd * LANES).T.astype(BF16)
            k1, k2 = jnp.split(mla_w_kr, 2, axis=-1)
            nope_pad = lambda w: jnp.pad(w, ((0, 0), (QK_NOPE, 0))).astype(BF16)
            wka = nope_pad(jnp.concatenate([k1, k2, k2, -k1], axis=-1))
            wkb = nope_pad(jnp.concatenate([-k2, k1, k1, k2], axis=-1))
            ql = mla_w_uq.shape[1]
            wq = jnp.concatenate([mla_w_uq[bl], mla_w_qr[bl], mla_w_qr[bl]], axis=-1) * scale
            wq = wq.reshape(ql, hd * LANES).astype(BF16)
            consts = [row(kv_norm_g), row(norm_g[layer, 0]), mla_w_dkv.astype(BF16), row(mla_g_kv), wuk, wuvt,
                      wka, wkb, mla_w_dq[bl].astype(BF16), row(mla_g_q[bl]), wq, freq]
            ta = min(ATTN_BLOCK, seq)
            q, kf, vt = _mla_pre(x2d, pos2d, mod_kv, mod_mix, consts, seq, ta)
            o = _attn(q.reshape(bsz, seq, -1), kf.reshape(bsz, seq, -1), vt, ta)
            x2d, h, route, grow = _post_call(
                _mla_post_kernel, "mla_post", [x2d, o.reshape(t, d)], [mod_mix, mod_ffn],
                [mla_w_o[bl].astype(BF16), row(norm_g[layer, 1]), rw_t, rb_col], seq, tm)
        wg, wu, wd = moe_weights(layer)
        x2d = _moe(x2d, h, route, grow, mod_ffn, wg, wu, wd, expand, tri, row(final_g), layer == depth - 1, seq,
                   tm_moe)
    return x2d.reshape(bsz, seq, d)
```

```python
import functools

import jax
import jax.numpy as jnp
from jax import lax
from jax.experimental import pallas as pl
from jax.experimental.pallas import tpu as pltpu

F32 = jnp.float32
BF16 = jnp.bfloat16

NORM_EPS = 1e-6
GN_EPS = 64e-5
ROPE_THETA = 10000.0
LOG2_E = 1.4426950408889634
RWKV_HEAD_DIM = 64
MLA_HEADS = 16
QK_NOPE = 64
QK_ROPE = 32
V_HEAD = 64
N_EXPERTS = 16
N_GROUPS = 4
EXPERTS_PER_GROUP = N_EXPERTS // N_GROUPS
TOP_K = 2

LANES = 128
MXU_DIM = 256
WKV_CHUNK = 64
ATTN_BLOCK = 512
ATTN_HEADS = 4
MOE_ROWS = 256
MOE_ROWS_WIDE = 320
MOE_EXTRA_ROWS = 64
VMEM_LIMIT = 56 * 1024 * 1024


def _dot(a, b):
    return jnp.dot(a, b, preferred_element_type=F32)


def _dot_nt(a, b):
    return lax.dot_general(a, b, (((1,), (1,)), ((), ())), preferred_element_type=F32)


def _split2(x):
    hi = x.astype(BF16)
    lo = (x - hi.astype(F32)).astype(BF16)
    return hi, lo


def _split3(x):
    hi = x.astype(BF16)
    r1 = x - hi.astype(F32)
    mid = r1.astype(BF16)
    lo = (r1 - mid.astype(F32)).astype(BF16)
    return hi, mid, lo


def _dot_x3(a, b):
    ah, al = _split2(a)
    bh, bl = _split2(b)
    return _dot(ah, bh) + (_dot(ah, bl) + _dot(al, bh))


def _rms(x):
    return x * lax.rsqrt(jnp.mean(x * x, axis=-1, keepdims=True) + NORM_EPS)


def _seg_bcast_sum(x, bd, two_terms=False):
    d = x.shape[-1]
    terms = _split2(x) if two_terms else (x.astype(BF16),)
    outs = []
    for j in range(d // MXU_DIM):
        sl = slice(j * MXU_DIM, (j + 1) * MXU_DIM)
        acc = _dot(terms[0][:, sl], bd)
        for t in terms[1:]:
            acc = acc + _dot(t[:, sl], bd)
        outs.append(acc)
    return jnp.concatenate(outs, axis=-1)


def _params(sem):
    return pltpu.CompilerParams(dimension_semantics=sem, vmem_limit_bytes=VMEM_LIMIT)


def _ada_kernel(c_ref, w_ref, b_ref, o_ref):
    c = c_ref[...]
    s = c * jax.nn.sigmoid(c)
    o_ref[0] = _dot_x3(s, w_ref[0]) + b_ref[0]


def _ada(c_pad, w, b, tn=1024):
    n, d, nn = w.shape
    return pl.pallas_call(
        _ada_kernel,
        out_shape=jax.ShapeDtypeStruct((n, 8, nn), F32),
        grid=(n, nn // tn),
        in_specs=[pl.BlockSpec((8, d), lambda i, j: (0, 0)),
                  pl.BlockSpec((1, d, tn), lambda i, j: (i, 0, j)),
                  pl.BlockSpec((1, 1, tn), lambda i, j: (i, 0, j))],
        out_specs=pl.BlockSpec((1, 8, tn), lambda i, j: (i, 0, j)),
        compiler_params=_params(("parallel", "parallel")),
        name="ada",
    )(c_pad, w, b)


def _rwkv_pre_kernel(x_ref, xp_ref, mod_ref, g_ref, mu_ref, wrkv_ref, w1_ref, w2_ref, a1_ref, a2_ref,
                     g1_ref, g2_ref, vec_ref, bd_ref,
                     r_ref, k_ref, v_ref, an_ref, b_ref, lw_ref, gate_ref, bonus_ref, *, tiles_per_seq):
    d = x_ref.shape[-1]
    tm = x_ref.shape[0]
    i = pl.program_id(0)
    shift = mod_ref[:, 0:d]
    scale = mod_ref[:, d:2 * d]
    gn = g_ref[...]

    def modulate(xv):
        return _rms(xv) * gn * (1.0 + scale) + shift

    h = modulate(x_ref[...])
    hp_last = modulate(xp_ref[...])[7:8, :]
    hp_last = jnp.where(i % tiles_per_seq == 0, 0.0, hp_last)
    row = lax.broadcasted_iota(jnp.int32, (tm, d), 0)
    h_prev = jnp.where(row == 0, hp_last, pltpu.roll(h, 1, axis=0))
    xx = h_prev - h

    def mix(j):
        return (h + xx * mu_ref[j:j + 1, :]).astype(BF16)

    w0 = vec_ref[0:1, :]
    a0 = vec_ref[1:2, :]
    k_k = vec_ref[2:3, :]
    k_a = vec_ref[3:4, :]
    r_k = vec_ref[4:5, :]
    bd = bd_ref[...]

    r = _dot(mix(0), wrkv_ref[0])
    k = _dot(mix(2), wrkv_ref[1])
    v = _dot(mix(3), wrkv_ref[2])
    z = w0 + _dot(jnp.tanh(_dot(mix(1), w1_ref[...])).astype(BF16), w2_ref[...])
    w_log = -(jnp.maximum(-z, 0.0) + jnp.log(1.0 + jnp.exp(-jnp.abs(z)))) - 0.5
    lw_ref[...] = -jnp.exp(w_log)
    a = jax.nn.sigmoid(a0 + _dot(_dot(mix(4), a1_ref[...]).astype(BF16), a2_ref[...]))
    gate_ref[...] = _dot(jax.nn.sigmoid(_dot(mix(5), g1_ref[...])).astype(BF16), g2_ref[...]).astype(BF16)

    kk = k * k_k
    kk = kk / jnp.maximum(jnp.sqrt(_seg_bcast_sum(kk * kk, bd)), 1e-12)
    km = k * (1.0 + (a - 1.0) * k_a)
    bonus_ref[...] = (_seg_bcast_sum(r * km * r_k, bd) * v).astype(BF16)
    r_ref[...] = r.astype(BF16)
    k_ref[...] = km.astype(BF16)
    v_ref[...] = v.astype(BF16)
    an_ref[...] = (-kk).astype(BF16)
    b_ref[...] = (kk * a).astype(BF16)


def _rwkv_pre(x2d, mod, norm_g, mu8, wrkv, w1, w2, a1, a2, g1, g2, vecs, bd, seq, tm):
    t, d = x2d.shape
    tps = seq // tm
    full = lambda a: pl.BlockSpec(a.shape, lambda i: (0,) * a.ndim)
    tok = pl.BlockSpec((tm, d), lambda i: (i, 0))
    out_bf = jax.ShapeDtypeStruct((t, d), BF16)
    return pl.pallas_call(
        functools.partial(_rwkv_pre_kernel, tiles_per_seq=tps),
        out_shape=(out_bf, out_bf, out_bf, out_bf, out_bf, jax.ShapeDtypeStruct((t, d), F32), out_bf, out_bf),
        grid=(t // tm,),
        in_specs=[tok,
                  pl.BlockSpec((8, d), lambda i: (jnp.maximum(i * (tm // 8) - 1, 0), 0)),
                  pl.BlockSpec((None, 1, mod.shape[-1]), lambda i: (i // tps, 0, 0)),
                  full(norm_g), full(mu8), full(wrkv), full(w1), full(w2), full(a1), full(a2),
                  full(g1), full(g2), full(vecs), full(bd)],
        out_specs=(tok,) * 8,
        compiler_params=_params(("parallel",)),
        name="rwkv_pre",
    )(x2d, x2d, mod, norm_g, mu8, wrkv, w1, w2, a1, a2, g1, g2, vecs, bd)


def _wkv_a_kernel(r_ref, k_ref, v_ref, an_ref, b_ref, lw_ref, q_ref, y0_ref, g_ref, c_ref, *, n_chunks):
    L = WKV_CHUNK
    L2 = 2 * L
    lane = lax.broadcasted_iota(jnp.int32, (L, LANES), 1)
    head0 = lane < RWKV_HEAD_DIM
    ri = lax.broadcasted_iota(jnp.int32, (L2, L2), 0)
    ci = lax.broadcasted_iota(jnp.int32, (L2, L2), 1)
    strict = ci < ri
    incl = ci <= ri
    eye = ci == ri
    tri = (lax.broadcasted_iota(jnp.int32, (L, L), 1) <= lax.broadcasted_iota(jnp.int32, (L, L), 0)).astype(BF16)
    tri3 = jnp.concatenate([tri] * 3, axis=1)

    def stack(xv):
        return jnp.concatenate([jnp.where(head0, xv, 0.0), jnp.where(head0, 0.0, xv)], axis=0)

    chunks = range(n_chunks)
    pre = []
    for c in chunks:
        sl = pl.ds(c * L, L)
        lw = lw_ref[sl, :]
        cum = _dot(tri3, jnp.concatenate(_split3(lw), axis=0))
        cl = cum[L - 1:L, :]
        g_in = jnp.exp(cum)
        g_ex = jnp.exp(cum - lw)
        g_inv = jnp.exp(-cum)
        g_end = jnp.exp(cl - cum)
        kf = k_ref[sl, :].astype(F32)
        bf = b_ref[sl, :].astype(F32)
        pre.append(dict(
            a2=stack(an_ref[sl, :].astype(F32) * g_ex).astype(BF16),
            r2=stack(r_ref[sl, :].astype(F32) * g_in),
            b2=stack(bf * g_inv).astype(BF16),
            k2=stack(kf * g_inv).astype(BF16),
            v2=stack(v_ref[sl, :].astype(F32)).astype(BF16),
            bg2t=stack(bf * g_end).T.astype(BF16),
            kg2t=stack(kf * g_end).T.astype(BF16),
            g_last=jnp.exp(cl)))

    gram = [_dot_nt(jnp.concatenate([p["a2"], p["r2"].astype(BF16)], axis=0),
                    jnp.concatenate([p["b2"], p["k2"]], axis=0)) for p in pre]
    m_ab = [jnp.where(strict, g[:L2, :L2], 0.0) for g in gram]
    m_ak = [jnp.where(strict, g[:L2, L2:], 0.0) for g in gram]
    m_rb = [jnp.where(incl, g[L2:, :L2], 0.0) for g in gram]
    m_rk = [jnp.where(incl, g[L2:, L2:], 0.0) for g in gram]
    mv = [_dot(jnp.concatenate([m_ak[c], m_rk[c]], axis=0).astype(BF16), pre[c]["v2"]) for c in chunks]
    kgv = [_dot(p["kg2t"], p["v2"]) for p in pre]

    nb = [n.astype(BF16) for n in m_ab]
    nk = [_dot(n, n) for n in nb]
    tinv = [jnp.where(eye, 1.0, n) for n in m_ab]
    for step in range(1, 6):
        nb = [n.astype(BF16) for n in nk]
        if step < 5:
            both = [_dot(nb[c], jnp.concatenate([nb[c], tinv[c].astype(BF16)], axis=1)) for c in chunks]
            nk = [m[:, :L2] for m in both]
            tinv = [tinv[c] + both[c][:, L2:] for c in chunks]
        else:
            tinv = [tinv[c] + _dot(nb[c], tinv[c].astype(BF16)) for c in chunks]

    wu = [_dot(tinv[c].astype(BF16), jnp.concatenate([pre[c]["a2"], mv[c][:L2].astype(BF16)], axis=1)) for c in chunks]
    x = [_dot(jnp.concatenate([m_rb[c].astype(BF16), pre[c]["bg2t"]], axis=0), wu[c].astype(BF16)) for c in chunks]

    fold = lambda m: m[:L] + m[L:]
    for c in chunks:
        q_ref[c, 0] = fold(pre[c]["r2"] + x[c][:L2, :L2]).astype(BF16)
        y0_ref[pl.ds(c * L, L), :] = fold(x[c][:L2, L2:] + mv[c][L2:])
        g_ref[c, 0] = fold(jnp.where(eye, jnp.broadcast_to(pre[c]["g_last"], (L2, L2)), 0.0)
                           + x[c][L2:, :L2]).astype(BF16)
        c_ref[c, 0] = fold(x[c][L2:, L2:] + kgv[c]).astype(BF16)


def _wkv_a(r, k, v, an, b, lw, n_chunks_step):
    bsz, s, d = r.shape
    L = WKV_CHUNK
    pairs = d // LANES
    nck = s // L
    rows = n_chunks_step * L
    tok = pl.BlockSpec((None, rows, LANES), lambda bi, ci, pi: (bi, ci, pi))
    mat = pl.BlockSpec((n_chunks_step, 1, L, LANES), lambda bi, ci, pi: (ci, bi * pairs + pi, 0, 0))
    mshape = jax.ShapeDtypeStruct((nck, bsz * pairs, L, LANES), BF16)
    return pl.pallas_call(
        functools.partial(_wkv_a_kernel, n_chunks=n_chunks_step),
        out_shape=(mshape, jax.ShapeDtypeStruct((bsz, s, d), F32), mshape, mshape),
        grid=(bsz, nck // n_chunks_step, pairs),
        in_specs=[tok] * 6,
        out_specs=(mat, tok, mat, mat),
        compiler_params=_params(("parallel", "parallel", "parallel")),
        name="wkv_a",
    )(r, k, v, an, b, lw)


def _wkv_b_kernel(q_ref, y0_ref, g_ref, c_ref, y_ref, s_ref, *, n_chunks, bsz, pairs):
    L = WKV_CHUNK
    head0 = lax.broadcasted_iota(jnp.int32, (L, LANES), 1) < RWKV_HEAD_DIM
    zero = jnp.zeros((L, LANES), BF16)

    def unfold(m):
        return jnp.concatenate([jnp.where(head0, m, zero), jnp.where(head0, zero, m)], axis=0)

    @pl.when(pl.program_id(0) == 0)
    def _():
        s_ref[...] = jnp.zeros_like(s_ref)

    for c in range(n_chunks):
        for bi in range(bsz):
            for pi in range(pairs):
                n = bi * pairs + pi
                sb = s_ref[n].astype(BF16)
                y2 = _dot(unfold(q_ref[c, n]), sb)
                y_ref[bi, pl.ds(c * L, L), pl.ds(pi * LANES, LANES)] = (
                    y2[:L] + y2[L:] + y0_ref[bi, pl.ds(c * L, L), pl.ds(pi * LANES, LANES)])
                s_ref[n] = _dot(unfold(g_ref[c, n]), sb) + unfold(c_ref[c, n]).astype(F32)


def _wkv_b(q, y0, g, cmat, n_chunks_step):
    nck, bp, L, _ = q.shape
    L2 = 2 * L
    bsz, s, d = y0.shape
    pairs = d // LANES
    rows = n_chunks_step * L
    mat = pl.BlockSpec((n_chunks_step, bp, L, LANES), lambda ci: (ci, 0, 0, 0))
    tok = pl.BlockSpec((bsz, rows, d), lambda ci: (0, ci, 0))
    return pl.pallas_call(
        functools.partial(_wkv_b_kernel, n_chunks=n_chunks_step, bsz=bsz, pairs=pairs),
        out_shape=jax.ShapeDtypeStruct((bsz, s, d), F32),
        grid=(nck // n_chunks_step,),
        in_specs=[mat, tok, mat, mat],
        out_specs=tok,
        scratch_shapes=[pltpu.VMEM((bp, L2, L2), F32)],
        compiler_params=_params(("arbitrary",)),
        name="wkv_b",
    )(q, y0, g, cmat)


def _router_combine(h, rw_ref, rb_ref):
    rw = rw_ref[...]
    hh, hl = _split2(h)
    wh, wl = _split2(rw)
    logits = _dot_nt(wh, hh) + (_dot_nt(wh, hl) + _dot_nt(wl, hh))
    aff = jax.nn.sigmoid(logits)
    sel = aff + rb_ref[...]
    affr = [aff[e:e + 1, :] for e in range(N_EXPERTS)]
    selr = [sel[e:e + 1, :] for e in range(N_EXPERTS)]
    top = []
    score = []
    for g in range(N_GROUPS):
        es = range(g * EXPERTS_PER_GROUP, (g + 1) * EXPERTS_PER_GROUP)
        sc = None
        for e in es:
            rank = None
            for o in es:
                if o == e:
                    continue
                beats = ((selr[o] >= selr[e]) if o < e else (selr[o] > selr[e])).astype(F32)
                rank = beats if rank is None else rank + beats
            t = rank < float(TOP_K)
            top.append(t)
            contrib = jnp.where(t, selr[e], 0.0)
            sc = contrib if sc is None else sc + contrib
        score.append(sc)
    best = score[0]
    for g in range(1, N_GROUPS):
        best = jnp.maximum(best, score[g])
    taken = None
    rows = []
    groups = []
    for g in range(N_GROUPS):
        is_best = score[g] == best
        if taken is not None:
            is_best = is_best & jnp.logical_not(taken)
        taken = is_best if taken is None else (taken | is_best)
        groups.append(is_best.astype(F32))
        for e in range(g * EXPERTS_PER_GROUP, (g + 1) * EXPERTS_PER_GROUP):
            rows.append(jnp.where(is_best & top[e], affr[e], 0.0))
    comb = jnp.concatenate(rows, axis=0)
    comb = comb / jnp.sum(comb, axis=0, keepdims=True)
    tm = comb.shape[1]
    grp = jnp.concatenate(groups, axis=0)
    pad = jnp.zeros((LANES - 3 * N_EXPERTS - N_GROUPS, tm), F32)
    route = jnp.concatenate(list(_split3(comb)) + [grp, pad], axis=0).T.astype(BF16)
    grow = jnp.concatenate([grp, jnp.zeros((8 - N_GROUPS, tm), F32)], axis=0)
    return route, grow


def _moe_prologue(x_new, mod2_ref, g2_ref, rw_ref, rb_ref, h_ref, route_ref, grow_ref):
    d = x_new.shape[-1]
    h = _rms(x_new) * g2_ref[...] * (1.0 + mod2_ref[:, d:2 * d]) + mod2_ref[:, 0:d]
    h_ref[...] = h.astype(BF16)
    route_ref[...], grow_ref[...] = _router_combine(h, rw_ref, rb_ref)


def _rwkv_post_kernel(x_ref, y_ref, bonus_ref, gate_ref, mod_ref, mod2_ref, lnw_ref, lnb_ref, wo_ref, bd_ref,
                      g2_ref, rw_ref, rb_ref, xo_ref, h_ref, route_ref, grow_ref):
    d = x_ref.shape[-1]
    bd = bd_ref[...]
    y = y_ref[...]
    inv_n = 1.0 / RWKV_HEAD_DIM
    yc = y - _seg_bcast_sum(y, bd, two_terms=True) * inv_n
    var = _seg_bcast_sum(yc * yc, bd) * inv_n
    yn = yc * lax.rsqrt(var + GN_EPS) * lnw_ref[...] + lnb_ref[...]
    o = (yn + bonus_ref[...].astype(F32)) * gate_ref[...].astype(F32)
    mixed = _dot(o.astype(BF16), wo_ref[...])
    x_new = x_ref[...] + mod_ref[:, 2 * d:3 * d] * mixed
    xo_ref[...] = x_new
    _moe_prologue(x_new, mod2_ref, g2_ref, rw_ref, rb_ref, h_ref, route_ref, grow_ref)


def _mla_post_kernel(x_ref, o_ref, mod_ref, mod2_ref, wo_ref, g2_ref, rw_ref, rb_ref,
                     xo_ref, h_ref, route_ref, grow_ref):
    d = x_ref.shape[-1]
    mixed = _dot(o_ref[...], wo_ref[...])
    x_new = x_ref[...] + mod_ref[:, 2 * d:3 * d] * mixed
    xo_ref[...] = x_new
    _moe_prologue(x_new, mod2_ref, g2_ref, rw_ref, rb_ref, h_ref, route_ref, grow_ref)


def _post_call(kern, name, tok_inputs, mods, consts, seq, tm):
    t, d = tok_inputs[0].shape
    tps = seq // tm
    full = lambda a: pl.BlockSpec(a.shape, lambda i: (0,) * a.ndim)
    tok = pl.BlockSpec((tm, d), lambda i: (i, 0))
    modspec = lambda m: pl.BlockSpec((None, 1, m.shape[-1]), lambda i: (i // tps, 0, 0))
    return pl.pallas_call(
        kern,
        out_shape=(jax.ShapeDtypeStruct((t, d), F32), jax.ShapeDtypeStruct((t, d), BF16),
                   jax.ShapeDtypeStruct((t, LANES), BF16), jax.ShapeDtypeStruct((8, t), F32)),
        grid=(t // tm,),
        in_specs=[tok] * len(tok_inputs) + [modspec(m) for m in mods] + [full(a) for a in consts],
        out_specs=(tok, tok, pl.BlockSpec((tm, LANES), lambda i: (i, 0)), pl.BlockSpec((8, tm), lambda i: (0, i))),
        compiler_params=_params(("parallel",)),
        name=name,
    )(*tok_inputs, *mods, *consts)


def _group_count_kernel(grow_ref, o_ref):
    o_ref[0] = jnp.broadcast_to(jnp.sum(grow_ref[...], axis=1, keepdims=True), o_ref.shape[1:])


def _group_counts(grow, tm):
    n_tiles = grow.shape[1] // tm
    sums = pl.pallas_call(
        _group_count_kernel,
        out_shape=jax.ShapeDtypeStruct((n_tiles, 8, LANES), F32),
        grid=(n_tiles,),
        in_specs=[pl.BlockSpec((8, tm), lambda i: (0, i))],
        out_specs=pl.BlockSpec((1, 8, LANES), lambda i: (i, 0, 0)),
        compiler_params=_params(("parallel",)),
        name="moe_counts",
    )(grow)
    return sums[:, :N_GROUPS, 0].astype(jnp.int32).reshape(-1)


def _moe_kernel(cnt_ref, x_ref, h_ref, route_ref, grow_ref, mod_ref, wg_ref, wu_ref, wd_ref, ex_ref, tri_ref, fg_ref,
                o_ref, acc_ref, posc_ref, posr_ref, *, final):
    d = x_ref.shape[-1]
    tm = x_ref.shape[0]
    i = pl.program_id(0)
    g = pl.program_id(1)
    grp_lane0 = 3 * N_EXPERTS

    @pl.when(g == 0)
    def _():
        acc_ref[...] = jnp.zeros_like(acc_ref)
        route = route_ref[...]
        posc_ref[...] = jnp.where(route > 0, _dot(tri_ref[...], route), -1.0)
        grow = grow_ref[...]
        grow16 = jnp.concatenate([grow, jnp.zeros_like(grow)], axis=0).astype(BF16)
        posr_ref[...] = jnp.where(grow > 0, _dot_nt(grow16, tri_ref[...])[:8], -1.0)

    n = cnt_ref[i * N_GROUPS + g]
    h = h_ref[...]
    route = route_ref[...]
    lane = lax.broadcasted_iota(jnp.int32, (tm, LANES), 1)
    pos_col = jnp.sum(jnp.where(lane == grp_lane0 + g, posc_ref[...], 0.0), axis=1, keepdims=True)
    pos_row = posr_ref[pl.ds(g, 1), :]

    def run_rows(r0, rows):
        rank_r = (r0 + lax.broadcasted_iota(jnp.int32, (rows, tm), 0)).astype(F32)
        sel = jnp.where(pos_row == rank_r, 1.0, 0.0).astype(BF16)
        hs = _dot(sel, h).astype(BF16)
        cb = _dot(sel, route).astype(BF16)
        gt = _dot(hs, wg_ref[...])
        up = _dot(hs, wu_ref[...])
        cs = _dot(cb, ex_ref[...])
        act = (gt * jax.nn.sigmoid(gt)) * up * cs
        ys = _dot(act.astype(BF16), wd_ref[...]).astype(BF16)
        rank_c = (r0 + lax.broadcasted_iota(jnp.int32, (tm, rows), 1)).astype(F32)
        sel_t = jnp.where(pos_col == rank_c, 1.0, 0.0).astype(BF16)
        acc_ref[...] += _dot(sel_t, ys)

    @pl.when((n > 0) & (n <= MOE_ROWS))
    def _():
        run_rows(0, MOE_ROWS)

    @pl.when(n > MOE_ROWS)
    def _():
        run_rows(0, MOE_ROWS_WIDE)

    def extra(b, carry):
        run_rows(MOE_ROWS_WIDE + b * MOE_EXTRA_ROWS, MOE_EXTRA_ROWS)
        return carry

    lax.fori_loop(0, (jnp.maximum(n - MOE_ROWS_WIDE, 0) + MOE_EXTRA_ROWS - 1) // MOE_EXTRA_ROWS, extra, 0)

    @pl.when(g == pl.num_programs(1) - 1)
    def _():
        x_new = x_ref[...] + mod_ref[:, 2 * d:3 * d] * acc_ref[...]
        o_ref[...] = _rms(x_new) * fg_ref[...] if final else x_new


def _moe(x2d, h, route, grow, mod, wg, wu, wd, expand, tri, final_g, final, seq, tm):
    t, d = x2d.shape
    fg = wg.shape[1] // N_GROUPS
    tps = seq // tm
    counts = _group_counts(grow, tm)
    tok = lambda w: pl.BlockSpec((tm, w), lambda i, g, c: (i, 0))
    return pl.pallas_call(
        functools.partial(_moe_kernel, final=final),
        out_shape=jax.ShapeDtypeStruct((t, d), F32),
        grid_spec=pltpu.PrefetchScalarGridSpec(
            num_scalar_prefetch=1,
            grid=(t // tm, N_GROUPS),
            in_specs=[tok(d), tok(d), tok(LANES),
                      pl.BlockSpec((8, tm), lambda i, g, c: (0, i)),
                      pl.BlockSpec((None, 1, mod.shape[-1]), lambda i, g, c: (i // tps, 0, 0)),
                      pl.BlockSpec((d, fg), lambda i, g, c: (0, g)),
                      pl.BlockSpec((d, fg), lambda i, g, c: (0, g)),
                      pl.BlockSpec((fg, d), lambda i, g, c: (g, 0)),
                      pl.BlockSpec((LANES, fg), lambda i, g, c: (0, g)),
                      pl.BlockSpec((tm, tm), lambda i, g, c: (0, 0)),
                      pl.BlockSpec((1, d), lambda i, g, c: (0, 0))],
            out_specs=tok(d),
            scratch_shapes=[pltpu.VMEM((tm, d), F32), pltpu.VMEM((tm, LANES), F32), pltpu.VMEM((8, tm), F32)]),
        compiler_params=_params(("parallel", "arbitrary")),
        name="moe",
    )(counts, x2d, h, route, grow, mod, wg, wu, wd, expand, tri, final_g)


def _mla_pre_kernel(x_ref, pos_ref, modkv_ref, modq_ref, gkv_ref, gq_ref, wdkv_ref, gckv_ref, wuk_ref, wuvt_ref,
                    wkab_ref, wdq_ref, gcq_ref, wq_ref, freq_ref,
                    q_ref, k_ref, vt_ref):
    d = x_ref.shape[-1]
    tm = x_ref.shape[0]
    xn = _rms(x_ref[...])
    ang = pos_ref[...].astype(F32) * freq_ref[...]
    cos = jnp.cos(ang)
    sin = jnp.sin(ang)
    lane = lax.broadcasted_iota(jnp.int32, (tm, LANES), 1)
    trig_q = jnp.where(lane < QK_NOPE + QK_ROPE, cos, sin)

    hkv = (xn * gkv_ref[...] * (1.0 + modkv_ref[:, d:2 * d]) + modkv_ref[:, 0:d]).astype(BF16)
    ckv = (_rms(_dot(hkv, wdkv_ref[...])) * gckv_ref[...]).astype(BF16)
    vt = _dot_nt(wuvt_ref[...], ckv)
    row = lax.broadcasted_iota(jnp.int32, vt.shape, 0)
    vt_ref[...] = jnp.where((row & (LANES - 1)) == V_HEAD, 1.0, vt).astype(BF16)
    kab = _dot(hkv, wkab_ref[...])
    kr = kab[:, :LANES] * cos + kab[:, LANES:] * sin
    kn = _dot(ckv, wuk_ref[...])

    hq = (xn * gq_ref[...] * (1.0 + modq_ref[:, d:2 * d]) + modq_ref[:, 0:d]).astype(BF16)
    cq = (_rms(_dot(hq, wdq_ref[...])) * gcq_ref[...]).astype(BF16)
    qa = _dot(cq, wq_ref[...])
    for hh in range(MLA_HEADS):
        sl = slice(hh * LANES, (hh + 1) * LANES)
        k_ref[:, sl] = (kn[:, sl] + kr).astype(BF16)
        q_ref[:, sl] = (qa[:, sl] * trig_q).astype(BF16)


def _mla_pre(x2d, pos2d, modkv, modq, consts, seq, tm):
    t, d = x2d.shape
    tps = seq // tm
    full = lambda a: pl.BlockSpec(a.shape, lambda i: (0,) * a.ndim)
    modspec = lambda m: pl.BlockSpec((None, 1, m.shape[-1]), lambda i: (i // tps, 0, 0))
    hq = MLA_HEADS * LANES
    return pl.pallas_call(
        _mla_pre_kernel,
        out_shape=(jax.ShapeDtypeStruct((t, hq), BF16), jax.ShapeDtypeStruct((t, hq), BF16),
                   jax.ShapeDtypeStruct((t // seq, tps, hq, tm), BF16)),
        grid=(t // tm,),
        in_specs=[pl.BlockSpec((tm, d), lambda i: (i, 0)), pl.BlockSpec((tm, 1), lambda i: (i, 0)),
                  modspec(modkv), modspec(modq)] + [full(a) for a in consts],
        out_specs=(pl.BlockSpec((tm, hq), lambda i: (i, 0)), pl.BlockSpec((tm, hq), lambda i: (i, 0)),
                   pl.BlockSpec((None, None, hq, tm), lambda i: (i // tps, i % tps, 0, 0))),
        compiler_params=_params(("parallel",)),
        name="mla_pre",
    )(x2d, pos2d, modkv, modq, *consts)


def _attn_kernel(q_ref, k_ref, vt_ref, o_ref, m_ref, acc_ref, *, blk):
    qi = pl.program_id(2)
    m_ref[...] = jnp.full_like(m_ref, -jnp.inf)
    acc_ref[...] = jnp.zeros_like(acc_ref)

    def block(j0, n_blk, diagonal=False):
        k0 = pl.multiple_of(j0 * blk, blk)
        width = n_blk * blk
        heads = range(ATTN_HEADS)
        sls = [slice(hh * LANES, (hh + 1) * LANES) for hh in heads]
        s = [_dot_nt(k_ref[pl.ds(k0, width), sl], q_ref[:, sl]) for sl in sls]
        if diagonal:
            key = lax.broadcasted_iota(jnp.int32, (width, blk), 0)
            qry = lax.broadcasted_iota(jnp.int32, (width, blk), 1) + (width - blk)
            s = [jnp.where(key <= qry, sh, -jnp.inf) for sh in s]
        for hh in heads:
            m_old = m_ref[hh]
            m_new = jnp.maximum(m_old, jnp.max(s[hh], axis=0, keepdims=True))
            p = jnp.exp2(s[hh] - m_new).astype(BF16)
            pv = _dot(vt_ref[j0, sls[hh], :], p[:blk])
            for i in range(1, n_blk):
                pv = pv + _dot(vt_ref[j0 + i, sls[hh], :], p[i * blk:(i + 1) * blk])
            acc_ref[hh] = acc_ref[hh] * jnp.exp2(m_old - m_new) + pv
            m_ref[hh] = m_new

    n_pairs = qi // 2

    def body(j, carry):
        block(4 * j, 4)
        return carry

    lax.fori_loop(0, n_pairs // 2, body, 0)

    @pl.when(n_pairs % 2 == 1)
    def _():
        block(2 * (n_pairs - 1), 2)

    @pl.when(qi % 2 == 0)
    def _():
        block(qi, 1, diagonal=True)

    @pl.when(qi % 2 == 1)
    def _():
        block(qi - 1, 2, diagonal=True)

    outs = []
    for hh in range(ATTN_HEADS):
        acc = acc_ref[hh].T
        outs.append((acc / acc[:, V_HEAD:V_HEAD + 1])[:, :V_HEAD])
    o_ref[...] = jnp.concatenate(outs, axis=1).astype(BF16)


def _attn(q, k, vt, blk):
    bsz, s, _ = q.shape
    hp = MLA_HEADS // ATTN_HEADS
    nh = ATTN_HEADS
    return pl.pallas_call(
        functools.partial(_attn_kernel, blk=blk),
        out_shape=jax.ShapeDtypeStruct((bsz, s, MLA_HEADS * V_HEAD), BF16),
        grid=(bsz, hp, s // blk),
        in_specs=[pl.BlockSpec((None, blk, nh * LANES), lambda b, h, qi: (b, qi, h)),
                  pl.BlockSpec((None, s, nh * LANES), lambda b, h, qi: (b, 0, h), pipeline_mode=pl.Buffered(1)),
                  pl.BlockSpec((None, s // blk, nh * LANES, blk), lambda b, h, qi: (b, 0, h, 0),
                               pipeline_mode=pl.Buffered(1))],
        out_specs=pl.BlockSpec((None, blk, nh * V_HEAD), lambda b, h, qi: (b, qi, h)),
        scratch_shapes=[pltpu.VMEM((nh, 1, blk), F32), pltpu.VMEM((nh, LANES, blk), F32)],
        compiler_params=_params(("parallel", "parallel", "arbitrary")),
        name="attn",
    )(q, k, vt)


def kernel(x, c, positions, ada_w, ada_b, norm_g, rwkv_mu, rwkv_w_rkv, rwkv_w0, rwkv_w1, rwkv_w2, rwkv_a0, rwkv_a1, rwkv_a2, rwkv_g1, rwkv_g2, rwkv_k_k, rwkv_k_a, rwkv_r_k, rwkv_lnx_w, rwkv_lnx_b, rwkv_w_o, kv_ada_w, kv_ada_b, kv_norm_g, mla_w_dkv, mla_g_kv, mla_w_uk, mla_w_uv, mla_w_kr, mla_w_dq, mla_g_q, mla_w_uq, mla_w_qr, mla_w_o, router_w, router_b, moe_w_gu, moe_w_down, final_g):
    bsz, seq, d = x.shape
    depth = ada_w.shape[0]
    n_a = rwkv_mu.shape[0]
    t = bsz * seq
    tm = min(512, seq)
    row = lambda a: a.reshape(1, -1).astype(F32)

    c_pad = jnp.pad(c, ((0, 8 - bsz), (0, 0)))
    mods = _ada(c_pad, ada_w.reshape(depth * 2, d, 3 * d), ada_b.reshape(depth * 2, 1, 3 * d))
    mods = mods[:, :bsz].reshape(depth, 2, bsz, 1, 3 * d)
    mod_kv = _ada(c_pad, kv_ada_w[None], kv_ada_b.reshape(1, 1, 2 * d))[0, :bsz].reshape(bsz, 1, 2 * d)

    idx = jnp.arange(MXU_DIM) // RWKV_HEAD_DIM
    bd = (idx[:, None] == idx[None, :]).astype(BF16)
    rw_t = router_w.T.astype(F32)
    rb_col = router_b.reshape(N_EXPERTS, 1).astype(F32)
    d_exp = moe_w_down.shape[2]
    f_all = N_EXPERTS * d_exp
    exp_id = jnp.arange(f_all) // d_exp
    expand = (jnp.arange(N_EXPERTS)[:, None] == exp_id[None, :]).astype(BF16)
    expand = jnp.concatenate([expand] * 3 + [jnp.zeros((LANES - 3 * N_EXPERTS, f_all), BF16)], axis=0)
    tm_moe = min(1024, seq)
    tri = (jnp.arange(tm_moe)[:, None] > jnp.arange(tm_moe)[None, :]).astype(BF16)

    def moe_weights(layer):
        wgu = moe_w_gu[layer]
        wg = wgu[:, :, :d_exp].transpose(1, 0, 2).reshape(d, f_all).astype(BF16)
        wu = wgu[:, :, d_exp:].transpose(1, 0, 2).reshape(d, f_all).astype(BF16)
        wd = moe_w_down[layer].reshape(f_all, d).astype(BF16)
        return wg, wu, wd

    inv_freq = ROPE_THETA ** (-jnp.arange(0, QK_ROPE, 2, dtype=F32) / QK_ROPE)
    freq = jnp.concatenate([jnp.zeros((QK_NOPE,), F32)] + [inv_freq] * 4).reshape(1, LANES)
    pos2d = positions.reshape(t, 1)

    x2d = x.reshape(t, d)
    assert depth - n_a == 1
    for layer in range(depth):
        mod_mix = mods[layer, 0]
        mod_ffn = mods[layer, 1]
        if layer < n_a:
            a = layer
            mu8 = jnp.pad(rwkv_mu[a], ((0, 2), (0, 0)))
            vecs = jnp.stack([rwkv_w0[a], rwkv_a0[a], rwkv_k_k[a], rwkv_k_a[a], rwkv_r_k[a].reshape(d),
                              jnp.zeros((d,), F32), jnp.zeros((d,), F32), jnp.zeros((d,), F32)])
            r, k, v, an, b, lw, gate, bonus = _rwkv_pre(
                x2d, mod_mix, row(norm_g[layer, 0]), mu8, rwkv_w_rkv[a].astype(BF16),
                rwkv_w1[a].astype(BF16), rwkv_w2[a].astype(BF16), rwkv_a1[a].astype(BF16), rwkv_a2[a].astype(BF16),
                rwkv_g1[a].astype(BF16), rwkv_g2[a].astype(BF16), vecs, bd, seq, tm)
            sh = lambda z: z.reshape(bsz, seq, d)
            ncs = min(16, seq // WKV_CHUNK)
            qm, y0, gm, cm = _wkv_a(sh(r), sh(k), sh(v), sh(an), sh(b), sh(lw), ncs)
            y = _wkv_b(qm, y0, gm, cm, min(4, seq // WKV_CHUNK)).reshape(t, d)
            x2d, h, route, grow = _post_call(
                _rwkv_post_kernel, "rwkv_post", [x2d, y, bonus, gate], [mod_mix, mod_ffn],
                [row(rwkv_lnx_w[a]), row(rwkv_lnx_b[a]), rwkv_w_o[a].astype(BF16), bd,
                 row(norm_g[layer, 1]), rw_t, rb_col], seq, tm)
        else:
            bl = layer - n_a
            scale = (QK_NOPE + QK_ROPE) ** -0.5 * LOG2_E
            hd = MLA_HEADS
            zpad = lambda w, lo, hi: jnp.pad(w, ((0, 0), (0, 0), (lo, hi)))
            kvl = mla_w_uk.shape[0]
            wuk = zpad(mla_w_uk, 0, LANES - QK_NOPE).reshape(kvl, hd * LANES).astype(BF16)
            wuvt = zpad(mla_w_uv, 0, LANES - V_HEAD).reshape(kvl, hd * LANES).T.astype(BF16)
            k1, k2 = jnp.split(mla_w_kr, 2, axis=-1)
            nope_pad = lambda w: jnp.pad(w, ((0, 0), (QK_NOPE, 0))).astype(BF16)
            wka = nope_pad(jnp.concatenate([k1, k2, k2, -k1], axis=-1))
            wkb = nope_pad(jnp.concatenate([-k2, k1, k1, k2], axis=-1))
            ql = mla_w_uq.shape[1]
            wq = jnp.concatenate([mla_w_uq[bl], mla_w_qr[bl], mla_w_qr[bl]], axis=-1) * scale
            wq = wq.reshape(ql, hd * LANES).astype(BF16)
            consts = [row(kv_norm_g), row(norm_g[layer, 0]), mla_w_dkv.astype(BF16), row(mla_g_kv), wuk, wuvt,
                      jnp.concatenate([wka, wkb], axis=1), mla_w_dq[bl].astype(BF16), row(mla_g_q[bl]), wq, freq]
            ta = min(ATTN_BLOCK, seq)
            q, kf, vt = _mla_pre(x2d, pos2d, mod_kv, mod_mix, consts, seq, ta)
            o = _attn(q.reshape(bsz, seq, -1), kf.reshape(bsz, seq, -1), vt, ta)
            x2d, h, route, grow = _post_call(
                _mla_post_kernel, "mla_post", [x2d, o.reshape(t, d)], [mod_mix, mod_ffn],
                [mla_w_o[bl].astype(BF16), row(norm_g[layer, 1]), rw_t, rb_col], seq, tm)
        wg, wu, wd = moe_weights(layer)
        x2d = _moe(x2d, h, route, grow, mod_ffn, wg, wu, wd, expand, tri, row(final_g), layer == depth - 1, seq,
                   tm_moe)
    return x2d.reshape(bsz, seq, d)
```

```python
import functools

import jax
import jax.numpy as jnp
from jax import lax
from jax.experimental import pallas as pl
from jax.experimental.pallas import tpu as pltpu

F32 = jnp.float32
BF16 = jnp.bfloat16

NORM_EPS = 1e-6
GN_EPS = 64e-5
ROPE_THETA = 10000.0
LOG2_E = 1.4426950408889634
RWKV_HEAD_DIM = 64
MLA_HEADS = 16
QK_NOPE = 64
QK_ROPE = 32
V_HEAD = 64
N_EXPERTS = 16
N_GROUPS = 4
EXPERTS_PER_GROUP = N_EXPERTS // N_GROUPS
TOP_K = 2

LANES = 128
MXU_DIM = 256
WKV_CHUNK = 64
ATTN_BLOCK = 512
ATTN_HEADS = 4
MOE_ROWS = 256
MOE_ROWS_WIDE = 320
MOE_EXTRA_ROWS = 64
VMEM_LIMIT = 56 * 1024 * 1024


def _dot(a, b):
    return jnp.dot(a, b, preferred_element_type=F32)


def _dot_nt(a, b):
    return lax.dot_general(a, b, (((1,), (1,)), ((), ())), preferred_element_type=F32)


def _split2(x):
    hi = x.astype(BF16)
    lo = (x - hi.astype(F32)).astype(BF16)
    return hi, lo


def _split3(x):
    hi = x.astype(BF16)
    r1 = x - hi.astype(F32)
    mid = r1.astype(BF16)
    lo = (r1 - mid.astype(F32)).astype(BF16)
    return hi, mid, lo


def _dot_x3(a, b):
    ah, al = _split2(a)
    bh, bl = _split2(b)
    return _dot(ah, bh) + (_dot(ah, bl) + _dot(al, bh))


def _rms(x):
    return x * lax.rsqrt(jnp.mean(x * x, axis=-1, keepdims=True) + NORM_EPS)


def _seg_bcast_sum(x, bd, two_terms=False):
    d = x.shape[-1]
    terms = _split2(x) if two_terms else (x.astype(BF16),)
    outs = []
    for j in range(d // MXU_DIM):
        sl = slice(j * MXU_DIM, (j + 1) * MXU_DIM)
        acc = _dot(terms[0][:, sl], bd)
        for t in terms[1:]:
            acc = acc + _dot(t[:, sl], bd)
        outs.append(acc)
    return jnp.concatenate(outs, axis=-1)


def _params(sem):
    return pltpu.CompilerParams(dimension_semantics=sem, vmem_limit_bytes=VMEM_LIMIT)


def _ada_kernel(c_ref, w_ref, b_ref, o_ref):
    c = c_ref[...]
    s = c * jax.nn.sigmoid(c)
    o_ref[0] = _dot_x3(s, w_ref[0]) + b_ref[0]


def _ada(c_pad, w, b, tn=1024):
    n, d, nn = w.shape
    return pl.pallas_call(
        _ada_kernel,
        out_shape=jax.ShapeDtypeStruct((n, 8, nn), F32),
        grid=(n, nn // tn),
        in_specs=[pl.BlockSpec((8, d), lambda i, j: (0, 0)),
                  pl.BlockSpec((1, d, tn), lambda i, j: (i, 0, j)),
                  pl.BlockSpec((1, 1, tn), lambda i, j: (i, 0, j))],
        out_specs=pl.BlockSpec((1, 8, tn), lambda i, j: (i, 0, j)),
        compiler_params=_params(("parallel", "parallel")),
        name="ada",
    )(c_pad, w, b)


def _rwkv_pre_kernel(x_ref, xp_ref, mod_ref, g_ref, mu_ref, wrkv_ref, w1_ref, w2_ref, a1_ref, a2_ref,
                     g1_ref, g2_ref, vec_ref, bd_ref,
                     r_ref, k_ref, v_ref, an_ref, b_ref, lw_ref, gate_ref, bonus_ref, *, tiles_per_seq):
    d = x_ref.shape[-1]
    tm = x_ref.shape[0]
    i = pl.program_id(0)
    shift = mod_ref[:, 0:d]
    scale = mod_ref[:, d:2 * d]
    gn = g_ref[...]

    def modulate(xv):
        return _rms(xv) * gn * (1.0 + scale) + shift

    h = modulate(x_ref[...])
    hp_last = modulate(xp_ref[...])[7:8, :]
    hp_last = jnp.where(i % tiles_per_seq == 0, 0.0, hp_last)
    row = lax.broadcasted_iota(jnp.int32, (tm, d), 0)
    h_prev = jnp.where(row == 0, hp_last, pltpu.roll(h, 1, axis=0))
    xx = h_prev - h

    def mix(j):
        return (h + xx * mu_ref[j:j + 1, :]).astype(BF16)

    w0 = vec_ref[0:1, :]
    a0 = vec_ref[1:2, :]
    k_k = vec_ref[2:3, :]
    k_a = vec_ref[3:4, :]
    r_k = vec_ref[4:5, :]
    bd = bd_ref[...]

    r = _dot(mix(0), wrkv_ref[0])
    k = _dot(mix(2), wrkv_ref[1])
    v = _dot(mix(3), wrkv_ref[2])
    z = w0 + _dot(jnp.tanh(_dot(mix(1), w1_ref[...])).astype(BF16), w2_ref[...])
    w_log = -(jnp.maximum(-z, 0.0) + jnp.log(1.0 + jnp.exp(-jnp.abs(z)))) - 0.5
    lw_ref[...] = -jnp.exp(w_log)
    a = jax.nn.sigmoid(a0 + _dot(_dot(mix(4), a1_ref[...]).astype(BF16), a2_ref[...]))
    gate_ref[...] = _dot(jax.nn.sigmoid(_dot(mix(5), g1_ref[...])).astype(BF16), g2_ref[...]).astype(BF16)

    kk = k * k_k
    kk = kk / jnp.maximum(jnp.sqrt(_seg_bcast_sum(kk * kk, bd)), 1e-12)
    km = k * (1.0 + (a - 1.0) * k_a)
    bonus_ref[...] = (_seg_bcast_sum(r * km * r_k, bd) * v).astype(BF16)
    r_ref[...] = r.astype(BF16)
    k_ref[...] = km.astype(BF16)
    v_ref[...] = v.astype(BF16)
    an_ref[...] = (-kk).astype(BF16)
    b_ref[...] = (kk * a).astype(BF16)


def _rwkv_pre(x2d, mod, norm_g, mu8, wrkv, w1, w2, a1, a2, g1, g2, vecs, bd, seq, tm):
    t, d = x2d.shape
    tps = seq // tm
    full = lambda a: pl.BlockSpec(a.shape, lambda i: (0,) * a.ndim)
    tok = pl.BlockSpec((tm, d), lambda i: (i, 0))
    out_bf = jax.ShapeDtypeStruct((t, d), BF16)
    return pl.pallas_call(
        functools.partial(_rwkv_pre_kernel, tiles_per_seq=tps),
        out_shape=(out_bf, out_bf, out_bf, out_bf, out_bf, jax.ShapeDtypeStruct((t, d), F32), out_bf, out_bf),
        grid=(t // tm,),
        in_specs=[tok,
                  pl.BlockSpec((8, d), lambda i: (jnp.maximum(i * (tm // 8) - 1, 0), 0)),
                  pl.BlockSpec((None, 1, mod.shape[-1]), lambda i: (i // tps, 0, 0)),
                  full(norm_g), full(mu8), full(wrkv), full(w1), full(w2), full(a1), full(a2),
                  full(g1), full(g2), full(vecs), full(bd)],
        out_specs=(tok,) * 8,
        compiler_params=_params(("parallel",)),
        name="rwkv_pre",
    )(x2d, x2d, mod, norm_g, mu8, wrkv, w1, w2, a1, a2, g1, g2, vecs, bd)


def _wkv_a_kernel(r_ref, k_ref, v_ref, an_ref, b_ref, lw_ref, q_ref, y0_ref, g_ref, c_ref, *, n_chunks):
    L = WKV_CHUNK
    L2 = 2 * L
    lane = lax.broadcasted_iota(jnp.int32, (L, LANES), 1)
    head0 = lane < RWKV_HEAD_DIM
    ri = lax.broadcasted_iota(jnp.int32, (L2, L2), 0)
    ci = lax.broadcasted_iota(jnp.int32, (L2, L2), 1)
    strict = ci < ri
    incl = ci <= ri
    eye = ci == ri
    tri = (lax.broadcasted_iota(jnp.int32, (L, L), 1) <= lax.broadcasted_iota(jnp.int32, (L, L), 0)).astype(BF16)
    tri3 = jnp.concatenate([tri] * 3, axis=1)

    def stack(xv):
        return jnp.concatenate([jnp.where(head0, xv, 0.0), jnp.where(head0, 0.0, xv)], axis=0)

    chunks = range(n_chunks)
    pre = []
    for c in chunks:
        sl = pl.ds(c * L, L)
        lw = lw_ref[sl, :]
        cum = _dot(tri3, jnp.concatenate(_split3(lw), axis=0))
        cl = cum[L - 1:L, :]
        g_in = jnp.exp(cum)
        g_ex = jnp.exp(cum - lw)
        g_inv = jnp.exp(-cum)
        g_end = jnp.exp(cl - cum)
        kf = k_ref[sl, :].astype(F32)
        bf = b_ref[sl, :].astype(F32)
        pre.append(dict(
            a2=stack(an_ref[sl, :].astype(F32) * g_ex).astype(BF16),
            r2=stack(r_ref[sl, :].astype(F32) * g_in),
            b2=stack(bf * g_inv).astype(BF16),
            k2=stack(kf * g_inv).astype(BF16),
            v2=stack(v_ref[sl, :].astype(F32)).astype(BF16),
            bg2t=stack(bf * g_end).T.astype(BF16),
            kg2t=stack(kf * g_end).T.astype(BF16),
            g_last=jnp.exp(cl)))

    gram = [_dot_nt(jnp.concatenate([p["a2"], p["r2"].astype(BF16)], axis=0),
                    jnp.concatenate([p["b2"], p["k2"]], axis=0)) for p in pre]
    m_ab = [jnp.where(strict, g[:L2, :L2], 0.0) for g in gram]
    m_ak = [jnp.where(strict, g[:L2, L2:], 0.0) for g in gram]
    m_rb = [jnp.where(incl, g[L2:, :L2], 0.0) for g in gram]
    m_rk = [jnp.where(incl, g[L2:, L2:], 0.0) for g in gram]
    mv = [_dot(jnp.concatenate([m_ak[c], m_rk[c]], axis=0).astype(BF16), pre[c]["v2"]) for c in chunks]
    kgv = [_dot(p["kg2t"], p["v2"]) for p in pre]

    nb = [n.astype(BF16) for n in m_ab]
    nk = [_dot(n, n) for n in nb]
    tinv = [jnp.where(eye, 1.0, n) for n in m_ab]
    for step in range(1, 6):
        nb = [n.astype(BF16) for n in nk]
        if step < 5:
            both = [_dot(nb[c], jnp.concatenate([nb[c], tinv[c].astype(BF16)], axis=1)) for c in chunks]
            nk = [m[:, :L2] for m in both]
            tinv = [tinv[c] + both[c][:, L2:] for c in chunks]
        else:
            tinv = [tinv[c] + _dot(nb[c], tinv[c].astype(BF16)) for c in chunks]

    wu = [_dot(tinv[c].astype(BF16), jnp.concatenate([pre[c]["a2"], mv[c][:L2].astype(BF16)], axis=1)) for c in chunks]
    x = [_dot(jnp.concatenate([m_rb[c].astype(BF16), pre[c]["bg2t"]], axis=0), wu[c].astype(BF16)) for c in chunks]

    fold = lambda m: m[:L] + m[L:]
    for c in chunks:
        q_ref[c, 0] = fold(pre[c]["r2"] + x[c][:L2, :L2]).astype(BF16)
        y0_ref[pl.ds(c * L, L), :] = fold(x[c][:L2, L2:] + mv[c][L2:])
        g_ref[c, 0] = fold(jnp.where(eye, jnp.broadcast_to(pre[c]["g_last"], (L2, L2)), 0.0)
                           + x[c][L2:, :L2]).astype(BF16)
        c_ref[c, 0] = fold(x[c][L2:, L2:] + kgv[c]).astype(BF16)


def _wkv_a(r, k, v, an, b, lw, n_chunks_step):
    bsz, s, d = r.shape
    L = WKV_CHUNK
    pairs = d // LANES
    nck = s // L
    rows = n_chunks_step * L
    tok = pl.BlockSpec((None, rows, LANES), lambda bi, ci, pi: (bi, ci, pi))
    mat = pl.BlockSpec((n_chunks_step, 1, L, LANES), lambda bi, ci, pi: (ci, bi * pairs + pi, 0, 0))
    mshape = jax.ShapeDtypeStruct((nck, bsz * pairs, L, LANES), BF16)
    return pl.pallas_call(
        functools.partial(_wkv_a_kernel, n_chunks=n_chunks_step),
        out_shape=(mshape, jax.ShapeDtypeStruct((bsz, s, d), F32), mshape, mshape),
        grid=(bsz, nck // n_chunks_step, pairs),
        in_specs=[tok] * 6,
        out_specs=(mat, tok, mat, mat),
        compiler_params=_params(("parallel", "parallel", "parallel")),
        name="wkv_a",
    )(r, k, v, an, b, lw)


def _wkv_b_kernel(q_ref, y0_ref, g_ref, c_ref, y_ref, s_ref, *, n_chunks, bsz, pairs):
    L = WKV_CHUNK
    head0 = lax.broadcasted_iota(jnp.int32, (L, LANES), 1) < RWKV_HEAD_DIM
    zero = jnp.zeros((L, LANES), BF16)

    def unfold(m):
        return jnp.concatenate([jnp.where(head0, m, zero), jnp.where(head0, zero, m)], axis=0)

    @pl.when(pl.program_id(0) == 0)
    def _():
        s_ref[...] = jnp.zeros_like(s_ref)

    for c in range(n_chunks):
        for bi in range(bsz):
            for pi in range(pairs):
                n = bi * pairs + pi
                sb = s_ref[n].astype(BF16)
                y2 = _dot(unfold(q_ref[c, n]), sb)
                y_ref[bi, pl.ds(c * L, L), pl.ds(pi * LANES, LANES)] = (
                    y2[:L] + y2[L:] + y0_ref[bi, pl.ds(c * L, L), pl.ds(pi * LANES, LANES)])
                s_ref[n] = _dot(unfold(g_ref[c, n]), sb) + unfold(c_ref[c, n]).astype(F32)


def _wkv_b(q, y0, g, cmat, n_chunks_step):
    nck, bp, L, _ = q.shape
    L2 = 2 * L
    bsz, s, d = y0.shape
    pairs = d // LANES
    rows = n_chunks_step * L
    mat = pl.BlockSpec((n_chunks_step, bp, L, LANES), lambda ci: (ci, 0, 0, 0))
    tok = pl.BlockSpec((bsz, rows, d), lambda ci: (0, ci, 0))
    return pl.pallas_call(
        functools.partial(_wkv_b_kernel, n_chunks=n_chunks_step, bsz=bsz, pairs=pairs),
        out_shape=jax.ShapeDtypeStruct((bsz, s, d), F32),
        grid=(nck // n_chunks_step,),
        in_specs=[mat, tok, mat, mat],
        out_specs=tok,
        scratch_shapes=[pltpu.VMEM((bp, L2, L2), F32)],
        compiler_params=_params(("arbitrary",)),
        name="wkv_b",
    )(q, y0, g, cmat)


def _router_combine(h, rw_ref, rb_ref):
    rw = rw_ref[...]
    hh, hl = _split2(h)
    wh, wl = _split2(rw)
    logits = _dot_nt(wh, hh) + (_dot_nt(wh, hl) + _dot_nt(wl, hh))
    aff = jax.nn.sigmoid(logits)
    sel = aff + rb_ref[...]
    affr = [aff[e:e + 1, :] for e in range(N_EXPERTS)]
    selr = [sel[e:e + 1, :] for e in range(N_EXPERTS)]
    top = []
    score = []
    for g in range(N_GROUPS):
        es = range(g * EXPERTS_PER_GROUP, (g + 1) * EXPERTS_PER_GROUP)
        sc = None
        for e in es:
            rank = None
            for o in es:
                if o == e:
                    continue
                beats = ((selr[o] >= selr[e]) if o < e else (selr[o] > selr[e])).astype(F32)
                rank = beats if rank is None else rank + beats
            t = rank < float(TOP_K)
            top.append(t)
            contrib = jnp.where(t, selr[e], 0.0)
            sc = contrib if sc is None else sc + contrib
        score.append(sc)
    best = score[0]
    for g in range(1, N_GROUPS):
        best = jnp.maximum(best, score[g])
    taken = None
    rows = []
    groups = []
    for g in range(N_GROUPS):
        is_best = score[g] == best
        if taken is not None:
            is_best = is_best & jnp.logical_not(taken)
        taken = is_best if taken is None else (taken | is_best)
        groups.append(is_best.astype(F32))
        for e in range(g * EXPERTS_PER_GROUP, (g + 1) * EXPERTS_PER_GROUP):
            rows.append(jnp.where(is_best & top[e], affr[e], 0.0))
    comb = jnp.concatenate(rows, axis=0)
    comb = comb / jnp.sum(comb, axis=0, keepdims=True)
    tm = comb.shape[1]
    grp = jnp.concatenate(groups, axis=0)
    pad = jnp.zeros((LANES - 3 * N_EXPERTS - N_GROUPS, tm), F32)
    route = jnp.concatenate(list(_split3(comb)) + [grp, pad], axis=0).T.astype(BF16)
    grow = jnp.concatenate([grp, jnp.zeros((8 - N_GROUPS, tm), F32)], axis=0)
    return route, grow


def _moe_prologue(x_new, mod2_ref, g2_ref, rw_ref, rb_ref, h_ref, route_ref, grow_ref):
    d = x_new.shape[-1]
    h = _rms(x_new) * g2_ref[...] * (1.0 + mod2_ref[:, d:2 * d]) + mod2_ref[:, 0:d]
    h_ref[...] = h.astype(BF16)
    route_ref[...], grow_ref[...] = _router_combine(h, rw_ref, rb_ref)


def _rwkv_post_kernel(x_ref, y_ref, bonus_ref, gate_ref, mod_ref, mod2_ref, lnw_ref, lnb_ref, wo_ref, bd_ref,
                      g2_ref, rw_ref, rb_ref, xo_ref, h_ref, route_ref, grow_ref):
    d = x_ref.shape[-1]
    bd = bd_ref[...]
    y = y_ref[...]
    inv_n = 1.0 / RWKV_HEAD_DIM
    yc = y - _seg_bcast_sum(y, bd, two_terms=True) * inv_n
    var = _seg_bcast_sum(yc * yc, bd) * inv_n
    yn = yc * lax.rsqrt(var + GN_EPS) * lnw_ref[...] + lnb_ref[...]
    o = (yn + bonus_ref[...].astype(F32)) * gate_ref[...].astype(F32)
    mixed = _dot(o.astype(BF16), wo_ref[...])
    x_new = x_ref[...] + mod_ref[:, 2 * d:3 * d] * mixed
    xo_ref[...] = x_new
    _moe_prologue(x_new, mod2_ref, g2_ref, rw_ref, rb_ref, h_ref, route_ref, grow_ref)


def _mla_post_kernel(x_ref, o_ref, mod_ref, mod2_ref, wo_ref, g2_ref, rw_ref, rb_ref,
                     xo_ref, h_ref, route_ref, grow_ref):
    d = x_ref.shape[-1]
    mixed = _dot(o_ref[...], wo_ref[...])
    x_new = x_ref[...] + mod_ref[:, 2 * d:3 * d] * mixed
    xo_ref[...] = x_new
    _moe_prologue(x_new, mod2_ref, g2_ref, rw_ref, rb_ref, h_ref, route_ref, grow_ref)


def _post_call(kern, name, tok_inputs, mods, consts, seq, tm):
    t, d = tok_inputs[0].shape
    tps = seq // tm
    full = lambda a: pl.BlockSpec(a.shape, lambda i: (0,) * a.ndim)
    tok = pl.BlockSpec((tm, d), lambda i: (i, 0))
    modspec = lambda m: pl.BlockSpec((None, 1, m.shape[-1]), lambda i: (i // tps, 0, 0))
    return pl.pallas_call(
        kern,
        out_shape=(jax.ShapeDtypeStruct((t, d), F32), jax.ShapeDtypeStruct((t, d), BF16),
                   jax.ShapeDtypeStruct((t, LANES), BF16), jax.ShapeDtypeStruct((8, t), F32)),
        grid=(t // tm,),
        in_specs=[tok] * len(tok_inputs) + [modspec(m) for m in mods] + [full(a) for a in consts],
        out_specs=(tok, tok, pl.BlockSpec((tm, LANES), lambda i: (i, 0)), pl.BlockSpec((8, tm), lambda i: (0, i))),
        compiler_params=_params(("parallel",)),
        name=name,
    )(*tok_inputs, *mods, *consts)


def _group_count_kernel(grow_ref, o_ref):
    o_ref[0] = jnp.broadcast_to(jnp.sum(grow_ref[...], axis=1, keepdims=True), o_ref.shape[1:])


def _group_counts(grow, tm):
    n_tiles = grow.shape[1] // tm
    sums = pl.pallas_call(
        _group_count_kernel,
        out_shape=jax.ShapeDtypeStruct((n_tiles, 8, LANES), F32),
        grid=(n_tiles,),
        in_specs=[pl.BlockSpec((8, tm), lambda i: (0, i))],
        out_specs=pl.BlockSpec((1, 8, LANES), lambda i: (i, 0, 0)),
        compiler_params=_params(("parallel",)),
        name="moe_counts",
    )(grow)
    return sums[:, :N_GROUPS, 0].astype(jnp.int32).reshape(-1)


def _moe_kernel(cnt_ref, x_ref, h_ref, route_ref, grow_ref, mod_ref, wg_ref, wu_ref, wd_ref, ex_ref, tri_ref, fg_ref,
                o_ref, acc_ref, posc_ref, posr_ref, *, final):
    d = x_ref.shape[-1]
    tm = x_ref.shape[0]
    i = pl.program_id(0)
    g = pl.program_id(1)
    grp_lane0 = 3 * N_EXPERTS

    @pl.when(g == 0)
    def _():
        acc_ref[...] = jnp.zeros_like(acc_ref)
        route = route_ref[...]
        posc_ref[...] = jnp.where(route > 0, _dot(tri_ref[...], route), -1.0)
        grow = grow_ref[...]
        grow16 = jnp.concatenate([grow, jnp.zeros_like(grow)], axis=0).astype(BF16)
        posr_ref[...] = jnp.where(grow > 0, _dot_nt(grow16, tri_ref[...])[:8], -1.0)

    n = cnt_ref[i * N_GROUPS + g]
    h = h_ref[...]
    route = route_ref[...]
    lane = lax.broadcasted_iota(jnp.int32, (tm, LANES), 1)
    pos_col = jnp.sum(jnp.where(lane == grp_lane0 + g, posc_ref[...], 0.0), axis=1, keepdims=True)
    pos_row = posr_ref[pl.ds(g, 1), :]

    def run_rows(r0, rows):
        rank_r = (r0 + lax.broadcasted_iota(jnp.int32, (rows, tm), 0)).astype(F32)
        sel = jnp.where(pos_row == rank_r, 1.0, 0.0).astype(BF16)
        hs = _dot(sel, h).astype(BF16)
        cb = _dot(sel, route).astype(BF16)
        gt = _dot(hs, wg_ref[...])
        up = _dot(hs, wu_ref[...])
        cs = _dot(cb, ex_ref[...])
        act = (gt * jax.nn.sigmoid(gt)) * up * cs
        ys = _dot(act.astype(BF16), wd_ref[...]).astype(BF16)
        rank_c = (r0 + lax.broadcasted_iota(jnp.int32, (tm, rows), 1)).astype(F32)
        sel_t = jnp.where(pos_col == rank_c, 1.0, 0.0).astype(BF16)
        acc_ref[...] += _dot(sel_t, ys)

    @pl.when((n > 0) & (n <= MOE_ROWS))
    def _():
        run_rows(0, MOE_ROWS)

    @pl.when(n > MOE_ROWS)
    def _():
        run_rows(0, MOE_ROWS_WIDE)

    def extra(b, carry):
        run_rows(MOE_ROWS_WIDE + b * MOE_EXTRA_ROWS, MOE_EXTRA_ROWS)
        return carry

    lax.fori_loop(0, (jnp.maximum(n - MOE_ROWS_WIDE, 0) + MOE_EXTRA_ROWS - 1) // MOE_EXTRA_ROWS, extra, 0)

    @pl.when(g == pl.num_programs(1) - 1)
    def _():
        x_new = x_ref[...] + mod_ref[:, 2 * d:3 * d] * acc_ref[...]
        o_ref[...] = _rms(x_new) * fg_ref[...] if final else x_new


def _moe(x2d, h, route, grow, mod, wg, wu, wd, expand, tri, final_g, final, seq, tm):
    t, d = x2d.shape
    fg = wg.shape[1] // N_GROUPS
    tps = seq // tm
    counts = _group_counts(grow, tm)
    tok = lambda w: pl.BlockSpec((tm, w), lambda i, g, c: (i, 0))
    return pl.pallas_call(
        functools.partial(_moe_kernel, final=final),
        out_shape=jax.ShapeDtypeStruct((t, d), F32),
        grid_spec=pltpu.PrefetchScalarGridSpec(
            num_scalar_prefetch=1,
            grid=(t // tm, N_GROUPS),
            in_specs=[tok(d), tok(d), tok(LANES),
                      pl.BlockSpec((8, tm), lambda i, g, c: (0, i)),
                      pl.BlockSpec((None, 1, mod.shape[-1]), lambda i, g, c: (i // tps, 0, 0)),
                      pl.BlockSpec((d, fg), lambda i, g, c: (0, g)),
                      pl.BlockSpec((d, fg), lambda i, g, c: (0, g)),
                      pl.BlockSpec((fg, d), lambda i, g, c: (g, 0)),
                      pl.BlockSpec((LANES, fg), lambda i, g, c: (0, g)),
                      pl.BlockSpec((tm, tm), lambda i, g, c: (0, 0)),
                      pl.BlockSpec((1, d), lambda i, g, c: (0, 0))],
            out_specs=tok(d),
            scratch_shapes=[pltpu.VMEM((tm, d), F32), pltpu.VMEM((tm, LANES), F32), pltpu.VMEM((8, tm), F32)]),
        compiler_params=_params(("parallel", "arbitrary")),
        name="moe",
    )(counts, x2d, h, route, grow, mod, wg, wu, wd, expand, tri, final_g)


def _mla_pre_kernel(x_ref, pos_ref, modkv_ref, modq_ref, gkv_ref, gq_ref, wdkv_ref, gckv_ref, wuk_ref, wuvt_ref,
                    wkab_ref, wdq_ref, gcq_ref, wq_ref, freq_ref,
                    q_ref, k_ref, vt_ref):
    d = x_ref.shape[-1]
    tm = x_ref.shape[0]
    xn = _rms(x_ref[...])
    ang = pos_ref[...].astype(F32) * freq_ref[...]
    cos = jnp.cos(ang)
    sin = jnp.sin(ang)
    lane = lax.broadcasted_iota(jnp.int32, (tm, LANES), 1)
    trig_q = jnp.where(lane < QK_NOPE + QK_ROPE, cos, sin)

    hkv = (xn * gkv_ref[...] * (1.0 + modkv_ref[:, d:2 * d]) + modkv_ref[:, 0:d]).astype(BF16)
    ckv = (_rms(_dot(hkv, wdkv_ref[...])) * gckv_ref[...]).astype(BF16)
    vt = _dot_nt(wuvt_ref[...], ckv)
    row = lax.broadcasted_iota(jnp.int32, vt.shape, 0)
    vt_ref[...] = jnp.where((row & (LANES - 1)) == V_HEAD, 1.0, vt).astype(BF16)
    kab = _dot(hkv, wkab_ref[...])
    kr = kab[:, :LANES] * cos + kab[:, LANES:] * sin
    kn = _dot(ckv, wuk_ref[...])

    hq = (xn * gq_ref[...] * (1.0 + modq_ref[:, d:2 * d]) + modq_ref[:, 0:d]).astype(BF16)
    cq = (_rms(_dot(hq, wdq_ref[...])) * gcq_ref[...]).astype(BF16)
    qa = _dot(cq, wq_ref[...])
    for hh in range(MLA_HEADS):
        sl = slice(hh * LANES, (hh + 1) * LANES)
        k_ref[:, sl] = (kn[:, sl] + kr).astype(BF16)
        q_ref[:, sl] = (qa[:, sl] * trig_q).astype(BF16)


def _mla_pre(x2d, pos2d, modkv, modq, consts, seq, tm):
    t, d = x2d.shape
    tps = seq // tm
    full = lambda a: pl.BlockSpec(a.shape, lambda i: (0,) * a.ndim)
    modspec = lambda m: pl.BlockSpec((None, 1, m.shape[-1]), lambda i: (i // tps, 0, 0))
    hq = MLA_HEADS * LANES
    return pl.pallas_call(
        _mla_pre_kernel,
        out_shape=(jax.ShapeDtypeStruct((t, hq), BF16), jax.ShapeDtypeStruct((t, hq), BF16),
                   jax.ShapeDtypeStruct((t // seq, tps, hq, tm), BF16)),
        grid=(t // tm,),
        in_specs=[pl.BlockSpec((tm, d), lambda i: (i, 0)), pl.BlockSpec((tm, 1), lambda i: (i, 0)),
                  modspec(modkv), modspec(modq)] + [full(a) for a in consts],
        out_specs=(pl.BlockSpec((tm, hq), lambda i: (i, 0)), pl.BlockSpec((tm, hq), lambda i: (i, 0)),
                   pl.BlockSpec((None, None, hq, tm), lambda i: (i // tps, i % tps, 0, 0))),
        compiler_params=_params(("parallel",)),
        name="mla_pre",
    )(x2d, pos2d, modkv, modq, *consts)


def _attn_kernel(q_ref, k_ref, vt_ref, o_ref, m_ref, acc_ref, *, blk):
    qi = pl.program_id(2)
    m_ref[...] = jnp.full_like(m_ref, -jnp.inf)
    acc_ref[...] = jnp.zeros_like(acc_ref)

    def block(j0, n_blk, diagonal=False):
        k0 = pl.multiple_of(j0 * blk, blk)
        width = n_blk * blk
        heads = range(ATTN_HEADS)
        sls = [slice(hh * LANES, (hh + 1) * LANES) for hh in heads]
        s = [_dot_nt(k_ref[pl.ds(k0, width), sl], q_ref[:, sl]) for sl in sls]
        if diagonal:
            key = lax.broadcasted_iota(jnp.int32, (width, blk), 0)
            qry = lax.broadcasted_iota(jnp.int32, (width, blk), 1) + (width - blk)
            s = [jnp.where(key <= qry, sh, -jnp.inf) for sh in s]
        for hh in heads:
            m_old = m_ref[hh]
            m_new = jnp.maximum(m_old, jnp.max(s[hh], axis=0, keepdims=True))
            p = jnp.exp2(s[hh] - m_new).astype(BF16)
            pv = _dot(vt_ref[j0, sls[hh], :], p[:blk])
            for i in range(1, n_blk):
                pv = pv + _dot(vt_ref[j0 + i, sls[hh], :], p[i * blk:(i + 1) * blk])
            acc_ref[hh] = acc_ref[hh] * jnp.exp2(m_old - m_new) + pv
            m_ref[hh] = m_new

    n_pairs = qi // 2

    def body(j, carry):
        block(4 * j, 4)
        return carry

    lax.fori_loop(0, n_pairs // 2, body, 0)

    @pl.when(n_pairs % 2 == 1)
    def _():
        block(2 * (n_pairs - 1), 2)

    @pl.when(qi % 2 == 0)
    def _():
        block(qi, 1, diagonal=True)

    @pl.when(qi % 2 == 1)
    def _():
        block(qi - 1, 2, diagonal=True)

    for pair in range(ATTN_HEADS // 2):
        halves = []
        for hh in (2 * pair, 2 * pair + 1):
            acc = acc_ref[hh]
            halves.append(acc[:V_HEAD] * (1.0 / acc[V_HEAD:V_HEAD + 1]))
        o_ref[:, pair * LANES:(pair + 1) * LANES] = jnp.concatenate(halves, axis=0).T.astype(BF16)


def _attn(q, k, vt, blk):
    bsz, s, _ = q.shape
    hp = MLA_HEADS // ATTN_HEADS
    nh = ATTN_HEADS
    return pl.pallas_call(
        functools.partial(_attn_kernel, blk=blk),
        out_shape=jax.ShapeDtypeStruct((bsz, s, MLA_HEADS * V_HEAD), BF16),
        grid=(bsz, hp, s // blk),
        in_specs=[pl.BlockSpec((None, blk, nh * LANES), lambda b, h, qi: (b, qi, h)),
                  pl.BlockSpec((None, s, nh * LANES), lambda b, h, qi: (b, 0, h)),
                  pl.BlockSpec((None, s // blk, nh * LANES, blk), lambda b, h, qi: (b, 0, h, 0))],
        out_specs=pl.BlockSpec((None, blk, nh * V_HEAD), lambda b, h, qi: (b, qi, h)),
        scratch_shapes=[pltpu.VMEM((nh, 1, blk), F32), pltpu.VMEM((nh, LANES, blk), F32)],
        compiler_params=_params(("parallel", "parallel", "arbitrary")),
        name="attn",
    )(q, k, vt)


def kernel(x, c, positions, ada_w, ada_b, norm_g, rwkv_mu, rwkv_w_rkv, rwkv_w0, rwkv_w1, rwkv_w2, rwkv_a0, rwkv_a1, rwkv_a2, rwkv_g1, rwkv_g2, rwkv_k_k, rwkv_k_a, rwkv_r_k, rwkv_lnx_w, rwkv_lnx_b, rwkv_w_o, kv_ada_w, kv_ada_b, kv_norm_g, mla_w_dkv, mla_g_kv, mla_w_uk, mla_w_uv, mla_w_kr, mla_w_dq, mla_g_q, mla_w_uq, mla_w_qr, mla_w_o, router_w, router_b, moe_w_gu, moe_w_down, final_g):
    bsz, seq, d = x.shape
    depth = ada_w.shape[0]
    n_a = rwkv_mu.shape[0]
    t = bsz * seq
    tm = min(512, seq)
    row = lambda a: a.reshape(1, -1).astype(F32)

    c_pad = jnp.pad(c, ((0, 8 - bsz), (0, 0)))
    mods = _ada(c_pad, ada_w.reshape(depth * 2, d, 3 * d), ada_b.reshape(depth * 2, 1, 3 * d))
    mods = mods[:, :bsz].reshape(depth, 2, bsz, 1, 3 * d)
    mod_kv = _ada(c_pad, kv_ada_w[None], kv_ada_b.reshape(1, 1, 2 * d))[0, :bsz].reshape(bsz, 1, 2 * d)

    idx = jnp.arange(MXU_DIM) // RWKV_HEAD_DIM
    bd = (idx[:, None] == idx[None, :]).astype(BF16)
    rw_t = router_w.T.astype(F32)
    rb_col = router_b.reshape(N_EXPERTS, 1).astype(F32)
    d_exp = moe_w_down.shape[2]
    f_all = N_EXPERTS * d_exp
    exp_id = jnp.arange(f_all) // d_exp
    expand = (jnp.arange(N_EXPERTS)[:, None] == exp_id[None, :]).astype(BF16)
    expand = jnp.concatenate([expand] * 3 + [jnp.zeros((LANES - 3 * N_EXPERTS, f_all), BF16)], axis=0)
    tm_moe = min(1024, seq)
    tri = (jnp.arange(tm_moe)[:, None] > jnp.arange(tm_moe)[None, :]).astype(BF16)

    def moe_weights(layer):
        wgu = moe_w_gu[layer]
        wg = wgu[:, :, :d_exp].transpose(1, 0, 2).reshape(d, f_all).astype(BF16)
        wu = wgu[:, :, d_exp:].transpose(1, 0, 2).reshape(d, f_all).astype(BF16)
        wd = moe_w_down[layer].reshape(f_all, d).astype(BF16)
        return wg, wu, wd

    inv_freq = ROPE_THETA ** (-jnp.arange(0, QK_ROPE, 2, dtype=F32) / QK_ROPE)
    freq = jnp.concatenate([jnp.zeros((QK_NOPE,), F32)] + [inv_freq] * 4).reshape(1, LANES)
    pos2d = positions.reshape(t, 1)

    x2d = x.reshape(t, d)
    assert depth - n_a == 1
    for layer in range(depth):
        mod_mix = mods[layer, 0]
        mod_ffn = mods[layer, 1]
        if layer < n_a:
            a = layer
            mu8 = jnp.pad(rwkv_mu[a], ((0, 2), (0, 0)))
            vecs = jnp.stack([rwkv_w0[a], rwkv_a0[a], rwkv_k_k[a], rwkv_k_a[a], rwkv_r_k[a].reshape(d),
                              jnp.zeros((d,), F32), jnp.zeros((d,), F32), jnp.zeros((d,), F32)])
            r, k, v, an, b, lw, gate, bonus = _rwkv_pre(
                x2d, mod_mix, row(norm_g[layer, 0]), mu8, rwkv_w_rkv[a].astype(BF16),
                rwkv_w1[a].astype(BF16), rwkv_w2[a].astype(BF16), rwkv_a1[a].astype(BF16), rwkv_a2[a].astype(BF16),
                rwkv_g1[a].astype(BF16), rwkv_g2[a].astype(BF16), vecs, bd, seq, tm)
            sh = lambda z: z.reshape(bsz, seq, d)
            ncs = min(16, seq // WKV_CHUNK)
            qm, y0, gm, cm = _wkv_a(sh(r), sh(k), sh(v), sh(an), sh(b), sh(lw), ncs)
            y = _wkv_b(qm, y0, gm, cm, min(4, seq // WKV_CHUNK)).reshape(t, d)
            x2d, h, route, grow = _post_call(
                _rwkv_post_kernel, "rwkv_post", [x2d, y, bonus, gate], [mod_mix, mod_ffn],
                [row(rwkv_lnx_w[a]), row(rwkv_lnx_b[a]), rwkv_w_o[a].astype(BF16), bd,
                 row(norm_g[layer, 1]), rw_t, rb_col], seq, tm)
        else:
            bl = layer - n_a
            scale = (QK_NOPE + QK_ROPE) ** -0.5 * LOG2_E
            hd = MLA_HEADS
            zpad = lambda w, lo, hi: jnp.pad(w, ((0, 0), (0, 0), (lo, hi)))
            kvl = mla_w_uk.shape[0]
            wuk = zpad(mla_w_uk, 0, LANES - QK_NOPE).reshape(kvl, hd * LANES).astype(BF16)
            wuvt = zpad(mla_w_uv, 0, LANES - V_HEAD).reshape(kvl, hd * LANES).T.astype(BF16)
            k1, k2 = jnp.split(mla_w_kr, 2, axis=-1)
            nope_pad = lambda w: jnp.pad(w, ((0, 0), (QK_NOPE, 0))).astype(BF16)
            wka = nope_pad(jnp.concatenate([k1, k2, k2, -k1], axis=-1))
            wkb = nope_pad(jnp.concatenate([-k2, k1, k1, k2], axis=-1))
            ql = mla_w_uq.shape[1]
            wq = jnp.concatenate([mla_w_uq[bl], mla_w_qr[bl], mla_w_qr[bl]], axis=-1) * scale
            wq = wq.reshape(ql, hd * LANES).astype(BF16)
            consts = [row(kv_norm_g), row(norm_g[layer, 0]), mla_w_dkv.astype(BF16), row(mla_g_kv), wuk, wuvt,
                      jnp.concatenate([wka, wkb], axis=1), mla_w_dq[bl].astype(BF16), row(mla_g_q[bl]), wq, freq]
            ta = min(ATTN_BLOCK, seq)
            q, kf, vt = _mla_pre(x2d, pos2d, mod_kv, mod_mix, consts, seq, ta)
            o = _attn(q.reshape(bsz, seq, -1), kf.reshape(bsz, seq, -1), vt, ta)
            x2d, h, route, grow = _post_call(
                _mla_post_kernel, "mla_post", [x2d, o.reshape(t, d)], [mod_mix, mod_ffn],
                [mla_w_o[bl].astype(BF16), row(norm_g[layer, 1]), rw_t, rb_col], seq, tm)
        wg, wu, wd = moe_weights(layer)
        x2d = _moe(x2d, h, route, grow, mod_ffn, wg, wu, wd, expand, tri, row(final_g), layer == depth - 1, seq,
                   tm_moe)
    return x2d.reshape(bsz, seq, d)
```

```python
import functools

import jax
import jax.numpy as jnp
from jax import lax
from jax.experimental import pallas as pl
from jax.experimental.pallas import tpu as pltpu

F32 = jnp.float32
BF16 = jnp.bfloat16

NORM_EPS = 1e-6
GN_EPS = 64e-5
ROPE_THETA = 10000.0
LOG2_E = 1.4426950408889634
RWKV_HEAD_DIM = 64
MLA_HEADS = 16
QK_NOPE = 64
QK_ROPE = 32
V_HEAD = 64
N_EXPERTS = 16
N_GROUPS = 4
EXPERTS_PER_GROUP = N_EXPERTS // N_GROUPS
TOP_K = 2

LANES = 128
MXU_DIM = 256
WKV_CHUNK = 64
ATTN_BLOCK = 512
ATTN_HEADS = 4
MOE_ROWS = 256
MOE_ROWS_WIDE = 320
MOE_EXTRA_ROWS = 64
VMEM_LIMIT = 56 * 1024 * 1024


def _dot(a, b):
    return jnp.dot(a, b, preferred_element_type=F32)


def _dot_nt(a, b):
    return lax.dot_general(a, b, (((1,), (1,)), ((), ())), preferred_element_type=F32)


def _split2(x):
    hi = x.astype(BF16)
    lo = (x - hi.astype(F32)).astype(BF16)
    return hi, lo


def _split3(x):
    hi = x.astype(BF16)
    r1 = x - hi.astype(F32)
    mid = r1.astype(BF16)
    lo = (r1 - mid.astype(F32)).astype(BF16)
    return hi, mid, lo


def _dot_x3(a, b):
    ah, al = _split2(a)
    bh, bl = _split2(b)
    return _dot(ah, bh) + (_dot(ah, bl) + _dot(al, bh))


def _rms(x):
    return x * lax.rsqrt(jnp.mean(x * x, axis=-1, keepdims=True) + NORM_EPS)


def _seg_bcast_sum(x, bd, two_terms=False):
    d = x.shape[-1]
    terms = _split2(x) if two_terms else (x.astype(BF16),)
    outs = []
    for j in range(d // MXU_DIM):
        sl = slice(j * MXU_DIM, (j + 1) * MXU_DIM)
        acc = _dot(terms[0][:, sl], bd)
        for t in terms[1:]:
            acc = acc + _dot(t[:, sl], bd)
        outs.append(acc)
    return jnp.concatenate(outs, axis=-1)


def _params(sem):
    return pltpu.CompilerParams(dimension_semantics=sem, vmem_limit_bytes=VMEM_LIMIT)


def _ada_kernel(c_ref, w_ref, b_ref, o_ref):
    c = c_ref[...]
    s = c * jax.nn.sigmoid(c)
    o_ref[0] = _dot_x3(s, w_ref[0]) + b_ref[0]


def _ada(c_pad, w, b, tn=1024):
    n, d, nn = w.shape
    return pl.pallas_call(
        _ada_kernel,
        out_shape=jax.ShapeDtypeStruct((n, 8, nn), F32),
        grid=(n, nn // tn),
        in_specs=[pl.BlockSpec((8, d), lambda i, j: (0, 0)),
                  pl.BlockSpec((1, d, tn), lambda i, j: (i, 0, j)),
                  pl.BlockSpec((1, 1, tn), lambda i, j: (i, 0, j))],
        out_specs=pl.BlockSpec((1, 8, tn), lambda i, j: (i, 0, j)),
        compiler_params=_params(("parallel", "parallel")),
        name="ada",
    )(c_pad, w, b)


def _rwkv_pre_kernel(x_ref, xp_ref, mod_ref, g_ref, mu_ref, wrkv_ref, w1_ref, w2_ref, a1_ref, a2_ref,
                     g1_ref, g2_ref, vec_ref, bd_ref,
                     r_ref, k_ref, v_ref, an_ref, b_ref, lw_ref, gate_ref, bonus_ref, *, tiles_per_seq):
    d = x_ref.shape[-1]
    tm = x_ref.shape[0]
    i = pl.program_id(0)
    shift = mod_ref[:, 0:d]
    scale = mod_ref[:, d:2 * d]
    gn = g_ref[...]

    def modulate(xv):
        return _rms(xv) * gn * (1.0 + scale) + shift

    h = modulate(x_ref[...])
    hp_last = modulate(xp_ref[...])[7:8, :]
    hp_last = jnp.where(i % tiles_per_seq == 0, 0.0, hp_last)
    row = lax.broadcasted_iota(jnp.int32, (tm, d), 0)
    h_prev = jnp.where(row == 0, hp_last, pltpu.roll(h, 1, axis=0))
    xx = h_prev - h

    def mix(j):
        return (h + xx * mu_ref[j:j + 1, :]).astype(BF16)

    w0 = vec_ref[0:1, :]
    a0 = vec_ref[1:2, :]
    k_k = vec_ref[2:3, :]
    k_a = vec_ref[3:4, :]
    r_k = vec_ref[4:5, :]
    bd = bd_ref[...]

    r = _dot(mix(0), wrkv_ref[0])
    k = _dot(mix(2), wrkv_ref[1])
    v = _dot(mix(3), wrkv_ref[2])
    z = w0 + _dot(jnp.tanh(_dot(mix(1), w1_ref[...])).astype(BF16), w2_ref[...])
    w_log = -(jnp.maximum(-z, 0.0) + jnp.log(1.0 + jnp.exp(-jnp.abs(z)))) - 0.5
    lw_ref[...] = -jnp.exp(w_log)
    a = jax.nn.sigmoid(a0 + _dot(_dot(mix(4), a1_ref[...]).astype(BF16), a2_ref[...]))
    gate_ref[...] = _dot(jax.nn.sigmoid(_dot(mix(5), g1_ref[...])).astype(BF16), g2_ref[...]).astype(BF16)

    kk = k * k_k
    kk = kk / jnp.maximum(jnp.sqrt(_seg_bcast_sum(kk * kk, bd)), 1e-12)
    km = k * (1.0 + (a - 1.0) * k_a)
    bonus_ref[...] = (_seg_bcast_sum(r * km * r_k, bd) * v).astype(BF16)
    r_ref[...] = r.astype(BF16)
    k_ref[...] = km.astype(BF16)
    v_ref[...] = v.astype(BF16)
    an_ref[...] = (-kk).astype(BF16)
    b_ref[...] = (kk * a).astype(BF16)


def _rwkv_pre(x2d, mod, norm_g, mu8, wrkv, w1, w2, a1, a2, g1, g2, vecs, bd, seq, tm):
    t, d = x2d.shape
    tps = seq // tm
    full = lambda a: pl.BlockSpec(a.shape, lambda i: (0,) * a.ndim)
    tok = pl.BlockSpec((tm, d), lambda i: (i, 0))
    out_bf = jax.ShapeDtypeStruct((t, d), BF16)
    return pl.pallas_call(
        functools.partial(_rwkv_pre_kernel, tiles_per_seq=tps),
        out_shape=(out_bf, out_bf, out_bf, out_bf, out_bf, jax.ShapeDtypeStruct((t, d), F32), out_bf, out_bf),
        grid=(t // tm,),
        in_specs=[tok,
                  pl.BlockSpec((8, d), lambda i: (jnp.maximum(i * (tm // 8) - 1, 0), 0)),
                  pl.BlockSpec((None, 1, mod.shape[-1]), lambda i: (i // tps, 0, 0)),
                  full(norm_g), full(mu8), full(wrkv), full(w1), full(w2), full(a1), full(a2),
                  full(g1), full(g2), full(vecs), full(bd)],
        out_specs=(tok,) * 8,
        compiler_params=_params(("parallel",)),
        name="rwkv_pre",
    )(x2d, x2d, mod, norm_g, mu8, wrkv, w1, w2, a1, a2, g1, g2, vecs, bd)


def _wkv_a_kernel(r_ref, k_ref, v_ref, an_ref, b_ref, lw_ref, q_ref, y0_ref, g_ref, c_ref, *, n_chunks):
    L = WKV_CHUNK
    L2 = 2 * L
    lane = lax.broadcasted_iota(jnp.int32, (L, LANES), 1)
    head0 = lane < RWKV_HEAD_DIM
    ri = lax.broadcasted_iota(jnp.int32, (L2, L2), 0)
    ci = lax.broadcasted_iota(jnp.int32, (L2, L2), 1)
    strict = ci < ri
    incl = ci <= ri
    eye = ci == ri
    tri = (lax.broadcasted_iota(jnp.int32, (L, L), 1) <= lax.broadcasted_iota(jnp.int32, (L, L), 0)).astype(BF16)
    tri3 = jnp.concatenate([tri] * 3, axis=1)

    def stack(xv):
        return jnp.concatenate([jnp.where(head0, xv, 0.0), jnp.where(head0, 0.0, xv)], axis=0)

    chunks = range(n_chunks)
    pre = []
    for c in chunks:
        sl = pl.ds(c * L, L)
        lw = lw_ref[sl, :]
        cum = _dot(tri3, jnp.concatenate(_split3(lw), axis=0))
        cl = cum[L - 1:L, :]
        g_in = jnp.exp(cum)
        g_ex = jnp.exp(cum - lw)
        g_inv = jnp.exp(-cum)
        g_end = jnp.exp(cl - cum)
        kf = k_ref[sl, :].astype(F32)
        bf = b_ref[sl, :].astype(F32)
        pre.append(dict(
            a2=stack(an_ref[sl, :].astype(F32) * g_ex).astype(BF16),
            r2=stack(r_ref[sl, :].astype(F32) * g_in),
            b2=stack(bf * g_inv).astype(BF16),
            k2=stack(kf * g_inv).astype(BF16),
            v2=stack(v_ref[sl, :].astype(F32)).astype(BF16),
            bg2t=stack(bf * g_end).T.astype(BF16),
            kg2t=stack(kf * g_end).T.astype(BF16),
            g_last=jnp.exp(cl)))

    gram = [_dot_nt(jnp.concatenate([p["a2"], p["r2"].astype(BF16)], axis=0),
                    jnp.concatenate([p["b2"], p["k2"]], axis=0)) for p in pre]
    m_ab = [jnp.where(strict, g[:L2, :L2], 0.0) for g in gram]
    m_ak = [jnp.where(strict, g[:L2, L2:], 0.0) for g in gram]
    m_rb = [jnp.where(incl, g[L2:, :L2], 0.0) for g in gram]
    m_rk = [jnp.where(incl, g[L2:, L2:], 0.0) for g in gram]
    mv = [_dot(jnp.concatenate([m_ak[c], m_rk[c]], axis=0).astype(BF16), pre[c]["v2"]) for c in chunks]
    kgv = [_dot(p["kg2t"], p["v2"]) for p in pre]

    nb = [n.astype(BF16) for n in m_ab]
    nk = [_dot(n, n) for n in nb]
    tinv = [jnp.where(eye, 1.0, n) for n in m_ab]
    for step in range(1, 6):
        nb = [n.astype(BF16) for n in nk]
        if step < 5:
            both = [_dot(nb[c], jnp.concatenate([nb[c], tinv[c].astype(BF16)], axis=1)) for c in chunks]
            nk = [m[:, :L2] for m in both]
            tinv = [tinv[c] + both[c][:, L2:] for c in chunks]
        else:
            tinv = [tinv[c] + _dot(nb[c], tinv[c].astype(BF16)) for c in chunks]

    wu = [_dot(tinv[c].astype(BF16), jnp.concatenate([pre[c]["a2"], mv[c][:L2].astype(BF16)], axis=1)) for c in chunks]
    x = [_dot(jnp.concatenate([m_rb[c].astype(BF16), pre[c]["bg2t"]], axis=0), wu[c].astype(BF16)) for c in chunks]

    fold = lambda m: m[:L] + m[L:]
    for c in chunks:
        q_ref[c, 0] = fold(pre[c]["r2"] + x[c][:L2, :L2]).astype(BF16)
        y0_ref[pl.ds(c * L, L), :] = fold(x[c][:L2, L2:] + mv[c][L2:])
        g_ref[c, 0] = fold(jnp.where(eye, jnp.broadcast_to(pre[c]["g_last"], (L2, L2)), 0.0)
                           + x[c][L2:, :L2]).astype(BF16)
        c_ref[c, 0] = fold(x[c][L2:, L2:] + kgv[c]).astype(BF16)


def _wkv_a(r, k, v, an, b, lw, n_chunks_step):
    bsz, s, d = r.shape
    L = WKV_CHUNK
    pairs = d // LANES
    nck = s // L
    rows = n_chunks_step * L
    tok = pl.BlockSpec((None, rows, LANES), lambda bi, ci, pi: (bi, ci, pi))
    mat = pl.BlockSpec((n_chunks_step, 1, L, LANES), lambda bi, ci, pi: (ci, bi * pairs + pi, 0, 0))
    mshape = jax.ShapeDtypeStruct((nck, bsz * pairs, L, LANES), BF16)
    return pl.pallas_call(
        functools.partial(_wkv_a_kernel, n_chunks=n_chunks_step),
        out_shape=(mshape, jax.ShapeDtypeStruct((bsz, s, d), F32), mshape, mshape),
        grid=(bsz, nck // n_chunks_step, pairs),
        in_specs=[tok] * 6,
        out_specs=(mat, tok, mat, mat),
        compiler_params=_params(("parallel", "parallel", "parallel")),
        name="wkv_a",
    )(r, k, v, an, b, lw)


def _wkv_b_kernel(q_ref, y0_ref, g_ref, c_ref, y_ref, s_ref, *, n_chunks, bsz, pairs):
    L = WKV_CHUNK
    head0 = lax.broadcasted_iota(jnp.int32, (L, LANES), 1) < RWKV_HEAD_DIM
    zero = jnp.zeros((L, LANES), BF16)

    def unfold(m):
        return jnp.concatenate([jnp.where(head0, m, zero), jnp.where(head0, zero, m)], axis=0)

    @pl.when(pl.program_id(0) == 0)
    def _():
        s_ref[...] = jnp.zeros_like(s_ref)

    for c in range(n_chunks):
        for bi in range(bsz):
            for pi in range(pairs):
                n = bi * pairs + pi
                sb = s_ref[n].astype(BF16)
                y2 = _dot(unfold(q_ref[c, n]), sb)
                y_ref[bi, pl.ds(c * L, L), pl.ds(pi * LANES, LANES)] = (
                    y2[:L] + y2[L:] + y0_ref[bi, pl.ds(c * L, L), pl.ds(pi * LANES, LANES)])
                s_ref[n] = _dot(unfold(g_ref[c, n]), sb) + unfold(c_ref[c, n]).astype(F32)


def _wkv_b(q, y0, g, cmat, n_chunks_step):
    nck, bp, L, _ = q.shape
    L2 = 2 * L
    bsz, s, d = y0.shape
    pairs = d // LANES
    rows = n_chunks_step * L
    mat = pl.BlockSpec((n_chunks_step, bp, L, LANES), lambda ci: (ci, 0, 0, 0))
    tok = pl.BlockSpec((bsz, rows, d), lambda ci: (0, ci, 0))
    return pl.pallas_call(
        functools.partial(_wkv_b_kernel, n_chunks=n_chunks_step, bsz=bsz, pairs=pairs),
        out_shape=jax.ShapeDtypeStruct((bsz, s, d), F32),
        grid=(nck // n_chunks_step,),
        in_specs=[mat, tok, mat, mat],
        out_specs=tok,
        scratch_shapes=[pltpu.VMEM((bp, L2, L2), F32)],
        compiler_params=_params(("arbitrary",)),
        name="wkv_b",
    )(q, y0, g, cmat)


def _router_combine(h, rw_ref, rb_ref):
    rw = rw_ref[...]
    hh, hl = _split2(h)
    wh, wl = _split2(rw)
    logits = _dot_nt(wh, hh) + (_dot_nt(wh, hl) + _dot_nt(wl, hh))
    aff = jax.nn.sigmoid(logits)
    sel = aff + rb_ref[...]
    affr = [aff[e:e + 1, :] for e in range(N_EXPERTS)]
    selr = [sel[e:e + 1, :] for e in range(N_EXPERTS)]
    top = []
    score = []
    for g in range(N_GROUPS):
        es = range(g * EXPERTS_PER_GROUP, (g + 1) * EXPERTS_PER_GROUP)
        sc = None
        for e in es:
            rank = None
            for o in es:
                if o == e:
                    continue
                beats = ((selr[o] >= selr[e]) if o < e else (selr[o] > selr[e])).astype(F32)
                rank = beats if rank is None else rank + beats
            t = rank < float(TOP_K)
            top.append(t)
            contrib = jnp.where(t, selr[e], 0.0)
            sc = contrib if sc is None else sc + contrib
        score.append(sc)
    best = score[0]
    for g in range(1, N_GROUPS):
        best = jnp.maximum(best, score[g])
    taken = None
    rows = []
    groups = []
    for g in range(N_GROUPS):
        is_best = score[g] == best
        if taken is not None:
            is_best = is_best & jnp.logical_not(taken)
        taken = is_best if taken is None else (taken | is_best)
        groups.append(is_best.astype(F32))
        for e in range(g * EXPERTS_PER_GROUP, (g + 1) * EXPERTS_PER_GROUP):
            rows.append(jnp.where(is_best & top[e], affr[e], 0.0))
    comb = jnp.concatenate(rows, axis=0)
    comb = comb / jnp.sum(comb, axis=0, keepdims=True)
    tm = comb.shape[1]
    grp = jnp.concatenate(groups, axis=0)
    pad = jnp.zeros((LANES - 3 * N_EXPERTS - N_GROUPS, tm), F32)
    route = jnp.concatenate(list(_split3(comb)) + [grp, pad], axis=0).T.astype(BF16)
    grow = jnp.concatenate([grp, jnp.zeros((8 - N_GROUPS, tm), F32)], axis=0)
    return route, grow


def _moe_prologue(x_new, mod2_ref, g2_ref, rw_ref, rb_ref, h_ref, route_ref, grow_ref):
    d = x_new.shape[-1]
    h = _rms(x_new) * g2_ref[...] * (1.0 + mod2_ref[:, d:2 * d]) + mod2_ref[:, 0:d]
    h_ref[...] = h.astype(BF16)
    route_ref[...], grow_ref[...] = _router_combine(h, rw_ref, rb_ref)


def _rwkv_post_kernel(x_ref, y_ref, bonus_ref, gate_ref, mod_ref, mod2_ref, lnw_ref, lnb_ref, wo_ref, bd_ref,
                      g2_ref, rw_ref, rb_ref, xo_ref, h_ref, route_ref, grow_ref):
    d = x_ref.shape[-1]
    bd = bd_ref[...]
    y = y_ref[...]
    inv_n = 1.0 / RWKV_HEAD_DIM
    yc = y - _seg_bcast_sum(y, bd, two_terms=True) * inv_n
    var = _seg_bcast_sum(yc * yc, bd) * inv_n
    yn = yc * lax.rsqrt(var + GN_EPS) * lnw_ref[...] + lnb_ref[...]
    o = (yn + bonus_ref[...].astype(F32)) * gate_ref[...].astype(F32)
    mixed = _dot(o.astype(BF16), wo_ref[...])
    x_new = x_ref[...] + mod_ref[:, 2 * d:3 * d] * mixed
    xo_ref[...] = x_new
    _moe_prologue(x_new, mod2_ref, g2_ref, rw_ref, rb_ref, h_ref, route_ref, grow_ref)


def _mla_post_kernel(x_ref, o_ref, mod_ref, mod2_ref, wo_ref, g2_ref, rw_ref, rb_ref,
                     xo_ref, h_ref, route_ref, grow_ref):
    d = x_ref.shape[-1]
    mixed = _dot(o_ref[...], wo_ref[...])
    x_new = x_ref[...] + mod_ref[:, 2 * d:3 * d] * mixed
    xo_ref[...] = x_new
    _moe_prologue(x_new, mod2_ref, g2_ref, rw_ref, rb_ref, h_ref, route_ref, grow_ref)


def _post_call(kern, name, tok_inputs, mods, consts, seq, tm):
    t, d = tok_inputs[0].shape
    tps = seq // tm
    full = lambda a: pl.BlockSpec(a.shape, lambda i: (0,) * a.ndim)
    tok = pl.BlockSpec((tm, d), lambda i: (i, 0))
    modspec = lambda m: pl.BlockSpec((None, 1, m.shape[-1]), lambda i: (i // tps, 0, 0))
    return pl.pallas_call(
        kern,
        out_shape=(jax.ShapeDtypeStruct((t, d), F32), jax.ShapeDtypeStruct((t, d), BF16),
                   jax.ShapeDtypeStruct((t, LANES), BF16), jax.ShapeDtypeStruct((8, t), F32)),
        grid=(t // tm,),
        in_specs=[tok] * len(tok_inputs) + [modspec(m) for m in mods] + [full(a) for a in consts],
        out_specs=(tok, tok, pl.BlockSpec((tm, LANES), lambda i: (i, 0)), pl.BlockSpec((8, tm), lambda i: (0, i))),
        compiler_params=_params(("parallel",)),
        name=name,
    )(*tok_inputs, *mods, *consts)


def _group_count_kernel(grow_ref, o_ref):
    o_ref[0] = jnp.broadcast_to(jnp.sum(grow_ref[...], axis=1, keepdims=True), o_ref.shape[1:])


def _group_counts(grow, tm):
    n_tiles = grow.shape[1] // tm
    sums = pl.pallas_call(
        _group_count_kernel,
        out_shape=jax.ShapeDtypeStruct((n_tiles, 8, LANES), F32),
        grid=(n_tiles,),
        in_specs=[pl.BlockSpec((8, tm), lambda i: (0, i))],
        out_specs=pl.BlockSpec((1, 8, LANES), lambda i: (i, 0, 0)),
        compiler_params=_params(("parallel",)),
        name="moe_counts",
    )(grow)
    return sums[:, :N_GROUPS, 0].astype(jnp.int32).reshape(-1)


def _moe_kernel(cnt_ref, x_ref, h_ref, route_ref, grow_ref, mod_ref, wg_ref, wu_ref, wd_ref, ex_ref, tri_ref, fg_ref,
                o_ref, acc_ref, posc_ref, posr_ref, *, final):
    d = x_ref.shape[-1]
    tm = x_ref.shape[0]
    i = pl.program_id(0)
    g = pl.program_id(1)
    grp_lane0 = 3 * N_EXPERTS

    @pl.when(g == 0)
    def _():
        acc_ref[...] = jnp.zeros_like(acc_ref)
        route = route_ref[...]
        posc_ref[...] = jnp.where(route > 0, _dot(tri_ref[...], route), -1.0)
        grow = grow_ref[...]
        grow16 = jnp.concatenate([grow, jnp.zeros_like(grow)], axis=0).astype(BF16)
        posr_ref[...] = jnp.where(grow > 0, _dot_nt(grow16, tri_ref[...])[:8], -1.0)

    n = cnt_ref[i * N_GROUPS + g]
    h = h_ref[...]
    route = route_ref[...]
    lane = lax.broadcasted_iota(jnp.int32, (tm, LANES), 1)
    pos_col = jnp.sum(jnp.where(lane == grp_lane0 + g, posc_ref[...], 0.0), axis=1, keepdims=True)
    pos_row = posr_ref[pl.ds(g, 1), :]

    def run_rows(r0, rows):
        rank_r = (r0 + lax.broadcasted_iota(jnp.int32, (rows, tm), 0)).astype(F32)
        sel = jnp.where(pos_row == rank_r, 1.0, 0.0).astype(BF16)
        hs = _dot(sel, h).astype(BF16)
        cb = _dot(sel, route).astype(BF16)
        gt = _dot(hs, wg_ref[...])
        up = _dot(hs, wu_ref[...])
        cs = _dot(cb, ex_ref[...])
        act = (gt * jax.nn.sigmoid(gt)) * up * cs
        ys = _dot(act.astype(BF16), wd_ref[...]).astype(BF16)
        rank_c = (r0 + lax.broadcasted_iota(jnp.int32, (tm, rows), 1)).astype(F32)
        sel_t = jnp.where(pos_col == rank_c, 1.0, 0.0).astype(BF16)
        acc_ref[...] += _dot(sel_t, ys)

    @pl.when((n > 0) & (n <= MOE_ROWS))
    def _():
        run_rows(0, MOE_ROWS)

    @pl.when(n > MOE_ROWS)
    def _():
        run_rows(0, MOE_ROWS_WIDE)

    def extra(b, carry):
        run_rows(MOE_ROWS_WIDE + b * MOE_EXTRA_ROWS, MOE_EXTRA_ROWS)
        return carry

    lax.fori_loop(0, (jnp.maximum(n - MOE_ROWS_WIDE, 0) + MOE_EXTRA_ROWS - 1) // MOE_EXTRA_ROWS, extra, 0)

    @pl.when(g == pl.num_programs(1) - 1)
    def _():
        x_new = x_ref[...] + mod_ref[:, 2 * d:3 * d] * acc_ref[...]
        o_ref[...] = _rms(x_new) * fg_ref[...] if final else x_new


def _moe(x2d, h, route, grow, mod, wg, wu, wd, expand, tri, final_g, final, seq, tm):
    t, d = x2d.shape
    fg = wg.shape[1] // N_GROUPS
    tps = seq // tm
    counts = _group_counts(grow, tm)
    tok = lambda w: pl.BlockSpec((tm, w), lambda i, g, c: (i, 0))
    return pl.pallas_call(
        functools.partial(_moe_kernel, final=final),
        out_shape=jax.ShapeDtypeStruct((t, d), F32),
        grid_spec=pltpu.PrefetchScalarGridSpec(
            num_scalar_prefetch=1,
            grid=(t // tm, N_GROUPS),
            in_specs=[tok(d), tok(d), tok(LANES),
                      pl.BlockSpec((8, tm), lambda i, g, c: (0, i)),
                      pl.BlockSpec((None, 1, mod.shape[-1]), lambda i, g, c: (i // tps, 0, 0)),
                      pl.BlockSpec((d, fg), lambda i, g, c: (0, g)),
                      pl.BlockSpec((d, fg), lambda i, g, c: (0, g)),
                      pl.BlockSpec((fg, d), lambda i, g, c: (g, 0)),
                      pl.BlockSpec((LANES, fg), lambda i, g, c: (0, g)),
                      pl.BlockSpec((tm, tm), lambda i, g, c: (0, 0)),
                      pl.BlockSpec((1, d), lambda i, g, c: (0, 0))],
            out_specs=tok(d),
            scratch_shapes=[pltpu.VMEM((tm, d), F32), pltpu.VMEM((tm, LANES), F32), pltpu.VMEM((8, tm), F32)]),
        compiler_params=_params(("parallel", "arbitrary")),
        name="moe",
    )(counts, x2d, h, route, grow, mod, wg, wu, wd, expand, tri, final_g)


def _mla_pre_kernel(x_ref, pos_ref, modkv_ref, modq_ref, gkv_ref, gq_ref, wdkv_ref, gckv_ref, wuk_ref, wuvt_ref,
                    wkab_ref, wdq_ref, gcq_ref, wq_ref, freq_ref,
                    q_ref, k_ref, vt_ref):
    d = x_ref.shape[-1]
    tm = x_ref.shape[0]
    xn = _rms(x_ref[...])
    ang = freq_ref[...] * pos_ref[...].astype(F32)
    cos_t = jnp.cos(ang)
    sin_t = jnp.sin(ang)
    one_t = jnp.ones((QK_NOPE, tm), F32)
    zero_t = jnp.zeros((QK_NOPE, tm), F32)
    trig_q = jnp.concatenate([one_t, cos_t, cos_t, sin_t, sin_t], axis=0).T
    cos = jnp.concatenate([one_t, cos_t, cos_t, cos_t, cos_t], axis=0).T
    sin = jnp.concatenate([zero_t, sin_t, sin_t, sin_t, sin_t], axis=0).T

    hkv = (xn * gkv_ref[...] * (1.0 + modkv_ref[:, d:2 * d]) + modkv_ref[:, 0:d]).astype(BF16)
    ckv = (_rms(_dot(hkv, wdkv_ref[...])) * gckv_ref[...]).astype(BF16)
    vt = _dot_nt(wuvt_ref[...], ckv)
    row = lax.broadcasted_iota(jnp.int32, vt.shape, 0)
    vt_ref[...] = jnp.where((row & (LANES - 1)) == V_HEAD, 1.0, vt).astype(BF16)
    kab = _dot(hkv, wkab_ref[...])
    kr = kab[:, :LANES] * cos + kab[:, LANES:] * sin
    kn = _dot(ckv, wuk_ref[...])

    hq = (xn * gq_ref[...] * (1.0 + modq_ref[:, d:2 * d]) + modq_ref[:, 0:d]).astype(BF16)
    cq = (_rms(_dot(hq, wdq_ref[...])) * gcq_ref[...]).astype(BF16)
    qa = _dot(cq, wq_ref[...])
    for hh in range(MLA_HEADS):
        sl = slice(hh * LANES, (hh + 1) * LANES)
        k_ref[:, sl] = (kn[:, sl] + kr).astype(BF16)
        q_ref[:, sl] = (qa[:, sl] * trig_q).astype(BF16)


def _mla_pre(x2d, pos2d, modkv, modq, consts, seq, tm):
    t, d = x2d.shape
    tps = seq // tm
    full = lambda a: pl.BlockSpec(a.shape, lambda i: (0,) * a.ndim)
    modspec = lambda m: pl.BlockSpec((None, 1, m.shape[-1]), lambda i: (i // tps, 0, 0))
    hq = MLA_HEADS * LANES
    return pl.pallas_call(
        _mla_pre_kernel,
        out_shape=(jax.ShapeDtypeStruct((t, hq), BF16), jax.ShapeDtypeStruct((t, hq), BF16),
                   jax.ShapeDtypeStruct((t // seq, tps, hq, tm), BF16)),
        grid=(t // tm,),
        in_specs=[pl.BlockSpec((tm, d), lambda i: (i, 0)), pl.BlockSpec((1, tm), lambda i: (0, i)),
                  modspec(modkv), modspec(modq)] + [full(a) for a in consts],
        out_specs=(pl.BlockSpec((tm, hq), lambda i: (i, 0)), pl.BlockSpec((tm, hq), lambda i: (i, 0)),
                   pl.BlockSpec((None, None, hq, tm), lambda i: (i // tps, i % tps, 0, 0))),
        compiler_params=_params(("parallel",)),
        name="mla_pre",
    )(x2d, pos2d, modkv, modq, *consts)


def _attn_kernel(q_ref, k_ref, vt_ref, o_ref, m_ref, acc_ref, *, blk):
    qi = pl.program_id(2)
    m_ref[...] = jnp.full_like(m_ref, -jnp.inf)
    acc_ref[...] = jnp.zeros_like(acc_ref)

    def block(j0, n_blk, diagonal=False):
        k0 = pl.multiple_of(j0 * blk, blk)
        width = n_blk * blk
        heads = range(ATTN_HEADS)
        sls = [slice(hh * LANES, (hh + 1) * LANES) for hh in heads]
        s = [_dot_nt(k_ref[pl.ds(k0, width), sl], q_ref[:, sl]) for sl in sls]
        if diagonal:
            key = lax.broadcasted_iota(jnp.int32, (width, blk), 0)
            qry = lax.broadcasted_iota(jnp.int32, (width, blk), 1) + (width - blk)
            s = [jnp.where(key <= qry, sh, -jnp.inf) for sh in s]
        for hh in heads:
            m_old = m_ref[hh]
            m_new = jnp.maximum(m_old, jnp.max(s[hh], axis=0, keepdims=True))
            p = jnp.exp2(s[hh] - m_new).astype(BF16)
            pv = _dot(vt_ref[j0, sls[hh], :], p[:blk])
            for i in range(1, n_blk):
                pv = pv + _dot(vt_ref[j0 + i, sls[hh], :], p[i * blk:(i + 1) * blk])
            acc_ref[hh] = acc_ref[hh] * jnp.exp2(m_old - m_new) + pv
            m_ref[hh] = m_new

    n_pairs = qi // 2

    def body(j, carry):
        block(4 * j, 4)
        return carry

    lax.fori_loop(0, n_pairs // 2, body, 0)

    @pl.when(n_pairs % 2 == 1)
    def _():
        block(2 * (n_pairs - 1), 2)

    @pl.when(qi % 2 == 0)
    def _():
        block(qi, 1, diagonal=True)

    @pl.when(qi % 2 == 1)
    def _():
        block(qi - 1, 2, diagonal=True)

    for pair in range(ATTN_HEADS // 2):
        halves = []
        for hh in (2 * pair, 2 * pair + 1):
            acc = acc_ref[hh]
            halves.append(acc[:V_HEAD] * (1.0 / acc[V_HEAD:V_HEAD + 1]))
        o_ref[:, pair * LANES:(pair + 1) * LANES] = jnp.concatenate(halves, axis=0).T.astype(BF16)


def _attn(q, k, vt, blk):
    bsz, s, _ = q.shape
    hp = MLA_HEADS // ATTN_HEADS
    nh = ATTN_HEADS
    return pl.pallas_call(
        functools.partial(_attn_kernel, blk=blk),
        out_shape=jax.ShapeDtypeStruct((bsz, s, MLA_HEADS * V_HEAD), BF16),
        grid=(bsz, hp, s // blk),
        in_specs=[pl.BlockSpec((None, blk, nh * LANES), lambda b, h, qi: (b, qi, h)),
                  pl.BlockSpec((None, s, nh * LANES), lambda b, h, qi: (b, 0, h)),
                  pl.BlockSpec((None, s // blk, nh * LANES, blk), lambda b, h, qi: (b, 0, h, 0))],
        out_specs=pl.BlockSpec((None, blk, nh * V_HEAD), lambda b, h, qi: (b, qi, h)),
        scratch_shapes=[pltpu.VMEM((nh, 1, blk), F32), pltpu.VMEM((nh, LANES, blk), F32)],
        compiler_params=_params(("parallel", "parallel", "arbitrary")),
        name="attn",
    )(q, k, vt)


def kernel(x, c, positions, ada_w, ada_b, norm_g, rwkv_mu, rwkv_w_rkv, rwkv_w0, rwkv_w1, rwkv_w2, rwkv_a0, rwkv_a1, rwkv_a2, rwkv_g1, rwkv_g2, rwkv_k_k, rwkv_k_a, rwkv_r_k, rwkv_lnx_w, rwkv_lnx_b, rwkv_w_o, kv_ada_w, kv_ada_b, kv_norm_g, mla_w_dkv, mla_g_kv, mla_w_uk, mla_w_uv, mla_w_kr, mla_w_dq, mla_g_q, mla_w_uq, mla_w_qr, mla_w_o, router_w, router_b, moe_w_gu, moe_w_down, final_g):
    bsz, seq, d = x.shape
    depth = ada_w.shape[0]
    n_a = rwkv_mu.shape[0]
    t = bsz * seq
    tm = min(512, seq)
    row = lambda a: a.reshape(1, -1).astype(F32)

    c_pad = jnp.pad(c, ((0, 8 - bsz), (0, 0)))
    mods = _ada(c_pad, ada_w.reshape(depth * 2, d, 3 * d), ada_b.reshape(depth * 2, 1, 3 * d))
    mods = mods[:, :bsz].reshape(depth, 2, bsz, 1, 3 * d)
    mod_kv = _ada(c_pad, kv_ada_w[None], kv_ada_b.reshape(1, 1, 2 * d))[0, :bsz].reshape(bsz, 1, 2 * d)

    idx = jnp.arange(MXU_DIM) // RWKV_HEAD_DIM
    bd = (idx[:, None] == idx[None, :]).astype(BF16)
    rw_t = router_w.T.astype(F32)
    rb_col = router_b.reshape(N_EXPERTS, 1).astype(F32)
    d_exp = moe_w_down.shape[2]
    f_all = N_EXPERTS * d_exp
    exp_id = jnp.arange(f_all) // d_exp
    expand = (jnp.arange(N_EXPERTS)[:, None] == exp_id[None, :]).astype(BF16)
    expand = jnp.concatenate([expand] * 3 + [jnp.zeros((LANES - 3 * N_EXPERTS, f_all), BF16)], axis=0)
    tm_moe = min(1024, seq)
    tri = (jnp.arange(tm_moe)[:, None] > jnp.arange(tm_moe)[None, :]).astype(BF16)

    def moe_weights(layer):
        wgu = moe_w_gu[layer]
        wg = wgu[:, :, :d_exp].transpose(1, 0, 2).reshape(d, f_all).astype(BF16)
        wu = wgu[:, :, d_exp:].transpose(1, 0, 2).reshape(d, f_all).astype(BF16)
        wd = moe_w_down[layer].reshape(f_all, d).astype(BF16)
        return wg, wu, wd

    inv_freq = ROPE_THETA ** (-jnp.arange(0, QK_ROPE, 2, dtype=F32) / QK_ROPE)
    freq = inv_freq.reshape(QK_ROPE // 2, 1)
    pos2d = positions.reshape(1, t)

    x2d = x.reshape(t, d)
    assert depth - n_a == 1
    for layer in range(depth):
        mod_mix = mods[layer, 0]
        mod_ffn = mods[layer, 1]
        if layer < n_a:
            a = layer
            mu8 = jnp.pad(rwkv_mu[a], ((0, 2), (0, 0)))
            vecs = jnp.stack([rwkv_w0[a], rwkv_a0[a], rwkv_k_k[a], rwkv_k_a[a], rwkv_r_k[a].reshape(d),
                              jnp.zeros((d,), F32), jnp.zeros((d,), F32), jnp.zeros((d,), F32)])
            r, k, v, an, b, lw, gate, bonus = _rwkv_pre(
                x2d, mod_mix, row(norm_g[layer, 0]), mu8, rwkv_w_rkv[a].astype(BF16),
                rwkv_w1[a].astype(BF16), rwkv_w2[a].astype(BF16), rwkv_a1[a].astype(BF16), rwkv_a2[a].astype(BF16),
                rwkv_g1[a].astype(BF16), rwkv_g2[a].astype(BF16), vecs, bd, seq, tm)
            sh = lambda z: z.reshape(bsz, seq, d)
            ncs = min(16, seq // WKV_CHUNK)
            qm, y0, gm, cm = _wkv_a(sh(r), sh(k), sh(v), sh(an), sh(b), sh(lw), ncs)
            y = _wkv_b(qm, y0, gm, cm, min(4, seq // WKV_CHUNK)).reshape(t, d)
            x2d, h, route, grow = _post_call(
                _rwkv_post_kernel, "rwkv_post", [x2d, y, bonus, gate], [mod_mix, mod_ffn],
                [row(rwkv_lnx_w[a]), row(rwkv_lnx_b[a]), rwkv_w_o[a].astype(BF16), bd,
                 row(norm_g[layer, 1]), rw_t, rb_col], seq, tm)
        else:
            bl = layer - n_a
            scale = (QK_NOPE + QK_ROPE) ** -0.5 * LOG2_E
            hd = MLA_HEADS
            zpad = lambda w, lo, hi: jnp.pad(w, ((0, 0), (0, 0), (lo, hi)))
            kvl = mla_w_uk.shape[0]
            wuk = zpad(mla_w_uk, 0, LANES - QK_NOPE).reshape(kvl, hd * LANES).astype(BF16)
            wuvt = zpad(mla_w_uv, 0, LANES - V_HEAD).reshape(kvl, hd * LANES).T.astype(BF16)
            k1, k2 = jnp.split(mla_w_kr, 2, axis=-1)
            nope_pad = lambda w: jnp.pad(w, ((0, 0), (QK_NOPE, 0))).astype(BF16)
            wka = nope_pad(jnp.concatenate([k1, k2, k2, -k1], axis=-1))
            wkb = nope_pad(jnp.concatenate([-k2, k1, k1, k2], axis=-1))
            ql = mla_w_uq.shape[1]
            wq = jnp.concatenate([mla_w_uq[bl], mla_w_qr[bl], mla_w_qr[bl]], axis=-1) * scale
            wq = wq.reshape(ql, hd * LANES).astype(BF16)
            consts = [row(kv_norm_g), row(norm_g[layer, 0]), mla_w_dkv.astype(BF16), row(mla_g_kv), wuk, wuvt,
                      jnp.concatenate([wka, wkb], axis=1), mla_w_dq[bl].astype(BF16), row(mla_g_q[bl]), wq, freq]
            ta = min(ATTN_BLOCK, seq)
            q, kf, vt = _mla_pre(x2d, pos2d, mod_kv, mod_mix, consts, seq, ta)
            o = _attn(q.reshape(bsz, seq, -1), kf.reshape(bsz, seq, -1), vt, ta)
            x2d, h, route, grow = _post_call(
                _mla_post_kernel, "mla_post", [x2d, o.reshape(t, d)], [mod_mix, mod_ffn],
                [mla_w_o[bl].astype(BF16), row(norm_g[layer, 1]), rw_t, rb_col], seq, tm)
        wg, wu, wd = moe_weights(layer)
        x2d = _moe(x2d, h, route, grow, mod_ffn, wg, wu, wd, expand, tri, row(final_g), layer == depth - 1, seq,
                   tm_moe)
    return x2d.reshape(bsz, seq, d)
```
